```python
import jax, jax.numpy as jnp
from jax import lax
import numpy as np

D_MODEL = 1024
BATCH = 2
SEQ = 8192
DEPTH = 2
DEC_BATCH = 128
DEC_SEQ = 4
PAST_LEN = 8192
PAGE_SIZE = 128

ALPHA = (2 * DEPTH) ** 0.25
BETA = (8 * DEPTH) ** -0.25
N_EVEN = (DEPTH + 1) // 2
N_ODD = DEPTH // 2
D_FF = 2816
MLA_HEADS = 8
Q_LORA = 512
KV_LORA = 256
QK_NOPE = 64
QK_ROPE = 32
V_DIM = 64
Q_BLOCK = 128
RET_HEADS = 4
RET_DK = 128
RET_DV = 128
RET_CHUNK = 128
POOL_WINDOWS = (2, 4, 8, 16)
POOL_GROUPS = 4
POOL_GROUP = D_MODEL // POOL_GROUPS
POOL_PREFIX = 15
ROPE_BASE = 10000.0
LN_EPS = 1e-5
RMS_EPS = 1e-6
D_MIX = MLA_HEADS * V_DIM + RET_HEADS * RET_DV
SPLIT_SIZES = (Q_LORA, KV_LORA, QK_ROPE, RET_HEADS * RET_DK, RET_HEADS * RET_DK, RET_HEADS * RET_DV, RET_HEADS * RET_DV)
D_IN = Q_LORA + KV_LORA + QK_ROPE + 2 * RET_HEADS * RET_DK + 2 * RET_HEADS * RET_DV

kernel_name = 'hybrid_mla_retention_pool_decoder'


def layer_norm(x, g, b):
    xf = x.astype(jnp.float32)
    mu = xf.mean(-1, keepdims=True)
    var = jnp.square(xf - mu).mean(-1, keepdims=True)
    return ((xf - mu) * lax.rsqrt(var + LN_EPS) * g + b).astype(x.dtype)


def rms_norm(x, g):
    xf = x.astype(jnp.float32)
    return (xf * lax.rsqrt(jnp.square(xf).mean(-1, keepdims=True) + RMS_EPS) * g).astype(x.dtype)


def rope(x, pos):
    r = x.shape[-1]
    inv = 1.0 / (ROPE_BASE ** (jnp.arange(0, r, 2, dtype=jnp.float32) / r))
    ang = pos.astype(jnp.float32)[:, None] * inv[None, :]
    ang = ang.reshape(ang.shape[:1] + (1,) * (x.ndim - 3) + ang.shape[1:])
    cos, sin = jnp.cos(ang), jnp.sin(ang)
    xf = x.astype(jnp.float32)
    x1, x2 = xf[..., : r // 2], xf[..., r // 2:]
    return jnp.concatenate([x1 * cos - x2 * sin, x1 * sin + x2 * cos], -1).astype(x.dtype)


def swiglu(x, wg, wu, wd):
    return (jax.nn.silu(x @ wg) * (x @ wu)) @ wd


def half_ffn_block(h, w, layer, m):
    f = swiglu(h, w['ffn_w_gate'][layer, m], w['ffn_w_up'][layer, m], w['ffn_w_down'][layer, m])
    return layer_norm(ALPHA * h + 0.5 * f, w['ln_gain'][layer, 2 * m], w['ln_bias'][layer, 2 * m])


def mla_prompt_attend(q_nope, q_pe, ckv, kpe, w_uk, w_uv):
    B, T, H, _ = q_nope.shape
    k_nope = jnp.einsum('btc,chd->bthd', ckv, w_uk)
    v = jnp.einsum('btc,chd->bthd', ckv, w_uv)
    nb = T // Q_BLOCK
    kpos = jnp.arange(T)
    scale = (QK_NOPE + QK_ROPE) ** -0.5
    neg = jnp.finfo(jnp.float32).min

    def blocks(a):
        return a.reshape((B, nb, Q_BLOCK) + a.shape[2:]).swapaxes(0, 1)

    def one_block(args):
        i, qn, qp = args
        s = jnp.einsum('bqhd,bkhd->bhqk', qn, k_nope) + jnp.einsum('bqhr,bkr->bhqk', qp, kpe)
        s = s.astype(jnp.float32) * scale
        qpos = i * Q_BLOCK + jnp.arange(Q_BLOCK)
        s = jnp.where(kpos[None, :] <= qpos[:, None], s, neg)
        p = jax.nn.softmax(s, axis=-1).astype(v.dtype)
        return jnp.einsum('bhqk,bkhd->bqhd', p, v)

    o = lax.map(one_block, (jnp.arange(nb), blocks(q_nope), blocks(q_pe)))
    return o.swapaxes(0, 1).reshape(B, T, H, V_DIM)


def mla_paged_attend(q_nope, q_pe, ckv_new, kpe_new, ckv_past, kpe_past, w_uk, w_uv):
    B, T, H, _ = q_nope.shape
    P = ckv_past.shape[1]
    scale = (QK_NOPE + QK_ROPE) ** -0.5
    neg = jnp.finfo(jnp.float32).min
    q_lat = jnp.einsum('bthd,chd->bthc', q_nope, w_uk)
    s_past = jnp.einsum('bthc,bkc->bhtk', q_lat, ckv_past) + jnp.einsum('bthr,bkr->bhtk', q_pe, kpe_past)
    s_new = jnp.einsum('bthc,bkc->bhtk', q_lat, ckv_new) + jnp.einsum('bthr,bkr->bhtk', q_pe, kpe_new)
    s = jnp.concatenate([s_past, s_new], -1).astype(jnp.float32) * scale
    causal = jnp.arange(T)[None, :] <= jnp.arange(T)[:, None]
    mask = jnp.concatenate([jnp.ones((T, P), dtype=bool), causal], -1)
    s = jnp.where(mask, s, neg)
    p = jax.nn.softmax(s, axis=-1).astype(ckv_past.dtype)
    o_lat = jnp.einsum('bhtk,bkc->bthc', p[..., :P], ckv_past) + jnp.einsum('bhtk,bkc->bthc', p[..., P:], ckv_new)
    return jnp.einsum('bthc,chd->bthd', o_lat, w_uv)


def retention(q, k, v, state0, chunk):
    B, T, H, DK = q.shape
    DV = v.shape[-1]
    n = T // chunk
    log_g = jnp.log(1.0 - 2.0 ** (-5.0 - jnp.arange(H, dtype=jnp.float32)))
    idx = jnp.arange(chunk, dtype=jnp.float32)
    diff = idx[:, None] - idx[None, :]
    d_in = jnp.where(diff >= 0, jnp.exp(jnp.maximum(diff, 0.0)[None] * log_g[:, None, None]), 0.0)
    q_dec = jnp.exp((idx + 1.0)[None, :] * log_g[:, None])[..., None]
    k_dec = jnp.exp((chunk - 1.0 - idx)[None, :] * log_g[:, None])[..., None]
    g_c = jnp.exp(chunk * log_g)[:, None, None]

    def to_chunks(a):
        return a.astype(jnp.float32).reshape(B, n, chunk, H, a.shape[-1]).transpose(1, 0, 3, 2, 4)

    def step(s, blk):
        qb, kb, vb = blk
        inner = jnp.einsum('bhid,bhjd->bhij', qb, kb) * d_in
        o = jnp.einsum('bhij,bhjv->bhiv', inner, vb) + jnp.einsum('bhid,bhdv->bhiv', qb * q_dec, s)
        s = s * g_c + jnp.einsum('bhjd,bhjv->bhdv', kb * k_dec, vb)
        return s, o

    s, o = lax.scan(step, state0.astype(jnp.float32), (to_chunks(q), to_chunks(k), to_chunks(v)))
    return o.transpose(1, 0, 3, 2, 4).reshape(B, T, H, DV), s


def mixer_ab(h, pos, ret_state0, mla_cache, w_in, q_norm, kv_norm, w_uq, w_uk, w_uv, gn_gain, gn_bias, w_out):
    B, T, _ = h.shape
    split_at = [int(i) for i in np.cumsum(SPLIT_SIZES)[:-1]]
    q_lat, ckv, kpe, rq, rk, rv, rg = jnp.split(h @ w_in, split_at, axis=-1)
    q = jnp.einsum('btc,chd->bthd', rms_norm(q_lat, q_norm), w_uq)
    q_nope, q_pe = q[..., :QK_NOPE], rope(q[..., QK_NOPE:], pos)
    ckv = rms_norm(ckv, kv_norm)
    kpe = rope(kpe, pos)
    if mla_cache is None:
        a = mla_prompt_attend(q_nope, q_pe, ckv, kpe, w_uk, w_uv)
    else:
        cache_ckv, cache_kpe, page_table = mla_cache
        ckv_past = cache_ckv[page_table].reshape(B, -1, KV_LORA)
        kpe_past = cache_kpe[page_table].reshape(B, -1, QK_ROPE)
        a = mla_paged_attend(q_nope, q_pe, ckv, kpe, ckv_past, kpe_past, w_uk, w_uv)
    rq = rope(rq.reshape(B, T, RET_HEADS, RET_DK), pos)
    rk = rope(rk.reshape(B, T, RET_HEADS, RET_DK), pos) * (RET_DK ** -0.5)
    rv = rv.reshape(B, T, RET_HEADS, RET_DV)
    chunk = RET_CHUNK if T % RET_CHUNK == 0 else T
    ro, ret_state = retention(rq, rk, rv, ret_state0, chunk)
    mu = ro.mean(-1, keepdims=True)
    var = jnp.square(ro - mu).mean(-1, keepdims=True)
    ro = ((ro - mu) * lax.rsqrt(var + LN_EPS)).reshape(B, T, RET_HEADS * RET_DV) * gn_gain + gn_bias
    ro = jax.nn.silu(rg.astype(jnp.float32)) * ro
    mixed = jnp.concatenate([a.reshape(B, T, MLA_HEADS * V_DIM), ro.astype(h.dtype)], -1)
    return mixed @ w_out, ckv, kpe, ret_state.astype(h.dtype)


def pool_mixer(h, prefix, pos, w, b, scale):
    B, T, D = h.shape
    xp = jnp.concatenate([prefix.astype(h.dtype), h], axis=1)
    c = jnp.cumsum(xp.astype(jnp.float32), axis=1)
    c = jnp.pad(c, ((0, 0), (1, 0), (0, 0)))
    end = c[:, POOL_PREFIX + 1:]
    outs = []
    for gi, wl in enumerate(POOL_WINDOWS):
        lo, hi = gi * POOL_GROUP, (gi + 1) * POOL_GROUP
        start = c[:, POOL_PREFIX + 1 - wl: POOL_PREFIX + 1 - wl + T, lo:hi]
        cnt = jnp.minimum(pos + 1, wl).astype(jnp.float32)[None, :, None]
        outs.append((end[..., lo:hi] - start) / cnt)
    pooled = jnp.concatenate(outs, -1) - h.astype(jnp.float32)
    y = jnp.einsum('btgi,gio->btgo', pooled.reshape(B, T, POOL_GROUPS, POOL_GROUP), w).reshape(B, T, D) + b
    return (scale * y).astype(h.dtype), xp[:, -POOL_PREFIX:]


def trunk(x, start, ret_states, pool_prefix, mla_cache, w):
    T = x.shape[1]
    pos = start + jnp.arange(T, dtype=jnp.int32)
    h = x
    ckvs, kpes, rets, pools = [], [], [], []
    for layer in range(DEPTH):
        j = layer // 2
        h = half_ffn_block(h, w, layer, 0)
        if layer % 2 == 0:
            cache_j = None if mla_cache is None else (mla_cache[0][j], mla_cache[1][j], mla_cache[2])
            y, ckv, kpe, rs = mixer_ab(h, pos, ret_states[j], cache_j, w['mix_w_in'][j], w['mla_q_norm'][j],
                                       w['mla_kv_norm'][j], w['mla_w_uq'][j], w['mla_w_uk'][j], w['mla_w_uv'][j],
                                       w['ret_gn_gain'][j], w['ret_gn_bias'][j], w['mix_w_out'][j])
            ckvs.append(ckv)
            kpes.append(kpe)
            rets.append(rs)
        else:
            y, rows = pool_mixer(h, pool_prefix[j], pos, w['pool_w'][j], w['pool_b'][j], w['pool_scale'][j])
            pools.append(rows)
        h = layer_norm(ALPHA * h + y, w['ln_gain'][layer, 1], w['ln_bias'][layer, 1])
        h = half_ffn_block(h, w, layer, 1)
    return h, jnp.stack(ckvs), jnp.stack(kpes), jnp.stack(rets), jnp.stack(pools)


def setup_inputs(seed: int = 0) -> dict:
    key = jax.random.key(seed)
    ks = jax.random.split(key, 24)
    n_pages = PAST_LEN // PAGE_SIZE
    n_used = DEC_BATCH * n_pages
    n_pool = n_used + max(1, n_used // 4)

    def nrm(k, shape, s=1.0):
        return s * jax.random.normal(k, shape, jnp.float32)

    page_table = jax.random.permutation(ks[6], n_pool)[:n_used].reshape(DEC_BATCH, n_pages).astype(jnp.int32)
    return {
        'x_prompt': nrm(ks[0], (BATCH, SEQ, D_MODEL)),
        'x_sample': nrm(ks[1], (DEC_BATCH, DEC_SEQ, D_MODEL)),
        'cache_mla_ckv': nrm(ks[2], (N_EVEN, n_pool, PAGE_SIZE, KV_LORA)),
        'cache_mla_kpe': nrm(ks[3], (N_EVEN, n_pool, PAGE_SIZE, QK_ROPE)),
        'state_ret': nrm(ks[4], (N_EVEN, DEC_BATCH, RET_HEADS, RET_DK, RET_DV), 0.5),
        'state_pool': nrm(ks[5], (N_ODD, DEC_BATCH, POOL_PREFIX, D_MODEL)),
        'page_table': page_table,
        'ffn_w_gate': nrm(ks[7], (DEPTH, 2, D_MODEL, D_FF), D_MODEL ** -0.5),
        'ffn_w_up': nrm(ks[8], (DEPTH, 2, D_MODEL, D_FF), D_MODEL ** -0.5),
        'ffn_w_down': nrm(ks[9], (DEPTH, 2, D_FF, D_MODEL), BETA * D_FF ** -0.5),
        'ln_gain': 1.0 + nrm(ks[10], (DEPTH, 3, D_MODEL), 0.05),
        'ln_bias': nrm(ks[11], (DEPTH, 3, D_MODEL), 0.02),
        'mix_w_in': nrm(ks[12], (N_EVEN, D_MODEL, D_IN), D_MODEL ** -0.5),
        'mla_q_norm': 1.0 + nrm(ks[13], (N_EVEN, Q_LORA), 0.05),
        'mla_kv_norm': 1.0 + nrm(ks[14], (N_EVEN, KV_LORA), 0.05),
        'mla_w_uq': nrm(ks[15], (N_EVEN, Q_LORA, MLA_HEADS, QK_NOPE + QK_ROPE), Q_LORA ** -0.5),
        'mla_w_uk': nrm(ks[16], (N_EVEN, KV_LORA, MLA_HEADS, QK_NOPE), KV_LORA ** -0.5),
        'mla_w_uv': nrm(ks[17], (N_EVEN, KV_LORA, MLA_HEADS, V_DIM), KV_LORA ** -0.5),
        'ret_gn_gain': 1.0 + nrm(ks[18], (N_EVEN, RET_HEADS * RET_DV), 0.05),
        'ret_gn_bias': nrm(ks[19], (N_EVEN, RET_HEADS * RET_DV), 0.02),
        'mix_w_out': nrm(ks[20], (N_EVEN, D_MIX, D_MODEL), BETA * D_MIX ** -0.5),
        'pool_w': nrm(ks[21], (N_ODD, POOL_GROUPS, POOL_GROUP, POOL_GROUP), BETA * POOL_GROUP ** -0.5),
        'pool_b': nrm(ks[22], (N_ODD, D_MODEL), 0.02),
        'pool_scale': 1.0 + nrm(ks[23], (N_ODD, D_MODEL), 0.1),
    }


def reference(x_prompt, x_sample, cache_mla_ckv, cache_mla_kpe, state_ret, state_pool, page_table,
              ffn_w_gate, ffn_w_up, ffn_w_down, ln_gain, ln_bias, mix_w_in, mla_q_norm, mla_kv_norm,
              mla_w_uq, mla_w_uk, mla_w_uv, ret_gn_gain, ret_gn_bias, mix_w_out, pool_w, pool_b, pool_scale):
    w = {'ffn_w_gate': ffn_w_gate, 'ffn_w_up': ffn_w_up, 'ffn_w_down': ffn_w_down,
         'ln_gain': ln_gain, 'ln_bias': ln_bias, 'mix_w_in': mix_w_in,
         'mla_q_norm': mla_q_norm, 'mla_kv_norm': mla_kv_norm, 'mla_w_uq': mla_w_uq,
         'mla_w_uk': mla_w_uk, 'mla_w_uv': mla_w_uv, 'ret_gn_gain': ret_gn_gain,
         'ret_gn_bias': ret_gn_bias, 'mix_w_out': mix_w_out, 'pool_w': pool_w,
         'pool_b': pool_b, 'pool_scale': pool_scale}
    bp = x_prompt.shape[0]
    zero_ret = jnp.zeros((N_EVEN, bp, RET_HEADS, RET_DK, RET_DV), jnp.float32)
    zero_pool = jnp.zeros((N_ODD, bp, POOL_PREFIX, D_MODEL), x_prompt.dtype)
    y_p, ckv_p, kpe_p, ret_p, pool_p = trunk(x_prompt, 0, zero_ret, zero_pool, None, w)
    y_s, ckv_s, kpe_s, ret_s, pool_s = trunk(x_sample, PAST_LEN, state_ret, state_pool,
                                             (cache_mla_ckv, cache_mla_kpe, page_table), w)
    return (y_p, y_s, ckv_p, kpe_p, ckv_s, kpe_s, ret_p, ret_s, pool_p, pool_s)
```

```python
import functools

import numpy as np
import jax
import jax.numpy as jnp
from jax import lax
from jax.experimental import pallas as pl
from jax.experimental.pallas import tpu as pltpu

F32 = jnp.float32
BF16 = jnp.bfloat16

D_MODEL = 1024
DEPTH = 2
PAST_LEN = 8192
PAGE_SIZE = 128
ALPHA = (2 * DEPTH) ** 0.25
D_FF = 2816
MLA_HEADS = 8
Q_LORA = 512
KV_LORA = 256
QK_NOPE = 64
QK_ROPE = 32
V_DIM = 64
RET_HEADS = 4
RET_DK = 128
RET_DV = 128
RET_CHUNK = 128
POOL_WINDOWS = (2, 4, 8, 16)
POOL_GROUPS = 4
POOL_GROUP = D_MODEL // POOL_GROUPS
POOL_PREFIX = 15
ROPE_BASE = 10000.0
LN_EPS = 1e-5
RMS_EPS = 1e-6
SPLIT_SIZES = (Q_LORA, KV_LORA, QK_ROPE, RET_HEADS * RET_DK, RET_HEADS * RET_DK,
               RET_HEADS * RET_DV, RET_HEADS * RET_DV)
ATT_SCALE = (QK_NOPE + QK_ROPE) ** -0.5

LANES = 128
HEAD_PAD = LANES
D_HEADS = MLA_HEADS * HEAD_PAD
D_RET = RET_HEADS * RET_DV
NEG = -1e30
VMEM_LIMIT = 56 * 1024 * 1024

_NT = (((1,), (1,)), ((), ()))
_TN = (((0,), (0,)), ((), ()))


def _params(*sem):
    return pltpu.CompilerParams(dimension_semantics=sem, vmem_limit_bytes=VMEM_LIMIT)


def _const_spec(shape):
    nd = len(shape)
    return pl.BlockSpec(shape, lambda *_: (0,) * nd, pipeline_mode=pl.Buffered(1))


def _layer_norm(y, g, b):
    mu = jnp.mean(y, axis=-1, keepdims=True)
    d = y - mu
    var = jnp.mean(d * d, axis=-1, keepdims=True)
    return d * lax.rsqrt(var + LN_EPS) * g + b


def _silu(x):
    return x * jax.nn.sigmoid(x)


FFN_CHUNK = 256


def _ffn_ln_kernel(x_ref, wg_ref, wu_ref, wd_ref, g_ref, b_ref, o_ref):
    x = x_ref[...]
    xb = x.astype(BF16)
    acc = None
    for c in range(D_FF // FFN_CHUNK):
        sl = slice(c * FFN_CHUNK, (c + 1) * FFN_CHUNK)
        g = jnp.dot(xb, wg_ref[:, sl], preferred_element_type=F32)
        u = jnp.dot(xb, wu_ref[:, sl], preferred_element_type=F32)
        a = (_silu(g) * u).astype(BF16)
        d = jnp.dot(a, wd_ref[sl, :], preferred_element_type=F32)
        acc = d if acc is None else acc + d
    o_ref[...] = _layer_norm(ALPHA * x + 0.5 * acc, g_ref[...], b_ref[...])


def _ffn_ln(x, wg, wu, wd, g, b, tm):
    m = x.shape[0]
    row = pl.BlockSpec((tm, D_MODEL), lambda i: (i, 0))
    return pl.pallas_call(
        _ffn_ln_kernel,
        grid=(m // tm,),
        in_specs=[row, _const_spec(wg.shape), _const_spec(wu.shape), _const_spec(wd.shape),
                  _const_spec(g.shape), _const_spec(b.shape)],
        out_specs=row,
        out_shape=jax.ShapeDtypeStruct((m, D_MODEL), F32),
        compiler_params=_params("parallel"),
        name="ffn_ln",
    )(x, wg, wu, wd, g, b)


_C_QL, _C_CKV, _C_KPE, _C_KPS, _C_RQ, _C_RK, _C_RV, _C_RG, _C_END = (
    0, 512, 768, 896, 1024, 1536, 2048, 2560, 3072)


def _prep_kernel(h_ref, w_ref, qn_ref, kvn_ref, uq1_ref, uq2_ref, wk_ref, wv_ref, vone_ref,
                 cq_ref, sq_ref, cr_ref, sr_ref,
                 q_ref, k_ref, v_ref, ckv_ref, kpe_ref, rq_ref, rk_ref, rv_ref, rg_ref):
    xb = h_ref[...].astype(BF16)

    def proj(a, b):
        return jnp.dot(xb, w_ref[:, a:b], preferred_element_type=F32)

    cq = cq_ref[...]
    sq = sq_ref[...]
    ql = proj(_C_QL, _C_CKV)
    qn = (ql * lax.rsqrt(jnp.mean(ql * ql, axis=-1, keepdims=True) + RMS_EPS) * qn_ref[...]).astype(BF16)
    for hh in range(MLA_HEADS):
        sl = slice(hh * HEAD_PAD, (hh + 1) * HEAD_PAD)
        a = jnp.dot(qn, uq1_ref[:, sl], preferred_element_type=F32)
        b = jnp.dot(qn, uq2_ref[:, sl], preferred_element_type=F32)
        q_ref[:, sl] = ((a * cq + b * sq) * ATT_SCALE).astype(BF16)
    c = proj(_C_CKV, _C_KPE)
    ckv = c * lax.rsqrt(jnp.mean(c * c, axis=-1, keepdims=True) + RMS_EPS) * kvn_ref[...]
    ckv_ref[...] = ckv
    cb = ckv.astype(BF16)
    kpe = proj(_C_KPE, _C_KPS) * cq + proj(_C_KPS, _C_RQ) * sq
    kpe_ref[...] = kpe
    kb = kpe.astype(BF16)
    k = (jnp.dot(cb, wk_ref[0:KV_LORA, :], preferred_element_type=F32)
         + jnp.dot(kb, wk_ref[KV_LORA:KV_LORA + HEAD_PAD, :], preferred_element_type=F32))
    k_ref[...] = k.astype(BF16)
    v_ref[...] = (jnp.dot(cb, wv_ref[...], preferred_element_type=F32) + vone_ref[...]).astype(BF16)
    cr = cr_ref[...]
    sr = sr_ref[...]
    rq = proj(_C_RQ, _C_RK)
    rk = proj(_C_RK, _C_RV)
    for hh in range(RET_HEADS):
        sl = slice(hh * RET_DK, (hh + 1) * RET_DK)
        xq = rq[:, sl]
        xk = rk[:, sl]
        rq_ref[:, sl] = xq * cr + pltpu.roll(xq, RET_DK // 2, 1) * sr
        rk_ref[:, sl] = (xk * cr + pltpu.roll(xk, RET_DK // 2, 1) * sr) * (RET_DK ** -0.5)
    rv_ref[...] = proj(_C_RV, _C_RG).astype(BF16)
    rg_ref[...] = proj(_C_RG, _C_END)


def _mixer_prep(h, mw, tabs, tm):
    m = h.shape[0]
    cq, sq, cr, sr = tabs
    tab_blocks = cq.shape[0] // tm

    def row(n):
        return pl.BlockSpec((tm, n), lambda i: (i, 0))

    tab = pl.BlockSpec((tm, LANES), lambda i: (i % tab_blocks, 0))
    consts = [mw["w_main"], mw["q_norm"], mw["kv_norm"], mw["uq1"], mw["uq2"], mw["wk"], mw["wv"], mw["vone"]]
    out_shape = [
        jax.ShapeDtypeStruct((m, D_HEADS), BF16),
        jax.ShapeDtypeStruct((m, D_HEADS), BF16),
        jax.ShapeDtypeStruct((m, D_HEADS), BF16),
        jax.ShapeDtypeStruct((m, KV_LORA), F32),
        jax.ShapeDtypeStruct((m, LANES), F32),
        jax.ShapeDtypeStruct((m, D_RET), F32),
        jax.ShapeDtypeStruct((m, D_RET), F32),
        jax.ShapeDtypeStruct((m, D_RET), BF16),
        jax.ShapeDtypeStruct((m, D_RET), F32),
    ]
    return pl.pallas_call(
        _prep_kernel,
        grid=(m // tm,),
        in_specs=[row(D_MODEL)] + [_const_spec(c.shape) for c in consts] + [tab] * 4,
        out_specs=[row(s.shape[1]) for s in out_shape],
        out_shape=out_shape,
        compiler_params=_params("parallel"),
        name="mixer_prep",
    )(h, *consts, cq, sq, cr, sr)


def _flash_kernel(q_ref, k_ref, v_ref, o_ref, *, blk):
    i = pl.program_id(2)
    q = q_ref[...]

    def scores(j):
        kj = k_ref[pl.ds(pl.multiple_of(j * blk, blk), blk), :]
        return lax.dot_general(q, kj, _NT, preferred_element_type=F32)

    def update(s, j, m, acc):
        m_new = jnp.maximum(m, jnp.max(s, axis=-1, keepdims=True))
        p = jnp.exp(s - m_new).astype(BF16)
        vj = v_ref[pl.ds(pl.multiple_of(j * blk, blk), blk), :]
        acc = acc * jnp.exp(m - m_new) + jnp.dot(p, vj, preferred_element_type=F32)
        return m_new, acc

    def body(j, carry):
        return update(scores(j), j, *carry)

    m0 = jnp.full((blk, 1), NEG, F32)
    acc0 = jnp.zeros((blk, HEAD_PAD), F32)
    m, acc = lax.fori_loop(0, i, body, (m0, acc0))
    rows = lax.broadcasted_iota(jnp.int32, (blk, blk), 0)
    cols = lax.broadcasted_iota(jnp.int32, (blk, blk), 1)
    s = jnp.where(cols <= rows, scores(i), NEG)
    m, acc = update(s, i, m, acc)
    o_ref[...] = (acc / acc[:, V_DIM:V_DIM + 1]).astype(BF16)


def _flash_attention(q, k, v, batch, seq, blk):
    nq = seq // blk
    qspec = pl.BlockSpec((blk, HEAD_PAD), lambda b, h, i: (b * nq + i, h))
    kvspec = pl.BlockSpec((seq, HEAD_PAD), lambda b, h, i: (b, h))
    return pl.pallas_call(
        functools.partial(_flash_kernel, blk=blk),
        grid=(batch, MLA_HEADS, nq),
        in_specs=[qspec, kvspec, kvspec],
        out_specs=qspec,
        out_shape=jax.ShapeDtypeStruct((batch * seq, D_HEADS), BF16),
        compiler_params=_params("parallel", "parallel", "arbitrary"),
        name="flash_attention",
    )(q, k, v)


T_PAD = 8
NEW_PAD = PAGE_SIZE
KEY_CHUNK = 1024


def _paged_kernel(pt_ref, q_ref, cn_ref, kn_ref, wq_ref, wuv_ref, ckv_hbm, kpe_hbm, o_ref,
                  ckv_buf, kpe_buf, kb_ref, s_ref, sem, *, n_pages, n_new):
    b = pl.program_id(0)
    nb = pl.num_programs(0)
    past = n_pages * PAGE_SIZE
    slot = b % 2

    def page_copies(bb, sl, p):
        page = pt_ref[bb, p]
        rows = pl.ds(p * PAGE_SIZE, PAGE_SIZE)
        return (pltpu.make_async_copy(ckv_hbm.at[page], ckv_buf.at[sl, rows, :], sem.at[sl, 0]),
                pltpu.make_async_copy(kpe_hbm.at[page], kpe_buf.at[sl, rows, :], sem.at[sl, 1]))

    def start_fetch(bb, sl):
        for p in range(n_pages):
            for cp in page_copies(bb, sl, p):
                cp.start()

    def wait_fetch(bb, sl):
        for p in range(n_pages):
            for cp in page_copies(bb, sl, p):
                cp.wait()

    @pl.when(b == 0)
    def _():
        ckv_buf[:, past:, :] = jnp.zeros((2, NEW_PAD, KV_LORA), F32)
        kpe_buf[:, past:, :] = jnp.zeros((2, NEW_PAD, QK_ROPE), F32)
        start_fetch(0, 0)

    @pl.when(b + 1 < nb)
    def _():
        start_fetch(b + 1, 1 - slot)

    q8 = q_ref[0]
    qrep = jnp.concatenate([q8] * MLA_HEADS, axis=0)
    n_rows = MLA_HEADS * T_PAD
    row_h = lax.broadcasted_iota(jnp.int32, (n_rows, D_HEADS), 0) // T_PAD
    col_h = lax.broadcasted_iota(jnp.int32, (n_rows, D_HEADS), 1) // HEAD_PAD
    qm = jnp.where(row_h == col_h, qrep, jnp.zeros_like(qrep))
    ql = jnp.dot(qm, wq_ref[...], preferred_element_type=F32)
    q_lat = ql[:, :KV_LORA].astype(BF16)
    q_pe = ql[:, KV_LORA:KV_LORA + QK_ROPE].astype(BF16)

    wait_fetch(b, slot)
    ckv_buf[slot, past:past + T_PAD, :] = cn_ref[0]
    kpe_buf[slot, past:past + T_PAD, :] = kn_ref[0]

    def chunk_scores(r0, n):
        kc = ckv_buf[slot, r0:r0 + n, :].astype(BF16)
        pc = kpe_buf[slot, r0:r0 + n, :].astype(BF16)
        kb_ref[r0:r0 + n, :] = kc
        return (lax.dot_general(q_lat, kc, _NT, preferred_element_type=F32)
                + lax.dot_general(q_pe, pc, _NT, preferred_element_type=F32))

    chunk = min(KEY_CHUNK, past)
    for c in range(past // chunk):
        s_ref[:, c * chunk:(c + 1) * chunk] = chunk_scores(c * chunk, chunk)
    s_new = chunk_scores(past, NEW_PAD)
    qt = lax.broadcasted_iota(jnp.int32, (n_rows, NEW_PAD), 0) % T_PAD
    kt = lax.broadcasted_iota(jnp.int32, (n_rows, NEW_PAD), 1)
    s_ref[:, past:] = jnp.where((kt <= qt) & (kt < n_new), s_new, NEG)

    m = jnp.max(s_ref[...], axis=-1, keepdims=True)
    l = jnp.zeros((n_rows, 1), F32)
    o = jnp.zeros((n_rows, KV_LORA), F32)
    bounds = [(c * chunk, chunk) for c in range(past // chunk)] + [(past, NEW_PAD)]
    for r0, n in bounds:
        p = jnp.exp(s_ref[:, r0:r0 + n] - m)
        l = l + jnp.sum(p, axis=-1, keepdims=True)
        o = o + jnp.dot(p.astype(BF16), kb_ref[r0:r0 + n, :], preferred_element_type=F32)
    o_lat = (o / l).astype(BF16)
    pv = jnp.dot(o_lat, wuv_ref[...], preferred_element_type=F32)
    pv = jnp.where(row_h == col_h, pv, 0.0)
    out = pv[0:T_PAD]
    for hh in range(1, MLA_HEADS):
        out = out + pv[hh * T_PAD:(hh + 1) * T_PAD]
    o_ref[0] = out.astype(BF16)


def _paged_attention(page_table, q8, ckv_new8, kpe_new8, wq, wuv, cache_ckv, cache_kpe, n_new):
    nb, n_pages = page_table.shape
    rows = n_pages * PAGE_SIZE + NEW_PAD
    grid_spec = pltpu.PrefetchScalarGridSpec(
        num_scalar_prefetch=1,
        grid=(nb,),
        in_specs=[
            pl.BlockSpec((1, T_PAD, D_HEADS), lambda b, pt: (b, 0, 0)),
            pl.BlockSpec((1, T_PAD, KV_LORA), lambda b, pt: (b, 0, 0)),
            pl.BlockSpec((1, T_PAD, QK_ROPE), lambda b, pt: (b, 0, 0)),
            pl.BlockSpec(wq.shape, lambda b, pt: (0, 0)),
            pl.BlockSpec(wuv.shape, lambda b, pt: (0, 0)),
            pl.BlockSpec(memory_space=pl.ANY),
            pl.BlockSpec(memory_space=pl.ANY),
        ],
        out_specs=pl.BlockSpec((1, T_PAD, D_HEADS), lambda b, pt: (b, 0, 0)),
        scratch_shapes=[
            pltpu.VMEM((2, rows, KV_LORA), F32),
            pltpu.VMEM((2, rows, QK_ROPE), F32),
            pltpu.VMEM((rows, KV_LORA), BF16),
            pltpu.VMEM((MLA_HEADS * T_PAD, rows), F32),
            pltpu.SemaphoreType.DMA((2, 2)),
        ],
    )
    return pl.pallas_call(
        functools.partial(_paged_kernel, n_pages=n_pages, n_new=n_new),
        grid_spec=grid_spec,
        out_shape=jax.ShapeDtypeStruct((nb, T_PAD, D_HEADS), BF16),
        compiler_params=_params("arbitrary"),
        name="paged_attention",
    )(page_table, q8, ckv_new8, kpe_new8, wq, wuv, cache_ckv, cache_kpe)


def _group_norm_gate(o, gate, gain, bias):
    mu = jnp.mean(o, axis=-1, keepdims=True)
    d = o - mu
    var = jnp.mean(d * d, axis=-1, keepdims=True)
    return _silu(gate) * (d * lax.rsqrt(var + LN_EPS) * gain + bias)


def _ret_kernel(rq_ref, rk_ref, rv_ref, rg_ref, din_ref, qd_ref, kd_ref, gc_ref, gg_ref, gb_ref, s0_ref,
                ro_ref, so_ref, s_ref, *, chunks):
    i = pl.program_id(1)

    @pl.when(i == 0)
    def _():
        s_ref[...] = s0_ref[0]

    for hh in range(RET_HEADS):
        sl = slice(hh * RET_DK, (hh + 1) * RET_DK)
        din = din_ref[:, sl]
        qd = qd_ref[:, sl]
        kd = kd_ref[:, sl]
        gc = gc_ref[:, sl]
        gain = gg_ref[:, sl]
        bias = gb_ref[:, sl]
        s = s_ref[hh]
        for c in range(chunks):
            rows = slice(c * RET_CHUNK, (c + 1) * RET_CHUNK)
            qh = rq_ref[rows, sl]
            kh = rk_ref[rows, sl]
            vh = rv_ref[rows, sl]
            inner = lax.dot_general(qh.astype(BF16), kh.astype(BF16), _NT, preferred_element_type=F32) * din
            o = (jnp.dot(inner.astype(BF16), vh, preferred_element_type=F32)
                 + jnp.dot((qh * qd).astype(BF16), s.astype(BF16), preferred_element_type=F32))
            s = s * gc + lax.dot_general((kh * kd).astype(BF16), vh, _TN, preferred_element_type=F32)
            ro_ref[rows, sl] = _group_norm_gate(o, rg_ref[rows, sl], gain, bias).astype(BF16)
        s_ref[hh] = s

    @pl.when(i == pl.num_programs(1) - 1)
    def _():
        so_ref[0] = s_ref[...]


def _retention_prompt(rq, rk, rv, rg, dec, gn_gain, gn_bias, state0, batch, seq, chunks):
    rows = chunks * RET_CHUNK
    steps = seq // rows
    rspec = pl.BlockSpec((rows, D_RET), lambda b, i: (b * steps + i, 0))
    sspec = pl.BlockSpec((1, RET_HEADS, RET_DK, RET_DV), lambda b, i: (b, 0, 0, 0))
    consts = [dec["din"], dec["qdec"], dec["kdec"], dec["gc"], gn_gain, gn_bias]
    return pl.pallas_call(
        functools.partial(_ret_kernel, chunks=chunks),
        grid=(batch, steps),
        in_specs=[rspec] * 4 + [_const_spec(c.shape) for c in consts] + [sspec],
        out_specs=[rspec, sspec],
        out_shape=[jax.ShapeDtypeStruct((batch * seq, D_RET), BF16),
                   jax.ShapeDtypeStruct((batch, RET_HEADS, RET_DK, RET_DV), F32)],
        scratch_shapes=[pltpu.VMEM((RET_HEADS, RET_DK, RET_DV), F32)],
        compiler_params=_params("parallel", "arbitrary"),
        name="retention_prompt",
    )(rq, rk, rv, rg, *consts, state0)


RS_BATCH = 8


def _ret_sample_kernel(qk_ref, v_ref, rg_ref, g_ref, gg_ref, gb_ref, s0_ref, ro_ref, so_ref, *, n_tok):
    def one_seq(bi, carry):
        qk = qk_ref[bi]
        for hh in range(RET_HEADS):
            g = g_ref[hh][0:1, :]
            idx = bi * RET_HEADS + hh
            s = s0_ref[idx]
            vt = v_ref[idx]
            outs = []
            for t in range(n_tok):
                cq = hh * n_tok + t
                ck = RET_HEADS * n_tok + cq
                kcol = jnp.broadcast_to(qk[:, ck:ck + 1], (RET_DK, RET_DV))
                qcol = jnp.broadcast_to(qk[:, cq:cq + 1], (RET_DK, RET_DV))
                s = s * g + kcol * vt[t:t + 1, :]
                outs.append(jnp.sum(qcol * s, axis=0, keepdims=True))
            so_ref[idx] = s
            o = jnp.concatenate(outs + [jnp.zeros((T_PAD - n_tok, RET_DV), F32)], axis=0)
            sl = slice(hh * RET_DV, (hh + 1) * RET_DV)
            ro_ref[idx] = _group_norm_gate(o, rg_ref[idx], gg_ref[:, sl], gb_ref[:, sl]).astype(BF16)
        return carry

    lax.fori_loop(0, RS_BATCH, one_seq, 0)


def _retention_sample(qk_t, v, rg, gtab, gn_gain, gn_bias, state0, n_tok):
    nb = qk_t.shape[0]
    g_per = RS_BATCH * RET_HEADS
    gspec = pl.BlockSpec((g_per, T_PAD, RET_DV), lambda i: (i, 0, 0))
    sspec = pl.BlockSpec((g_per, RET_DK, RET_DV), lambda i: (i, 0, 0))
    return pl.pallas_call(
        functools.partial(_ret_sample_kernel, n_tok=n_tok),
        grid=(nb // RS_BATCH,),
        in_specs=[pl.BlockSpec((RS_BATCH, RET_DK, LANES), lambda i: (i, 0, 0)), gspec, gspec,
                  _const_spec(gtab.shape), _const_spec(gn_gain.shape), _const_spec(gn_bias.shape), sspec],
        out_specs=[gspec, sspec],
        out_shape=[jax.ShapeDtypeStruct((nb * RET_HEADS, T_PAD, RET_DV), BF16),
                   jax.ShapeDtypeStruct((nb * RET_HEADS, RET_DK, RET_DV), F32)],
        compiler_params=_params("parallel"),
        name="retention_sample",
    )(qk_t, v, rg, gtab, gn_gain, gn_bias, state0)


def _outproj_ln_kernel(a_ref, ro_ref, h_ref, wa_ref, wr_ref, g_ref, b_ref, o_ref):
    y = (jnp.dot(a_ref[...], wa_ref[...], preferred_element_type=F32)
         + jnp.dot(ro_ref[...], wr_ref[...], preferred_element_type=F32))
    o_ref[...] = _layer_norm(ALPHA * h_ref[...] + y, g_ref[...], b_ref[...])


def _outproj_ln(a, ro, h, wa, wr, g, b, tm):
    m = h.shape[0]

    def row(n):
        return pl.BlockSpec((tm, n), lambda i: (i, 0))

    return pl.pallas_call(
        _outproj_ln_kernel,
        grid=(m // tm,),
        in_specs=[row(D_HEADS), row(D_RET), row(D_MODEL), _const_spec(wa.shape), _const_spec(wr.shape),
                  _const_spec(g.shape), _const_spec(b.shape)],
        out_specs=row(D_MODEL),
        out_shape=jax.ShapeDtypeStruct((m, D_MODEL), F32),
        compiler_params=_params("parallel"),
        name="outproj_ln",
    )(a, ro, h, wa, wr, g, b)


HALO = 16


def _pool_tail(pooled_groups, x, pw_ref, pb_ref, ps_ref, g_ref, b_ref):
    ys = [jnp.dot(p.astype(BF16), pw_ref[gi], preferred_element_type=F32) for gi, p in enumerate(pooled_groups)]
    y = (jnp.concatenate(ys, axis=-1) + pb_ref[...]) * ps_ref[...]
    return _layer_norm(ALPHA * x + y, g_ref[...], b_ref[...])


def _pool_prompt_kernel(h_ref, halo_ref, pre_ref, pw_ref, pb_ref, ps_ref, g_ref, b_ref, o_ref, xs_ref,
                        *, tm, tiles, start):
    t = pl.program_id(0) % tiles
    x = h_ref[...]
    xs_ref[0:HALO, :] = jnp.where(t == 0, pre_ref[0], halo_ref[...])
    xs_ref[HALO:, :] = x
    pos = start + t * tm + lax.broadcasted_iota(jnp.int32, (tm, 1), 0)
    pooled = []
    for gi, wl in enumerate(POOL_WINDOWS):
        sl = slice(gi * POOL_GROUP, (gi + 1) * POOL_GROUP)
        acc = x[:, sl]
        for d in range(1, wl):
            acc = acc + xs_ref[HALO - d:HALO - d + tm, sl]
        cnt = jnp.minimum(pos + 1, wl).astype(F32)
        pooled.append(acc / cnt - x[:, sl])
    o_ref[...] = _pool_tail(pooled, x, pw_ref, pb_ref, ps_ref, g_ref, b_ref)


def _pool_prompt(h, prefix16, pw, pb, ps, g, b, batch, seq, start, tm):
    tiles = seq // tm
    per = tm // HALO
    consts = [pw, pb, ps, g, b]
    return pl.pallas_call(
        functools.partial(_pool_prompt_kernel, tm=tm, tiles=tiles, start=start),
        grid=(batch * tiles,),
        in_specs=[pl.BlockSpec((tm, D_MODEL), lambda i: (i, 0)),
                  pl.BlockSpec((HALO, D_MODEL), lambda i: (jnp.maximum(i * per - 1, 0), 0)),
                  pl.BlockSpec((1, HALO, D_MODEL), lambda i: (i // tiles, 0, 0))]
                 + [_const_spec(c.shape) for c in consts],
        out_specs=pl.BlockSpec((tm, D_MODEL), lambda i: (i, 0)),
        out_shape=jax.ShapeDtypeStruct((batch * seq, D_MODEL), F32),
        scratch_shapes=[pltpu.VMEM((HALO + tm, D_MODEL), F32)],
        compiler_params=_params("parallel"),
        name="pool_prompt",
    )(h, h, prefix16, *consts)


def _pool_sample_kernel(xs_ref, pw_ref, pb_ref, ps_ref, g_ref, b_ref, o_ref, *, n_tok, start):
    nb = xs_ref.shape[1]
    x = jnp.concatenate([xs_ref[HALO + t] for t in range(n_tok)], axis=0)
    pooled = []
    for gi, wl in enumerate(POOL_WINDOWS):
        sl = slice(gi * POOL_GROUP, (gi + 1) * POOL_GROUP)
        parts = []
        for t in range(n_tok):
            acc = xs_ref[HALO + t, :, sl]
            for d in range(1, wl):
                acc = acc + xs_ref[HALO + t - d, :, sl]
            parts.append(acc / float(min(start + t + 1, wl)))
        pooled.append(jnp.concatenate(parts, axis=0) - x[:, sl])
    y = _pool_tail(pooled, x, pw_ref, pb_ref, ps_ref, g_ref, b_ref)
    for t in range(n_tok):
        o_ref[t] = y[t * nb:(t + 1) * nb]


def _pool_sample(xs_t, pw, pb, ps, g, b, n_tok, start):
    nb = xs_t.shape[1]
    args = [xs_t, pw, pb, ps, g, b]
    return pl.pallas_call(
        functools.partial(_pool_sample_kernel, n_tok=n_tok, start=start),
        grid=(1,),
        in_specs=[_const_spec(a.shape) for a in args],
        out_specs=_const_spec((n_tok, nb, D_MODEL)),
        out_shape=jax.ShapeDtypeStruct((n_tok, nb, D_MODEL), F32),
        compiler_params=_params("arbitrary"),
        name="pool_sample",
    )(*args)


def _head_pad(w, width):
    r, nh, d = w.shape
    out = jnp.zeros((r, nh, HEAD_PAD), w.dtype).at[:, :, :d].set(w)
    return out.reshape(r, nh * HEAD_PAD)[:, :width]


def _mixer_weights(mix_w_in, q_norm, kv_norm, w_uq, w_uk, w_uv, mix_w_out):
    offs = np.concatenate([[0], np.cumsum(SPLIT_SIZES)])
    wq, wckv, wkpe, wrq, wrk, wrv, wrg = [mix_w_in[:, offs[i]:offs[i + 1]] for i in range(7)]
    half = QK_ROPE // 2
    z_lo = jnp.zeros((D_MODEL, QK_NOPE), F32)
    z_hi = jnp.zeros((D_MODEL, HEAD_PAD - QK_NOPE - QK_ROPE), F32)
    kpe_blk = jnp.concatenate([z_lo, wkpe, z_hi], axis=1)
    kpe_swp = jnp.concatenate([z_lo, -wkpe[:, half:], wkpe[:, :half], z_hi], axis=1)
    w_main = jnp.concatenate([wq, wckv, kpe_blk, kpe_swp, wrq, wrk, wrv, wrg], axis=1).astype(BF16)
    pe = w_uq[:, :, QK_NOPE:]
    uq1 = _head_pad(w_uq, D_HEADS)
    uq2 = _head_pad(jnp.concatenate([jnp.zeros_like(w_uq[:, :, :QK_NOPE]), -pe[:, :, half:], pe[:, :, :half]], axis=2),
                    D_HEADS)
    lane = np.arange(HEAD_PAD)
    sel = ((lane[:, None] == lane[None, :]) & (lane[:, None] >= QK_NOPE) & (lane[:, None] < QK_NOPE + QK_ROPE))
    e_mat = jnp.asarray(np.tile(sel.astype(np.float32), (1, MLA_HEADS)))
    wk = jnp.concatenate([_head_pad(w_uk, D_HEADS), e_mat], axis=0)
    wv = _head_pad(w_uv, D_HEADS)
    vone = jnp.asarray((np.arange(D_HEADS) % HEAD_PAD == V_DIM).astype(np.float32))[None, :]
    wa = jnp.zeros((MLA_HEADS, HEAD_PAD, D_MODEL), F32).at[:, :V_DIM, :].set(
        mix_w_out[:MLA_HEADS * V_DIM].reshape(MLA_HEADS, V_DIM, D_MODEL)).reshape(D_HEADS, D_MODEL)
    wr = mix_w_out[MLA_HEADS * V_DIM:]
    wuk_t = jnp.zeros((MLA_HEADS, HEAD_PAD, KV_LORA), F32).at[:, :QK_NOPE, :].set(
        jnp.transpose(w_uk, (1, 2, 0))).reshape(D_HEADS, KV_LORA)
    r = np.arange(D_HEADS) % HEAD_PAD
    e_pe = ((r[:, None] - QK_NOPE) == np.arange(LANES)[None, :]) & (r[:, None] >= QK_NOPE) & (r[:, None] < QK_NOPE + QK_ROPE)
    wq_abs = jnp.concatenate([wuk_t, jnp.asarray(e_pe.astype(np.float32))], axis=1)
    return {
        "w_main": w_main, "q_norm": q_norm[None, :], "kv_norm": kv_norm[None, :],
        "uq1": uq1.astype(BF16), "uq2": uq2.astype(BF16), "wk": wk.astype(BF16), "wv": wv.astype(BF16),
        "vone": vone, "wa": wa.astype(BF16), "wr": wr.astype(BF16),
        "wq_abs": wq_abs.astype(BF16),
    }


def _rope_tables(pos):
    def angles(r):
        inv = 1.0 / (ROPE_BASE ** (jnp.arange(0, r, 2, dtype=F32) / r))
        return pos.astype(F32)[:, None] * inv[None, :]

    n = pos.shape[0]
    a = angles(QK_ROPE)
    c, s = jnp.cos(a), jnp.sin(a)
    hi = HEAD_PAD - QK_NOPE - QK_ROPE
    cq = jnp.concatenate([jnp.ones((n, QK_NOPE), F32), c, c, jnp.ones((n, hi), F32)], axis=1)
    sq = jnp.concatenate([jnp.zeros((n, QK_NOPE), F32), s, s, jnp.zeros((n, hi), F32)], axis=1)
    a = angles(RET_DK)
    c, s = jnp.cos(a), jnp.sin(a)
    return cq, sq, jnp.concatenate([c, c], axis=1), jnp.concatenate([-s, s], axis=1)


def _ret_log_decay():
    return jnp.log(1.0 - 2.0 ** (-5.0 - jnp.arange(RET_HEADS, dtype=F32)))


def _ret_decay_tables(chunk):
    log_g = _ret_log_decay()
    idx = jnp.arange(chunk, dtype=F32)
    diff = idx[:, None] - idx[None, :]
    d_in = jnp.where(diff >= 0, jnp.exp(jnp.maximum(diff, 0.0)[None] * log_g[:, None, None]), 0.0)
    q_dec = jnp.exp((idx + 1.0)[None, :] * log_g[:, None])
    k_dec = jnp.exp((chunk - 1.0 - idx)[None, :] * log_g[:, None])
    g_c = jnp.exp(chunk * log_g)
    lanes = (chunk, RET_HEADS * RET_DK)
    return {
        "din": jnp.transpose(d_in, (1, 0, 2)).reshape(chunk, RET_HEADS * chunk),
        "qdec": jnp.broadcast_to(q_dec.T[:, :, None], (chunk, RET_HEADS, RET_DK)).reshape(lanes),
        "kdec": jnp.broadcast_to(k_dec.T[:, :, None], (chunk, RET_HEADS, RET_DK)).reshape(lanes),
        "gc": jnp.broadcast_to(g_c[:, None], (RET_HEADS, RET_DV)).reshape(1, RET_HEADS * RET_DV),
    }


def _trunk(x, start, ret_state0, pool_prefix, mla_cache, w, mw):
    batch, seq, _ = x.shape
    m = batch * seq
    prompt = mla_cache is None
    tm = min(512, m)
    h = x.reshape(m, D_MODEL)

    def ffn(h, layer, half):
        return _ffn_ln(h, w["wg"][layer][half], w["wu"][layer][half], w["wd"][layer][half],
                       w["ln_gain"][layer, 2 * half][None, :], w["ln_bias"][layer, 2 * half][None, :], tm)

    h = ffn(h, 0, 0)
    if prompt:
        pos = start + jnp.arange(seq, dtype=jnp.int32)
    else:
        pos = start + (jnp.arange(m, dtype=jnp.int32) % seq)
    q, k, v, ckv, kpe, rq, rk, rv, rg = _mixer_prep(h, mw, _rope_tables(pos), tm)
    kpe = kpe[:, QK_NOPE:QK_NOPE + QK_ROPE]
    gn_gain = w["ret_gn_gain"][0][None, :]
    gn_bias = w["ret_gn_bias"][0][None, :]
    if prompt:
        a = _flash_attention(q, k, v, batch, seq, 512)
        chunk = RET_CHUNK if seq % RET_CHUNK == 0 else seq
        assert chunk == RET_CHUNK
        ro, ret_state = _retention_prompt(rq, rk, rv, rg, _ret_decay_tables(chunk), gn_gain, gn_bias,
                                          ret_state0, batch, seq, 8)
    else:
        assert seq <= T_PAD and seq % RET_CHUNK != 0
        cache_ckv, cache_kpe, page_table = mla_cache
        pad_t = ((0, 0), (0, T_PAD - seq), (0, 0))
        q8 = jnp.pad(q.reshape(batch, seq, D_HEADS), pad_t)
        cn8 = jnp.pad(ckv.reshape(batch, seq, KV_LORA), pad_t)
        kn8 = jnp.pad(kpe.reshape(batch, seq, QK_ROPE), pad_t)
        a8 = _paged_attention(page_table, q8, cn8, kn8, mw["wq_abs"], mw["wv"], cache_ckv, cache_kpe, seq)
        a = a8[:, :seq].reshape(m, D_HEADS)

        def heads_first(t):
            t = jnp.transpose(t.reshape(batch, seq, RET_HEADS, RET_DV), (0, 2, 1, 3))
            return jnp.pad(t.reshape(batch * RET_HEADS, seq, RET_DV), pad_t)

        def dk_first(t):
            return jnp.transpose(t.reshape(batch, seq, RET_HEADS, RET_DK), (0, 3, 2, 1)).reshape(
                batch, RET_DK, RET_HEADS * seq)

        qk_t = jnp.concatenate([dk_first(rq), dk_first(rk)], axis=2)
        qk_t = jnp.pad(qk_t, ((0, 0), (0, 0), (0, LANES - 2 * RET_HEADS * seq)))
        gtab = jnp.broadcast_to(jnp.exp(_ret_log_decay())[:, None, None], (RET_HEADS, 8, RET_DV))
        ro8, ret_state = _retention_sample(qk_t, heads_first(rv.astype(F32)), heads_first(rg), gtab,
                                           gn_gain, gn_bias,
                                           ret_state0.reshape(batch * RET_HEADS, RET_DK, RET_DV), seq)
        ro = jnp.transpose(ro8[:, :seq].reshape(batch, RET_HEADS, seq, RET_DV), (0, 2, 1, 3)).reshape(m, D_RET)
        ret_state = ret_state.reshape(batch, RET_HEADS, RET_DK, RET_DV)
    h = _outproj_ln(a, ro, h, mw["wa"], mw["wr"], w["ln_gain"][0, 1][None, :], w["ln_bias"][0, 1][None, :], tm)
    h = ffn(h, 0, 1)

    h = ffn(h, 1, 0)
    xp_tail = jnp.concatenate([pool_prefix, h.reshape(batch, seq, D_MODEL)], axis=1)[:, -POOL_PREFIX:]
    prefix16 = jnp.pad(pool_prefix, ((0, 0), (HALO - POOL_PREFIX, 0), (0, 0)))
    pool_args = (w["pool_w"], w["pool_b"][0][None, :], w["pool_scale"][0][None, :],
                 w["ln_gain"][1, 1][None, :], w["ln_bias"][1, 1][None, :])
    if prompt:
        h = _pool_prompt(h, prefix16, *pool_args, batch, seq, start, tm)
    else:
        xs_t = jnp.transpose(jnp.concatenate([prefix16, h.reshape(batch, seq, D_MODEL)], axis=1), (1, 0, 2))
        h = jnp.transpose(_pool_sample(xs_t, *pool_args, seq, start), (1, 0, 2)).reshape(m, D_MODEL)
    h = ffn(h, 1, 1)
    return (h.reshape(batch, seq, D_MODEL), ckv.reshape(1, batch, seq, KV_LORA),
            kpe.reshape(1, batch, seq, QK_ROPE), ret_state[None], xp_tail[None])


def kernel(x_prompt, x_sample, cache_mla_ckv, cache_mla_kpe, state_ret, state_pool, page_table, ffn_w_gate, ffn_w_up, ffn_w_down, ln_gain, ln_bias, mix_w_in, mla_q_norm, mla_kv_norm, mla_w_uq, mla_w_uk, mla_w_uv, ret_gn_gain, ret_gn_bias, mix_w_out, pool_w, pool_b, pool_scale):
    assert DEPTH == 2 and mix_w_in.shape[0] == 1 and pool_w.shape[0] == 1
    w = {
        "wg": ffn_w_gate.astype(BF16), "wu": ffn_w_up.astype(BF16), "wd": ffn_w_down.astype(BF16),
        "ln_gain": ln_gain, "ln_bias": ln_bias, "ret_gn_gain": ret_gn_gain, "ret_gn_bias": ret_gn_bias,
        "pool_w": pool_w[0].astype(BF16), "pool_b": pool_b, "pool_scale": pool_scale,
    }
    mw = _mixer_weights(mix_w_in[0], mla_q_norm[0], mla_kv_norm[0], mla_w_uq[0], mla_w_uk[0], mla_w_uv[0],
                        mix_w_out[0])
    bp = x_prompt.shape[0]
    zero_ret = jnp.zeros((bp, RET_HEADS, RET_DK, RET_DV), F32)
    zero_pool = jnp.zeros((bp, POOL_PREFIX, D_MODEL), x_prompt.dtype)
    y_p, ckv_p, kpe_p, ret_p, pool_p = _trunk(x_prompt, 0, zero_ret, zero_pool, None, w, mw)
    y_s, ckv_s, kpe_s, ret_s, pool_s = _trunk(x_sample, PAST_LEN, state_ret[0], state_pool[0],
                                              (cache_mla_ckv[0], cache_mla_kpe[0], page_table), w, mw)
    return (y_p, y_s, ckv_p, kpe_p, ckv_s, kpe_s, ret_p, ret_s, pool_p, pool_s)
```

```python
import functools

import numpy as np
import jax
import jax.numpy as jnp
from jax import lax
from jax.experimental import pallas as pl
from jax.experimental.pallas import tpu as pltpu

F32 = jnp.float32
BF16 = jnp.bfloat16

D_MODEL = 1024
DEPTH = 2
PAST_LEN = 8192
PAGE_SIZE = 128
ALPHA = (2 * DEPTH) ** 0.25
D_FF = 2816
MLA_HEADS = 8
Q_LORA = 512
KV_LORA = 256
QK_NOPE = 64
QK_ROPE = 32
V_DIM = 64
RET_HEADS = 4
RET_DK = 128
RET_DV = 128
RET_CHUNK = 128
POOL_WINDOWS = (2, 4, 8, 16)
POOL_GROUPS = 4
POOL_GROUP = D_MODEL // POOL_GROUPS
POOL_PREFIX = 15
ROPE_BASE = 10000.0
LN_EPS = 1e-5
RMS_EPS = 1e-6
SPLIT_SIZES = (Q_LORA, KV_LORA, QK_ROPE, RET_HEADS * RET_DK, RET_HEADS * RET_DK,
               RET_HEADS * RET_DV, RET_HEADS * RET_DV)
ATT_SCALE = (QK_NOPE + QK_ROPE) ** -0.5 * 1.4426950408889634

LANES = 128
HEAD_PAD = LANES
D_HEADS = MLA_HEADS * HEAD_PAD
D_RET = RET_HEADS * RET_DV
NEG = -1e30
VMEM_LIMIT = 56 * 1024 * 1024

_NT = (((1,), (1,)), ((), ()))
_TN = (((0,), (0,)), ((), ()))


def _params(*sem):
    return pltpu.CompilerParams(dimension_semantics=sem, vmem_limit_bytes=VMEM_LIMIT)


def _const_spec(shape):
    nd = len(shape)
    return pl.BlockSpec(shape, lambda *_: (0,) * nd, pipeline_mode=pl.Buffered(1))


def _layer_norm(y, g, b):
    mu = jnp.mean(y, axis=-1, keepdims=True)
    d = y - mu
    var = jnp.mean(d * d, axis=-1, keepdims=True)
    return d * lax.rsqrt(var + LN_EPS) * g + b


def _silu(x):
    return x * jax.nn.sigmoid(x)


FFN_CHUNK = 256


def _ffn_ln_kernel(x_ref, wg_ref, wu_ref, wd_ref, g_ref, b_ref, o_ref):
    x = x_ref[...]
    xb = x.astype(BF16)
    acc = None
    for c in range(D_FF // FFN_CHUNK):
        sl = slice(c * FFN_CHUNK, (c + 1) * FFN_CHUNK)
        g = jnp.dot(xb, wg_ref[:, sl], preferred_element_type=F32)
        u = jnp.dot(xb, wu_ref[:, sl], preferred_element_type=F32)
        a = (_silu(g) * u).astype(BF16)
        d = jnp.dot(a, wd_ref[sl, :], preferred_element_type=F32)
        acc = d if acc is None else acc + d
    o_ref[...] = _layer_norm(ALPHA * x + 0.5 * acc, g_ref[...], b_ref[...])


def _ffn_ln(x, wg, wu, wd, layer, half, g, b, tm):
    m = x.shape[0]
    row = pl.BlockSpec((tm, D_MODEL), lambda i: (i, 0))

    def wspec(w):
        return pl.BlockSpec((None, None) + w.shape[2:], lambda i: (layer, half, 0, 0),
                            pipeline_mode=pl.Buffered(1))

    return pl.pallas_call(
        _ffn_ln_kernel,
        grid=(m // tm,),
        in_specs=[row, wspec(wg), wspec(wu), wspec(wd), _const_spec(g.shape), _const_spec(b.shape)],
        out_specs=row,
        out_shape=jax.ShapeDtypeStruct((m, D_MODEL), F32),
        compiler_params=_params("parallel"),
        name="ffn_ln",
    )(x, wg, wu, wd, g, b)


_C_QL, _C_CKV, _C_KPE, _C_KPS, _C_RQ, _C_RK, _C_RV, _C_RG, _C_END = (
    0, 512, 768, 896, 1024, 1536, 2048, 2560, 3072)


def _prep_kernel(h_ref, w_ref, qn_ref, kvn_ref, uq1_ref, uq2_ref, wk_ref, wv_ref, vone_ref,
                 cq_ref, sq_ref, cr_ref, sr_ref,
                 q_ref, k_ref, v_ref, ckv_ref, kpe_ref, rq_ref, rk_ref, rv_ref, rg_ref):
    xb = h_ref[...].astype(BF16)

    def proj(a, b):
        return jnp.dot(xb, w_ref[:, a:b], preferred_element_type=F32)

    cq = cq_ref[...]
    sq = sq_ref[...]
    ql = proj(_C_QL, _C_CKV)
    qn = (ql * lax.rsqrt(jnp.mean(ql * ql, axis=-1, keepdims=True) + RMS_EPS) * qn_ref[...]).astype(BF16)
    for hh in range(MLA_HEADS):
        sl = slice(hh * HEAD_PAD, (hh + 1) * HEAD_PAD)
        a = jnp.dot(qn, uq1_ref[:, sl], preferred_element_type=F32)
        b = jnp.dot(qn, uq2_ref[:, sl], preferred_element_type=F32)
        q_ref[:, sl] = ((a * cq + b * sq) * ATT_SCALE).astype(BF16)
    c = proj(_C_CKV, _C_KPE)
    ckv = c * lax.rsqrt(jnp.mean(c * c, axis=-1, keepdims=True) + RMS_EPS) * kvn_ref[...]
    ckv_ref[...] = ckv
    cb = ckv.astype(BF16)
    kpe = proj(_C_KPE, _C_KPS) * cq + proj(_C_KPS, _C_RQ) * sq
    kpe_ref[...] = kpe
    kb = kpe.astype(BF16)
    k = (jnp.dot(cb, wk_ref[0:KV_LORA, :], preferred_element_type=F32)
         + jnp.dot(kb, wk_ref[KV_LORA:KV_LORA + HEAD_PAD, :], preferred_element_type=F32))
    k_ref[...] = k.astype(BF16)
    v_ref[...] = (jnp.dot(cb, wv_ref[...], preferred_element_type=F32) + vone_ref[...]).astype(BF16)
    cr = cr_ref[...]
    sr = sr_ref[...]
    rq = proj(_C_RQ, _C_RK)
    rk = proj(_C_RK, _C_RV)
    for hh in range(RET_HEADS):
        sl = slice(hh * RET_DK, (hh + 1) * RET_DK)
        xq = rq[:, sl]
        xk = rk[:, sl]
        rq_ref[:, sl] = xq * cr + pltpu.roll(xq, RET_DK // 2, 1) * sr
        rk_ref[:, sl] = (xk * cr + pltpu.roll(xk, RET_DK // 2, 1) * sr) * (RET_DK ** -0.5)
    rv_ref[...] = proj(_C_RV, _C_RG).astype(BF16)
    rg_ref[...] = proj(_C_RG, _C_END)


def _mixer_prep(h, mw, tabs, tm):
    m = h.shape[0]
    cq, sq, cr, sr = tabs
    tab_blocks = cq.shape[0] // tm

    def row(n):
        return pl.BlockSpec((tm, n), lambda i: (i, 0))

    tab = pl.BlockSpec((tm, LANES), lambda i: (i % tab_blocks, 0))
    consts = [mw["w_main"], mw["q_norm"], mw["kv_norm"], mw["uq1"], mw["uq2"], mw["wk"], mw["wv"], mw["vone"]]
    out_shape = [
        jax.ShapeDtypeStruct((m, D_HEADS), BF16),
        jax.ShapeDtypeStruct((m, D_HEADS), BF16),
        jax.ShapeDtypeStruct((m, D_HEADS), BF16),
        jax.ShapeDtypeStruct((m, KV_LORA), F32),
        jax.ShapeDtypeStruct((m, LANES), F32),
        jax.ShapeDtypeStruct((m, D_RET), F32),
        jax.ShapeDtypeStruct((m, D_RET), F32),
        jax.ShapeDtypeStruct((m, D_RET), BF16),
        jax.ShapeDtypeStruct((m, D_RET), F32),
    ]
    return pl.pallas_call(
        _prep_kernel,
        grid=(m // tm,),
        in_specs=[row(D_MODEL)] + [_const_spec(c.shape) for c in consts] + [tab] * 4,
        out_specs=[row(s.shape[1]) for s in out_shape],
        out_shape=out_shape,
        compiler_params=_params("parallel"),
        name="mixer_prep",
    )(h, *consts, cq, sq, cr, sr)


FLASH_HEADS = 2


def _flash_kernel(q_ref, k_ref, v_ref, o_ref, *, blk):
    i = pl.program_id(2)
    lanes = [slice(a * HEAD_PAD, (a + 1) * HEAD_PAD) for a in range(FLASH_HEADS)]
    qs = [q_ref[:, sl] for sl in lanes]

    def scores(j):
        rows = pl.ds(pl.multiple_of(j * blk, blk), blk)
        return tuple(lax.dot_general(q, k_ref[rows, sl], _NT, preferred_element_type=F32)
                     for q, sl in zip(qs, lanes))

    def update(s, j, m, acc, sl):
        m_new = jnp.maximum(m, jnp.max(s, axis=-1, keepdims=True))
        p = jnp.exp2(s - m_new).astype(BF16)
        vj = v_ref[pl.ds(pl.multiple_of(j * blk, blk), blk), sl]
        acc = acc * jnp.exp2(m - m_new) + jnp.dot(p, vj, preferred_element_type=F32)
        return m_new, acc

    def body(j, carry):
        ms, accs = carry
        new = [update(s, j, m, acc, sl) for s, m, acc, sl in zip(scores(j), ms, accs, lanes)]
        return tuple(n[0] for n in new), tuple(n[1] for n in new)

    m0 = tuple(jnp.full((blk, 1), NEG, F32) for _ in lanes)
    acc0 = tuple(jnp.zeros((blk, HEAD_PAD), F32) for _ in lanes)
    ms, accs = lax.fori_loop(0, i, body, (m0, acc0))
    rows = lax.broadcasted_iota(jnp.int32, (blk, blk), 0)
    cols = lax.broadcasted_iota(jnp.int32, (blk, blk), 1)
    for s, m, acc, sl in zip(scores(i), ms, accs, lanes):
        m, acc = update(jnp.where(cols <= rows, s, NEG), i, m, acc, sl)
        o_ref[:, sl] = (acc / acc[:, V_DIM:V_DIM + 1]).astype(BF16)


def _flash_attention(q, k, v, batch, seq, blk):
    nq = seq // blk
    width = FLASH_HEADS * HEAD_PAD
    qspec = pl.BlockSpec((blk, width), lambda b, h, i: (b * nq + i, h))
    kvspec = pl.BlockSpec((seq, width), lambda b, h, i: (b, h))
    return pl.pallas_call(
        functools.partial(_flash_kernel, blk=blk),
        grid=(batch, MLA_HEADS // FLASH_HEADS, nq),
        in_specs=[qspec, kvspec, kvspec],
        out_specs=qspec,
        out_shape=jax.ShapeDtypeStruct((batch * seq, D_HEADS), BF16),
        compiler_params=_params("parallel", "parallel", "arbitrary"),
        name="flash_attention",
    )(q, k, v)


T_PAD = 8
NEW_PAD = PAGE_SIZE
KEY_CHUNK = 1024


def _paged_kernel(pt_ref, q_ref, cn_ref, kn_ref, wq_ref, wuv_ref, ckv_hbm, kpe_hbm, o_ref,
                  ckv_buf, kpe_buf, kb_ref, s_ref, sem, *, n_pages, n_new):
    b = pl.program_id(0)
    nb = pl.num_programs(0)
    past = n_pages * PAGE_SIZE
    slot = b % 2

    def page_copies(bb, sl, p):
        page = pt_ref[bb, p]
        rows = pl.ds(p * PAGE_SIZE, PAGE_SIZE)
        return (pltpu.make_async_copy(ckv_hbm.at[page], ckv_buf.at[sl, rows, :], sem.at[sl, 0]),
                pltpu.make_async_copy(kpe_hbm.at[page], kpe_buf.at[sl, :, rows], sem.at[sl, 1]))

    def start_fetch(bb, sl):
        for p in range(n_pages):
            for cp in page_copies(bb, sl, p):
                cp.start()

    def wait_fetch(bb, sl):
        for p in range(n_pages):
            for cp in page_copies(bb, sl, p):
                cp.wait()

    @pl.when(b == 0)
    def _():
        ckv_buf[:, past:, :] = jnp.zeros((2, NEW_PAD, KV_LORA), F32)
        start_fetch(0, 0)

    @pl.when(b + 1 < nb)
    def _():
        start_fetch(b + 1, 1 - slot)

    q8 = q_ref[0]
    qrep = jnp.concatenate([q8] * MLA_HEADS, axis=0)
    n_rows = MLA_HEADS * T_PAD
    row_h = lax.broadcasted_iota(jnp.int32, (n_rows, D_HEADS), 0) // T_PAD
    col_h = lax.broadcasted_iota(jnp.int32, (n_rows, D_HEADS), 1) // HEAD_PAD
    qm = jnp.where(row_h == col_h, qrep, jnp.zeros_like(qrep))
    ql = jnp.dot(qm, wq_ref[...], preferred_element_type=F32)
    q_lat = ql[:, :KV_LORA].astype(BF16)
    q_pe = ql[:, KV_LORA:KV_LORA + QK_ROPE].astype(BF16)

    wait_fetch(b, slot)
    ckv_buf[slot, past:past + T_PAD, :] = cn_ref[0]
    kpe_buf[slot, :, past:] = kn_ref[0]

    chunk = min(KEY_CHUNK, past)
    bounds = [(c * chunk, chunk) for c in range(past // chunk)] + [(past, NEW_PAD)]
    qt = lax.broadcasted_iota(jnp.int32, (n_rows, NEW_PAD), 0) % T_PAD
    kt = lax.broadcasted_iota(jnp.int32, (n_rows, NEW_PAD), 1)
    for r0, n in bounds:
        kc = ckv_buf[slot, r0:r0 + n, :].astype(BF16)
        pc = kpe_buf[slot, :, r0:r0 + n].astype(BF16)
        kb_ref[r0:r0 + n, :] = kc
        s = (lax.dot_general(q_lat, kc, _NT, preferred_element_type=F32)
             + jnp.dot(q_pe, pc, preferred_element_type=F32))
        if r0 == past:
            s = jnp.where((kt <= qt) & (kt < n_new), s, NEG)
        s_ref[:, r0:r0 + n] = s
    m = jnp.max(s_ref[...], axis=-1, keepdims=True)
    l = jnp.zeros((n_rows, 1), F32)
    o = jnp.zeros((n_rows, KV_LORA), F32)
    for r0, n in bounds:
        p = jnp.exp2(s_ref[:, r0:r0 + n] - m)
        l = l + jnp.sum(p, axis=-1, keepdims=True)
        o = o + jnp.dot(p.astype(BF16), kb_ref[r0:r0 + n, :], preferred_element_type=F32)
    o_lat = (o / l).astype(BF16)
    pv = jnp.dot(o_lat, wuv_ref[...], preferred_element_type=F32)
    pv = jnp.where(row_h == col_h, pv, 0.0)
    out = pv[0:T_PAD]
    for hh in range(1, MLA_HEADS):
        out = out + pv[hh * T_PAD:(hh + 1) * T_PAD]
    o_ref[0] = out.astype(BF16)


def _paged_attention(page_table, q8, ckv_new8, kpe_new_t, wq, wuv, cache_ckv, cache_kpe_t, n_new):
    nb, n_pages = page_table.shape
    rows = n_pages * PAGE_SIZE + NEW_PAD
    grid_spec = pltpu.PrefetchScalarGridSpec(
        num_scalar_prefetch=1,
        grid=(nb,),
        in_specs=[
            pl.BlockSpec((1, T_PAD, D_HEADS), lambda b, pt: (b, 0, 0)),
            pl.BlockSpec((1, T_PAD, KV_LORA), lambda b, pt: (b, 0, 0)),
            pl.BlockSpec((1, QK_ROPE, NEW_PAD), lambda b, pt: (b, 0, 0)),
            pl.BlockSpec(wq.shape, lambda b, pt: (0, 0)),
            pl.BlockSpec(wuv.shape, lambda b, pt: (0, 0)),
            pl.BlockSpec(memory_space=pl.ANY),
            pl.BlockSpec(memory_space=pl.ANY),
        ],
        out_specs=pl.BlockSpec((1, T_PAD, D_HEADS), lambda b, pt: (b, 0, 0)),
        scratch_shapes=[
            pltpu.VMEM((2, rows, KV_LORA), F32),
            pltpu.VMEM((2, QK_ROPE, rows), F32),
            pltpu.VMEM((rows, KV_LORA), BF16),
            pltpu.VMEM((MLA_HEADS * T_PAD, rows), F32),
            pltpu.SemaphoreType.DMA((2, 2)),
        ],
    )
    return pl.pallas_call(
        functools.partial(_paged_kernel, n_pages=n_pages, n_new=n_new),
        grid_spec=grid_spec,
        out_shape=jax.ShapeDtypeStruct((nb, T_PAD, D_HEADS), BF16),
        compiler_params=_params("arbitrary"),
        name="paged_attention",
    )(page_table, q8, ckv_new8, kpe_new_t, wq, wuv, cache_ckv, cache_kpe_t)


def _group_norm_gate(o, gate, gain, bias):
    mu = jnp.mean(o, axis=-1, keepdims=True)
    d = o - mu
    var = jnp.mean(d * d, axis=-1, keepdims=True)
    return _silu(gate) * (d * lax.rsqrt(var + LN_EPS) * gain + bias)


def _ret_kernel(rq_ref, rk_ref, rv_ref, rg_ref, din_ref, qd_ref, kd_ref, gc_ref, gg_ref, gb_ref, s0_ref,
                ro_ref, so_ref, s_ref, *, chunks):
    i = pl.program_id(1)

    @pl.when(i == 0)
    def _():
        s_ref[...] = s0_ref[0]

    for hh in range(RET_HEADS):
        sl = slice(hh * RET_DK, (hh + 1) * RET_DK)
        din = din_ref[:, sl]
        qd = qd_ref[:, sl]
        kd = kd_ref[:, sl]
        gc = gc_ref[:, sl]
        gain = gg_ref[:, sl]
        bias = gb_ref[:, sl]
        s = s_ref[hh]
        for c in range(chunks):
            rows = slice(c * RET_CHUNK, (c + 1) * RET_CHUNK)
            qh = rq_ref[rows, sl]
            kh = rk_ref[rows, sl]
            vh = rv_ref[rows, sl]
            inner = lax.dot_general(qh.astype(BF16), kh.astype(BF16), _NT, preferred_element_type=F32) * din
            o = (jnp.dot(inner.astype(BF16), vh, preferred_element_type=F32)
                 + jnp.dot((qh * qd).astype(BF16), s.astype(BF16), preferred_element_type=F32))
            s = s * gc + lax.dot_general((kh * kd).astype(BF16), vh, _TN, preferred_element_type=F32)
            ro_ref[rows, sl] = _group_norm_gate(o, rg_ref[rows, sl], gain, bias).astype(BF16)
        s_ref[hh] = s

    @pl.when(i == pl.num_programs(1) - 1)
    def _():
        so_ref[0] = s_ref[...]


def _retention_prompt(rq, rk, rv, rg, dec, gn_gain, gn_bias, state0, batch, seq, chunks):
    rows = chunks * RET_CHUNK
    steps = seq // rows
    rspec = pl.BlockSpec((rows, D_RET), lambda b, i: (b * steps + i, 0))
    sspec = pl.BlockSpec((1, RET_HEADS, RET_DK, RET_DV), lambda b, i: (b, 0, 0, 0))
    consts = [dec["din"], dec["qdec"], dec["kdec"], dec["gc"], gn_gain, gn_bias]
    return pl.pallas_call(
        functools.partial(_ret_kernel, chunks=chunks),
        grid=(batch, steps),
        in_specs=[rspec] * 4 + [_const_spec(c.shape) for c in consts] + [sspec],
        out_specs=[rspec, sspec],
        out_shape=[jax.ShapeDtypeStruct((batch * seq, D_RET), BF16),
                   jax.ShapeDtypeStruct((batch, RET_HEADS, RET_DK, RET_DV), F32)],
        scratch_shapes=[pltpu.VMEM((RET_HEADS, RET_DK, RET_DV), F32)],
        compiler_params=_params("parallel", "arbitrary"),
        name="retention_prompt",
    )(rq, rk, rv, rg, *consts, state0)


RS_BATCH = 8


def _ret_sample_kernel(qk_ref, v_ref, rg_ref, g_ref, gg_ref, gb_ref, s0_ref, ro_ref, so_ref, *, n_tok):
    def one_seq(bi, carry):
        qk = qk_ref[bi]
        for hh in range(RET_HEADS):
            g = g_ref[hh][0:1, :]
            idx = bi * RET_HEADS + hh
            s = s0_ref[idx]
            vt = v_ref[idx]
            outs = []
            for t in range(n_tok):
                cq = hh * n_tok + t
                ck = RET_HEADS * n_tok + cq
                kcol = jnp.broadcast_to(qk[:, ck:ck + 1], (RET_DK, RET_DV))
                qcol = jnp.broadcast_to(qk[:, cq:cq + 1], (RET_DK, RET_DV))
                s = s * g + kcol * vt[t:t + 1, :]
                outs.append(jnp.sum(qcol * s, axis=0, keepdims=True))
            so_ref[idx] = s
            o = jnp.concatenate(outs + [jnp.zeros((T_PAD - n_tok, RET_DV), F32)], axis=0)
            sl = slice(hh * RET_DV, (hh + 1) * RET_DV)
            ro_ref[idx] = _group_norm_gate(o, rg_ref[idx], gg_ref[:, sl], gb_ref[:, sl]).astype(BF16)
        return carry

    lax.fori_loop(0, RS_BATCH, one_seq, 0)


def _retention_sample(qk_t, v, rg, gtab, gn_gain, gn_bias, state0, n_tok):
    nb = qk_t.shape[0]
    g_per = RS_BATCH * RET_HEADS
    gspec = pl.BlockSpec((g_per, T_PAD, RET_DV), lambda i: (i, 0, 0))
    sspec = pl.BlockSpec((g_per, RET_DK, RET_DV), lambda i: (i, 0, 0))
    return pl.pallas_call(
        functools.partial(_ret_sample_kernel, n_tok=n_tok),
        grid=(nb // RS_BATCH,),
        in_specs=[pl.BlockSpec((RS_BATCH, RET_DK, LANES), lambda i: (i, 0, 0)), gspec, gspec,
                  _const_spec(gtab.shape), _const_spec(gn_gain.shape), _const_spec(gn_bias.shape), sspec],
        out_specs=[gspec, sspec],
        out_shape=[jax.ShapeDtypeStruct((nb * RET_HEADS, T_PAD, RET_DV), BF16),
                   jax.ShapeDtypeStruct((nb * RET_HEADS, RET_DK, RET_DV), F32)],
        compiler_params=_params("parallel"),
        name="retention_sample",
    )(qk_t, v, rg, gtab, gn_gain, gn_bias, state0)


def _outproj_ln_kernel(a_ref, ro_ref, h_ref, wa_ref, wr_ref, g_ref, b_ref, o_ref):
    y = (jnp.dot(a_ref[...], wa_ref[...], preferred_element_type=F32)
         + jnp.dot(ro_ref[...], wr_ref[...], preferred_element_type=F32))
    o_ref[...] = _layer_norm(ALPHA * h_ref[...] + y, g_ref[...], b_ref[...])


def _outproj_ln(a, ro, h, wa, wr, g, b, tm):
    m = h.shape[0]

    def row(n):
        return pl.BlockSpec((tm, n), lambda i: (i, 0))

    return pl.pallas_call(
        _outproj_ln_kernel,
        grid=(m // tm,),
        in_specs=[row(D_HEADS), row(D_RET), row(D_MODEL), _const_spec(wa.shape), _const_spec(wr.shape),
                  _const_spec(g.shape), _const_spec(b.shape)],
        out_specs=row(D_MODEL),
        out_shape=jax.ShapeDtypeStruct((m, D_MODEL), F32),
        compiler_params=_params("parallel"),
        name="outproj_ln",
    )(a, ro, h, wa, wr, g, b)


HALO = 16


def _pool_tail(pooled_groups, x, pw_ref, pb_ref, ps_ref, g_ref, b_ref):
    ys = [jnp.dot(p.astype(BF16), pw_ref[gi], preferred_element_type=F32) for gi, p in enumerate(pooled_groups)]
    y = (jnp.concatenate(ys, axis=-1) + pb_ref[...]) * ps_ref[...]
    return _layer_norm(ALPHA * x + y, g_ref[...], b_ref[...])


def _pool_prompt_kernel(h_ref, halo_ref, pre_ref, pw_ref, pb_ref, ps_ref, g_ref, b_ref, o_ref, xs_ref,
                        *, tm, tiles, start):
    t = pl.program_id(0) % tiles
    x = h_ref[...]
    xs_ref[0:HALO, :] = jnp.where(t == 0, pre_ref[0], halo_ref[...])
    xs_ref[HALO:, :] = x
    pos = start + t * tm + lax.broadcasted_iota(jnp.int32, (tm, 1), 0)
    pooled = []
    for gi, wl in enumerate(POOL_WINDOWS):
        sl = slice(gi * POOL_GROUP, (gi + 1) * POOL_GROUP)
        acc = x[:, sl]
        for d in range(1, wl):
            acc = acc + xs_ref[HALO - d:HALO - d + tm, sl]
        cnt = jnp.minimum(pos + 1, wl).astype(F32)
        pooled.append(acc / cnt - x[:, sl])
    o_ref[...] = _pool_tail(pooled, x, pw_ref, pb_ref, ps_ref, g_ref, b_ref)


def _pool_prompt(h, prefix16, pw, pb, ps, g, b, batch, seq, start, tm):
    tiles = seq // tm
    per = tm // HALO
    consts = [pw, pb, ps, g, b]
    return pl.pallas_call(
        functools.partial(_pool_prompt_kernel, tm=tm, tiles=tiles, start=start),
        grid=(batch * tiles,),
        in_specs=[pl.BlockSpec((tm, D_MODEL), lambda i: (i, 0)),
                  pl.BlockSpec((HALO, D_MODEL), lambda i: (jnp.maximum(i * per - 1, 0), 0)),
                  pl.BlockSpec((1, HALO, D_MODEL), lambda i: (i // tiles, 0, 0))]
                 + [_const_spec(c.shape) for c in consts],
        out_specs=pl.BlockSpec((tm, D_MODEL), lambda i: (i, 0)),
        out_shape=jax.ShapeDtypeStruct((batch * seq, D_MODEL), F32),
        scratch_shapes=[pltpu.VMEM((HALO + tm, D_MODEL), F32)],
        compiler_params=_params("parallel"),
        name="pool_prompt",
    )(h, h, prefix16, *consts)


def _pool_sample_kernel(xs_ref, pw_ref, pb_ref, ps_ref, g_ref, b_ref, o_ref, *, n_tok, start):
    nb = xs_ref.shape[1]
    x = jnp.concatenate([xs_ref[HALO + t] for t in range(n_tok)], axis=0)
    pooled = []
    for gi, wl in enumerate(POOL_WINDOWS):
        sl = slice(gi * POOL_GROUP, (gi + 1) * POOL_GROUP)
        parts = []
        for t in range(n_tok):
            acc = xs_ref[HALO + t, :, sl]
            for d in range(1, wl):
                acc = acc + xs_ref[HALO + t - d, :, sl]
            parts.append(acc / float(min(start + t + 1, wl)))
        pooled.append(jnp.concatenate(parts, axis=0) - x[:, sl])
    y = _pool_tail(pooled, x, pw_ref, pb_ref, ps_ref, g_ref, b_ref)
    for t in range(n_tok):
        o_ref[t] = y[t * nb:(t + 1) * nb]


def _pool_sample(xs_t, pw, pb, ps, g, b, n_tok, start):
    nb = xs_t.shape[1]
    args = [xs_t, pw, pb, ps, g, b]
    return pl.pallas_call(
        functools.partial(_pool_sample_kernel, n_tok=n_tok, start=start),
        grid=(1,),
        in_specs=[_const_spec(a.shape) for a in args],
        out_specs=_const_spec((n_tok, nb, D_MODEL)),
        out_shape=jax.ShapeDtypeStruct((n_tok, nb, D_MODEL), F32),
        compiler_params=_params("arbitrary"),
        name="pool_sample",
    )(*args)


def _head_pad(w, width):
    r, nh, d = w.shape
    out = jnp.zeros((r, nh, HEAD_PAD), w.dtype).at[:, :, :d].set(w)
    return out.reshape(r, nh * HEAD_PAD)[:, :width]


def _mixer_weights(mix_w_in, q_norm, kv_norm, w_uq, w_uk, w_uv, mix_w_out):
    offs = np.concatenate([[0], np.cumsum(SPLIT_SIZES)])
    wq, wckv, wkpe, wrq, wrk, wrv, wrg = [mix_w_in[:, offs[i]:offs[i + 1]] for i in range(7)]
    half = QK_ROPE // 2
    z_lo = jnp.zeros((D_MODEL, QK_NOPE), F32)
    z_hi = jnp.zeros((D_MODEL, HEAD_PAD - QK_NOPE - QK_ROPE), F32)
    kpe_blk = jnp.concatenate([z_lo, wkpe, z_hi], axis=1)
    kpe_swp = jnp.concatenate([z_lo, -wkpe[:, half:], wkpe[:, :half], z_hi], axis=1)
    w_main = jnp.concatenate([wq, wckv, kpe_blk, kpe_swp, wrq, wrk, wrv, wrg], axis=1).astype(BF16)
    pe = w_uq[:, :, QK_NOPE:]
    uq1 = _head_pad(w_uq, D_HEADS)
    uq2 = _head_pad(jnp.concatenate([jnp.zeros_like(w_uq[:, :, :QK_NOPE]), -pe[:, :, half:], pe[:, :, :half]], axis=2),
                    D_HEADS)
    lane = np.arange(HEAD_PAD)
    sel = ((lane[:, None] == lane[None, :]) & (lane[:, None] >= QK_NOPE) & (lane[:, None] < QK_NOPE + QK_ROPE))
    e_mat = jnp.asarray(np.tile(sel.astype(np.float32), (1, MLA_HEADS)))
    wk = jnp.concatenate([_head_pad(w_uk, D_HEADS), e_mat], axis=0)
    wv = _head_pad(w_uv, D_HEADS)
    vone = jnp.asarray((np.arange(D_HEADS) % HEAD_PAD == V_DIM).astype(np.float32))[None, :]
    wa = jnp.zeros((MLA_HEADS, HEAD_PAD, D_MODEL), F32).at[:, :V_DIM, :].set(
        mix_w_out[:MLA_HEADS * V_DIM].reshape(MLA_HEADS, V_DIM, D_MODEL)).reshape(D_HEADS, D_MODEL)
    wr = mix_w_out[MLA_HEADS * V_DIM:]
    wuk_t = jnp.zeros((MLA_HEADS, HEAD_PAD, KV_LORA), F32).at[:, :QK_NOPE, :].set(
        jnp.transpose(w_uk, (1, 2, 0))).reshape(D_HEADS, KV_LORA)
    r = np.arange(D_HEADS) % HEAD_PAD
    e_pe = ((r[:, None] - QK_NOPE) == np.arange(LANES)[None, :]) & (r[:, None] >= QK_NOPE) & (r[:, None] < QK_NOPE + QK_ROPE)
    wq_abs = jnp.concatenate([wuk_t, jnp.asarray(e_pe.astype(np.float32))], axis=1)
    return {
        "w_main": w_main, "q_norm": q_norm[None, :], "kv_norm": kv_norm[None, :],
        "uq1": uq1.astype(BF16), "uq2": uq2.astype(BF16), "wk": wk.astype(BF16), "wv": wv.astype(BF16),
        "vone": vone, "wa": wa.astype(BF16), "wr": wr.astype(BF16),
        "wq_abs": wq_abs.astype(BF16),
    }


def _rope_tables(pos):
    def angles(r):
        inv = 1.0 / (ROPE_BASE ** (jnp.arange(0, r, 2, dtype=F32) / r))
        return pos.astype(F32)[:, None] * inv[None, :]

    n = pos.shape[0]
    a = angles(QK_ROPE)
    c, s = jnp.cos(a), jnp.sin(a)
    hi = HEAD_PAD - QK_NOPE - QK_ROPE
    cq = jnp.concatenate([jnp.ones((n, QK_NOPE), F32), c, c, jnp.ones((n, hi), F32)], axis=1)
    sq = jnp.concatenate([jnp.zeros((n, QK_NOPE), F32), s, s, jnp.zeros((n, hi), F32)], axis=1)
    a = angles(RET_DK)
    c, s = jnp.cos(a), jnp.sin(a)
    return cq, sq, jnp.concatenate([c, c], axis=1), jnp.concatenate([-s, s], axis=1)


def _ret_log_decay():
    return jnp.log(1.0 - 2.0 ** (-5.0 - jnp.arange(RET_HEADS, dtype=F32)))


def _ret_decay_tables(chunk):
    log_g = _ret_log_decay()
    idx = jnp.arange(chunk, dtype=F32)
    diff = idx[:, None] - idx[None, :]
    d_in = jnp.where(diff >= 0, jnp.exp(jnp.maximum(diff, 0.0)[None] * log_g[:, None, None]), 0.0)
    q_dec = jnp.exp((idx + 1.0)[None, :] * log_g[:, None])
    k_dec = jnp.exp((chunk - 1.0 - idx)[None, :] * log_g[:, None])
    g_c = jnp.exp(chunk * log_g)
    lanes = (chunk, RET_HEADS * RET_DK)
    return {
        "din": jnp.transpose(d_in, (1, 0, 2)).reshape(chunk, RET_HEADS * chunk),
        "qdec": jnp.broadcast_to(q_dec.T[:, :, None], (chunk, RET_HEADS, RET_DK)).reshape(lanes),
        "kdec": jnp.broadcast_to(k_dec.T[:, :, None], (chunk, RET_HEADS, RET_DK)).reshape(lanes),
        "gc": jnp.broadcast_to(g_c[:, None], (RET_HEADS, RET_DV)).reshape(1, RET_HEADS * RET_DV),
    }


def _trunk(x, start, ret_state0, pool_prefix, mla_cache, w, mw):
    batch, seq, _ = x.shape
    m = batch * seq
    prompt = mla_cache is None
    tm = min(512, m)
    h = x.reshape(m, D_MODEL)

    def ffn(h, layer, half):
        return _ffn_ln(h, w["wg"], w["wu"], w["wd"], layer, half,
                       w["ln_gain"][layer, 2 * half][None, :], w["ln_bias"][layer, 2 * half][None, :], tm)

    h = ffn(h, 0, 0)
    if prompt:
        pos = start + jnp.arange(seq, dtype=jnp.int32)
    else:
        pos = start + (jnp.arange(m, dtype=jnp.int32) % seq)
    q, k, v, ckv, kpe, rq, rk, rv, rg = _mixer_prep(h, mw, _rope_tables(pos), tm)
    kpe = kpe[:, QK_NOPE:QK_NOPE + QK_ROPE]
    gn_gain = w["ret_gn_gain"][0][None, :]
    gn_bias = w["ret_gn_bias"][0][None, :]
    if prompt:
        a = _flash_attention(q, k, v, batch, seq, min(1024, seq))
        chunk = RET_CHUNK if seq % RET_CHUNK == 0 else seq
        assert chunk == RET_CHUNK
        ro, ret_state = _retention_prompt(rq, rk, rv, rg, _ret_decay_tables(chunk), gn_gain, gn_bias,
                                          ret_state0, batch, seq, 8)
    else:
        assert seq <= T_PAD and seq % RET_CHUNK != 0
        cache_ckv, cache_kpe, page_table = mla_cache
        pad_t = ((0, 0), (0, T_PAD - seq), (0, 0))
        q8 = jnp.pad(q.reshape(batch, seq, D_HEADS), pad_t)
        cn8 = jnp.pad(ckv.reshape(batch, seq, KV_LORA), pad_t)
        kn_t = jnp.pad(jnp.swapaxes(kpe.reshape(batch, seq, QK_ROPE), 1, 2), ((0, 0), (0, 0), (0, NEW_PAD - seq)))
        a8 = _paged_attention(page_table, q8, cn8, kn_t, mw["wq_abs"], mw["wv"], cache_ckv,
                              jnp.swapaxes(cache_kpe, 1, 2), seq)
        a = a8[:, :seq].reshape(m, D_HEADS)

        def heads_first(t):
            t = jnp.transpose(t.reshape(batch, seq, RET_HEADS, RET_DV), (0, 2, 1, 3))
            return jnp.pad(t.reshape(batch * RET_HEADS, seq, RET_DV), pad_t)

        def dk_first(t):
            return jnp.transpose(t.reshape(batch, seq, RET_HEADS, RET_DK), (0, 3, 2, 1)).reshape(
                batch, RET_DK, RET_HEADS * seq)

        qk_t = jnp.concatenate([dk_first(rq), dk_first(rk)], axis=2)
        qk_t = jnp.pad(qk_t, ((0, 0), (0, 0), (0, LANES - 2 * RET_HEADS * seq)))
        gtab = jnp.broadcast_to(jnp.exp(_ret_log_decay())[:, None, None], (RET_HEADS, 8, RET_DV))
        ro8, ret_state = _retention_sample(qk_t, heads_first(rv.astype(F32)), heads_first(rg), gtab,
                                           gn_gain, gn_bias,
                                           ret_state0.reshape(batch * RET_HEADS, RET_DK, RET_DV), seq)
        ro = jnp.transpose(ro8[:, :seq].reshape(batch, RET_HEADS, seq, RET_DV), (0, 2, 1, 3)).reshape(m, D_RET)
        ret_state = ret_state.reshape(batch, RET_HEADS, RET_DK, RET_DV)
    h = _outproj_ln(a, ro, h, mw["wa"], mw["wr"], w["ln_gain"][0, 1][None, :], w["ln_bias"][0, 1][None, :], tm)
    h = ffn(h, 0, 1)

    h = ffn(h, 1, 0)
    xp_tail = jnp.concatenate([pool_prefix, h.reshape(batch, seq, D_MODEL)], axis=1)[:, -POOL_PREFIX:]
    prefix16 = jnp.pad(pool_prefix, ((0, 0), (HALO - POOL_PREFIX, 0), (0, 0)))
    pool_args = (w["pool_w"], w["pool_b"][0][None, :], w["pool_scale"][0][None, :],
                 w["ln_gain"][1, 1][None, :], w["ln_bias"][1, 1][None, :])
    if prompt:
        h = _pool_prompt(h, prefix16, *pool_args, batch, seq, start, tm)
    else:
        xs_t = jnp.transpose(jnp.concatenate([prefix16, h.reshape(batch, seq, D_MODEL)], axis=1), (1, 0, 2))
        h = jnp.transpose(_pool_sample(xs_t, *pool_args, seq, start), (1, 0, 2)).reshape(m, D_MODEL)
    h = ffn(h, 1, 1)
    return (h.reshape(batch, seq, D_MODEL), ckv.reshape(1, batch, seq, KV_LORA),
            kpe.reshape(1, batch, seq, QK_ROPE), ret_state[None], xp_tail[None])


def kernel(x_prompt, x_sample, cache_mla_ckv, cache_mla_kpe, state_ret, state_pool, page_table, ffn_w_gate, ffn_w_up, ffn_w_down, ln_gain, ln_bias, mix_w_in, mla_q_norm, mla_kv_norm, mla_w_uq, mla_w_uk, mla_w_uv, ret_gn_gain, ret_gn_bias, mix_w_out, pool_w, pool_b, pool_scale):
    assert DEPTH == 2 and mix_w_in.shape[0] == 1 and pool_w.shape[0] == 1
    w = {
        "wg": ffn_w_gate.astype(BF16), "wu": ffn_w_up.astype(BF16), "wd": ffn_w_down.astype(BF16),
        "ln_gain": ln_gain, "ln_bias": ln_bias, "ret_gn_gain": ret_gn_gain, "ret_gn_bias": ret_gn_bias,
        "pool_w": pool_w[0].astype(BF16), "pool_b": pool_b, "pool_scale": pool_scale,
    }
    mw = _mixer_weights(mix_w_in[0], mla_q_norm[0], mla_kv_norm[0], mla_w_uq[0], mla_w_uk[0], mla_w_uv[0],
                        mix_w_out[0])
    bp = x_prompt.shape[0]
    zero_ret = jnp.zeros((bp, RET_HEADS, RET_DK, RET_DV), F32)
    zero_pool = jnp.zeros((bp, POOL_PREFIX, D_MODEL), x_prompt.dtype)
    y_p, ckv_p, kpe_p, ret_p, pool_p = _trunk(x_prompt, 0, zero_ret, zero_pool, None, w, mw)
    y_s, ckv_s, kpe_s, ret_s, pool_s = _trunk(x_sample, PAST_LEN, state_ret[0], state_pool[0],
                                              (cache_mla_ckv[0], cache_mla_kpe[0], page_table), w, mw)
    return (y_p, y_s, ckv_p, kpe_p, ckv_s, kpe_s, ret_p, ret_s, pool_p, pool_s)
```

```python
import functools

import numpy as np
import jax
import jax.numpy as jnp
from jax import lax
from jax.experimental import pallas as pl
from jax.experimental.pallas import tpu as pltpu

F32 = jnp.float32
BF16 = jnp.bfloat16

D_MODEL = 1024
DEPTH = 2
PAST_LEN = 8192
PAGE_SIZE = 128
ALPHA = (2 * DEPTH) ** 0.25
D_FF = 2816
MLA_HEADS = 8
Q_LORA = 512
KV_LORA = 256
QK_NOPE = 64
QK_ROPE = 32
V_DIM = 64
RET_HEADS = 4
RET_DK = 128
RET_DV = 128
RET_CHUNK = 128
POOL_WINDOWS = (2, 4, 8, 16)
POOL_GROUPS = 4
POOL_GROUP = D_MODEL // POOL_GROUPS
POOL_PREFIX = 15
ROPE_BASE = 10000.0
LN_EPS = 1e-5
RMS_EPS = 1e-6
SPLIT_SIZES = (Q_LORA, KV_LORA, QK_ROPE, RET_HEADS * RET_DK, RET_HEADS * RET_DK,
               RET_HEADS * RET_DV, RET_HEADS * RET_DV)
ATT_SCALE = (QK_NOPE + QK_ROPE) ** -0.5 * 1.4426950408889634

LANES = 128
HEAD_PAD = LANES
D_HEADS = MLA_HEADS * HEAD_PAD
D_RET = RET_HEADS * RET_DV
NEG = -1e30
VMEM_LIMIT = 56 * 1024 * 1024

_NT = (((1,), (1,)), ((), ()))
_TN = (((0,), (0,)), ((), ()))


def _params(*sem):
    return pltpu.CompilerParams(dimension_semantics=sem, vmem_limit_bytes=VMEM_LIMIT)


def _const_spec(shape):
    nd = len(shape)
    return pl.BlockSpec(shape, lambda *_: (0,) * nd, pipeline_mode=pl.Buffered(1))


def _layer_norm(y, g, b):
    mu = jnp.mean(y, axis=-1, keepdims=True)
    d = y - mu
    var = jnp.mean(d * d, axis=-1, keepdims=True)
    return d * lax.rsqrt(var + LN_EPS) * g + b


def _silu(x):
    return x * jax.nn.sigmoid(x)


FFN_CHUNK = 256
FFN_ROWS = 512


def _ffn_ln_kernel(x_ref, wg_ref, wu_ref, wd_ref, g_ref, b_ref, o_ref):
    x = x_ref[...]
    xb = x.astype(BF16)
    acc = None
    for c in range(D_FF // FFN_CHUNK):
        sl = slice(c * FFN_CHUNK, (c + 1) * FFN_CHUNK)
        g = jnp.dot(xb, wg_ref[:, sl], preferred_element_type=F32)
        u = jnp.dot(xb, wu_ref[:, sl], preferred_element_type=F32)
        a = (_silu(g) * u).astype(BF16)
        d = jnp.dot(a, wd_ref[sl, :], preferred_element_type=F32)
        acc = d if acc is None else acc + d
    o_ref[...] = _layer_norm(ALPHA * x + 0.5 * acc, g_ref[...], b_ref[...])


def _ffn_ln(x, wg, wu, wd, layer, half, g, b, tm):
    m = x.shape[0]
    row = pl.BlockSpec((tm, D_MODEL), lambda i: (i, 0))

    def wspec(w):
        return pl.BlockSpec((None, None) + w.shape[2:], lambda i: (layer, half, 0, 0),
                            pipeline_mode=pl.Buffered(1))

    return pl.pallas_call(
        _ffn_ln_kernel,
        grid=(m // tm,),
        in_specs=[row, wspec(wg), wspec(wu), wspec(wd), _const_spec(g.shape), _const_spec(b.shape)],
        out_specs=row,
        out_shape=jax.ShapeDtypeStruct((m, D_MODEL), F32),
        compiler_params=_params("parallel"),
        name="ffn_ln",
    )(x, wg, wu, wd, g, b)


_C_QL, _C_CKV, _C_KPE, _C_KPS, _C_RQ, _C_RK, _C_RV, _C_RG, _C_END = (
    0, 512, 768, 896, 1024, 1536, 2048, 2560, 3072)


def _prep_kernel(h_ref, w_ref, qn_ref, kvn_ref, uq1_ref, uq2_ref, wk_ref, wvt_ref,
                 cq_ref, sq_ref, cr_ref, sr_ref,
                 q_ref, k_ref, vt_ref, ckv_ref, kpe_ref, rq_ref, rk_ref, rv_ref, rg_ref):
    xb = h_ref[...].astype(BF16)

    def proj(a, b):
        return jnp.dot(xb, w_ref[:, a:b], preferred_element_type=F32)

    cq = cq_ref[...]
    sq = sq_ref[...]
    ql = proj(_C_QL, _C_CKV)
    qn = (ql * lax.rsqrt(jnp.mean(ql * ql, axis=-1, keepdims=True) + RMS_EPS) * qn_ref[...]).astype(BF16)
    for hh in range(MLA_HEADS):
        sl = slice(hh * HEAD_PAD, (hh + 1) * HEAD_PAD)
        a = jnp.dot(qn, uq1_ref[:, sl], preferred_element_type=F32)
        b = jnp.dot(qn, uq2_ref[:, sl], preferred_element_type=F32)
        q_ref[:, sl] = ((a * cq + b * sq) * ATT_SCALE).astype(BF16)
    c = proj(_C_CKV, _C_KPE)
    ckv = c * lax.rsqrt(jnp.mean(c * c, axis=-1, keepdims=True) + RMS_EPS) * kvn_ref[...]
    ckv_ref[...] = ckv
    cb = ckv.astype(BF16)
    kpe = proj(_C_KPE, _C_KPS) * cq + proj(_C_KPS, _C_RQ) * sq
    kpe_ref[...] = kpe
    kb = kpe.astype(BF16)
    k = (jnp.dot(cb, wk_ref[0:KV_LORA, :], preferred_element_type=F32)
         + jnp.dot(kb, wk_ref[KV_LORA:KV_LORA + HEAD_PAD, :], preferred_element_type=F32))
    k_ref[...] = k.astype(BF16)
    vt = lax.dot_general(wvt_ref[...], cb, _NT, preferred_element_type=F32)
    head_row = lax.broadcasted_iota(jnp.int32, vt.shape, 0) % HEAD_PAD
    vt_ref[0] = jnp.where(head_row == V_DIM, 1.0, vt).astype(BF16)
    cr = cr_ref[...]
    sr = sr_ref[...]
    rq = proj(_C_RQ, _C_RK)
    rk = proj(_C_RK, _C_RV)
    for hh in range(RET_HEADS):
        sl = slice(hh * RET_DK, (hh + 1) * RET_DK)
        xq = rq[:, sl]
        xk = rk[:, sl]
        rq_ref[:, sl] = xq * cr + pltpu.roll(xq, RET_DK // 2, 1) * sr
        rk_ref[:, sl] = (xk * cr + pltpu.roll(xk, RET_DK // 2, 1) * sr) * (RET_DK ** -0.5)
    rv_ref[...] = proj(_C_RV, _C_RG).astype(BF16)
    rg_ref[...] = proj(_C_RG, _C_END)


def _mixer_prep(h, mw, tabs, tm, vt_blk):
    m = h.shape[0]
    cq, sq, cr, sr = tabs
    tab_blocks = cq.shape[0] // tm
    per = vt_blk // tm

    def row(n):
        return pl.BlockSpec((tm, n), lambda i: (i, 0))

    tab = pl.BlockSpec((tm, LANES), lambda i: (i % tab_blocks, 0))
    consts = [mw["w_main"], mw["q_norm"], mw["kv_norm"], mw["uq1"], mw["uq2"], mw["wk"], mw["wv_t"]]
    out_shape = [
        jax.ShapeDtypeStruct((m, D_HEADS), BF16),
        jax.ShapeDtypeStruct((m, D_HEADS), BF16),
        jax.ShapeDtypeStruct((m // vt_blk, D_HEADS, vt_blk), BF16),
        jax.ShapeDtypeStruct((m, KV_LORA), F32),
        jax.ShapeDtypeStruct((m, LANES), F32),
        jax.ShapeDtypeStruct((m, D_RET), F32),
        jax.ShapeDtypeStruct((m, D_RET), F32),
        jax.ShapeDtypeStruct((m, D_RET), BF16),
        jax.ShapeDtypeStruct((m, D_RET), F32),
    ]
    return pl.pallas_call(
        _prep_kernel,
        grid=(m // tm,),
        in_specs=[row(D_MODEL)] + [_const_spec(c.shape) for c in consts] + [tab] * 4,
        out_specs=[pl.BlockSpec((1, D_HEADS, tm), lambda i: (i // per, 0, i % per)) if len(s.shape) == 3
                   else row(s.shape[1]) for s in out_shape],
        out_shape=out_shape,
        compiler_params=_params("parallel"),
        name="mixer_prep",
    )(h, *consts, cq, sq, cr, sr)


FLASH_HEADS = 2


def _flash_kernel(q_ref, k_ref, vt_ref, o_ref, *, blk):
    i = pl.program_id(2)
    lanes = [slice(a * HEAD_PAD, (a + 1) * HEAD_PAD) for a in range(FLASH_HEADS)]
    qs = [q_ref[:, sl] for sl in lanes]

    def scores_t(j):
        rows = pl.ds(pl.multiple_of(j * blk, blk), blk)
        return tuple(lax.dot_general(k_ref[rows, sl], q, _NT, preferred_element_type=F32)
                     for q, sl in zip(qs, lanes))

    def update(st, j, m, acc, sl):
        m_new = jnp.maximum(m, jnp.max(st, axis=0, keepdims=True))
        p = jnp.exp2(st - m_new).astype(BF16)
        acc = acc * jnp.exp2(m - m_new) + jnp.dot(vt_ref[j, sl, :], p, preferred_element_type=F32)
        return m_new, acc

    def body(j, carry):
        ms, accs = carry
        new = [update(st, j, m, acc, sl) for st, m, acc, sl in zip(scores_t(j), ms, accs, lanes)]
        return tuple(n[0] for n in new), tuple(n[1] for n in new)

    m0 = tuple(jnp.full((1, blk), NEG, F32) for _ in lanes)
    acc0 = tuple(jnp.zeros((HEAD_PAD, blk), F32) for _ in lanes)
    ms, accs = lax.fori_loop(0, i, body, (m0, acc0))
    key = lax.broadcasted_iota(jnp.int32, (blk, blk), 0)
    qry = lax.broadcasted_iota(jnp.int32, (blk, blk), 1)
    for st, m, acc, sl in zip(scores_t(i), ms, accs, lanes):
        m, acc = update(jnp.where(key <= qry, st, NEG), i, m, acc, sl)
        o_ref[:, sl] = (acc / acc[V_DIM:V_DIM + 1, :]).T.astype(BF16)


def _flash_attention(q, k, vt, batch, seq, blk):
    nq = seq // blk
    width = FLASH_HEADS * HEAD_PAD
    qspec = pl.BlockSpec((blk, width), lambda b, h, i: (b * nq + i, h))
    kspec = pl.BlockSpec((seq, width), lambda b, h, i: (b, h))
    vspec = pl.BlockSpec((nq, width, blk), lambda b, h, i: (b, h, 0))
    return pl.pallas_call(
        functools.partial(_flash_kernel, blk=blk),
        grid=(batch, MLA_HEADS // FLASH_HEADS, nq),
        in_specs=[qspec, kspec, vspec],
        out_specs=qspec,
        out_shape=jax.ShapeDtypeStruct((batch * seq, D_HEADS), BF16),
        compiler_params=_params("parallel", "parallel", "arbitrary"),
        name="flash_attention",
    )(q, k, vt)


T_PAD = 8
NEW_PAD = PAGE_SIZE
KEY_CHUNK = 1024


def _paged_kernel(pt_ref, q_ref, cn_ref, kn_ref, wq_ref, wuv_ref, ckv_hbm, kpe_hbm, o_ref,
                  ckv_buf, kpe_buf, kb_ref, s_ref, p_ref, sem, *, n_pages, n_new):
    b = pl.program_id(0)
    nb = pl.num_programs(0)
    past = n_pages * PAGE_SIZE
    slot = b % 2

    def page_copies(bb, sl, p):
        page = pt_ref[bb, p]
        rows = pl.ds(p * PAGE_SIZE, PAGE_SIZE)
        return (pltpu.make_async_copy(ckv_hbm.at[page], ckv_buf.at[sl, rows, :], sem.at[sl, 0]),
                pltpu.make_async_copy(kpe_hbm.at[page], kpe_buf.at[sl, :, rows], sem.at[sl, 1]))

    def start_fetch(bb, sl):
        for p in range(n_pages):
            for cp in page_copies(bb, sl, p):
                cp.start()

    def wait_fetch(bb, sl):
        for p in range(n_pages):
            for cp in page_copies(bb, sl, p):
                cp.wait()

    @pl.when(b == 0)
    def _():
        ckv_buf[:, past:, :] = jnp.zeros((2, NEW_PAD, KV_LORA), F32)
        start_fetch(0, 0)

    @pl.when(b + 1 < nb)
    def _():
        start_fetch(b + 1, 1 - slot)

    q8 = q_ref[0]
    qrep = jnp.concatenate([q8] * MLA_HEADS, axis=0)
    n_rows = MLA_HEADS * T_PAD
    row_h = lax.broadcasted_iota(jnp.int32, (n_rows, D_HEADS), 0) // T_PAD
    col_h = lax.broadcasted_iota(jnp.int32, (n_rows, D_HEADS), 1) // HEAD_PAD
    qm = jnp.where(row_h == col_h, qrep, jnp.zeros_like(qrep))
    ql = jnp.dot(qm, wq_ref[...], preferred_element_type=F32)
    q_lat = ql[:, :KV_LORA].astype(BF16)
    q_pe = ql[:, KV_LORA:KV_LORA + QK_ROPE].astype(BF16)

    wait_fetch(b, slot)
    ckv_buf[slot, past:past + T_PAD, :] = cn_ref[0]
    kpe_buf[slot, :, past:] = kn_ref[0]

    chunk = min(KEY_CHUNK, past)
    bounds = [(c * chunk, chunk) for c in range(past // chunk)] + [(past, NEW_PAD)]
    qt = lax.broadcasted_iota(jnp.int32, (n_rows, NEW_PAD), 0) % T_PAD
    kt = lax.broadcasted_iota(jnp.int32, (n_rows, NEW_PAD), 1)
    for r0, n in bounds:
        kb_ref[r0:r0 + n, :] = ckv_buf[slot, r0:r0 + n, :].astype(BF16)
    s_ref[...] = (lax.dot_general(q_lat, kb_ref[...], _NT, preferred_element_type=F32)
                  + jnp.dot(q_pe, kpe_buf[slot].astype(BF16), preferred_element_type=F32))
    s_ref[:, past:] = jnp.where((kt <= qt) & (kt < n_new), s_ref[:, past:], NEG)
    m = jnp.max(s_ref[...], axis=-1, keepdims=True)
    l = jnp.zeros((n_rows, 1), F32)
    for r0, n in bounds:
        p = jnp.exp2(s_ref[:, r0:r0 + n] - m)
        l = l + jnp.sum(p, axis=-1, keepdims=True)
        p_ref[:, r0:r0 + n] = p.astype(BF16)
    o = jnp.dot(p_ref[...], kb_ref[...], preferred_element_type=F32)
    o_lat = (o / l).astype(BF16)
    pv = jnp.dot(o_lat, wuv_ref[...], preferred_element_type=F32)
    pv = jnp.where(row_h == col_h, pv, 0.0)
    out = pv[0:T_PAD]
    for hh in range(1, MLA_HEADS):
        out = out + pv[hh * T_PAD:(hh + 1) * T_PAD]
    o_ref[0] = out.astype(BF16)


def _paged_attention(page_table, q8, ckv_new8, kpe_new_t, wq, wuv, cache_ckv, cache_kpe_t, n_new):
    nb, n_pages = page_table.shape
    rows = n_pages * PAGE_SIZE + NEW_PAD
    grid_spec = pltpu.PrefetchScalarGridSpec(
        num_scalar_prefetch=1,
        grid=(nb,),
        in_specs=[
            pl.BlockSpec((1, T_PAD, D_HEADS), lambda b, pt: (b, 0, 0)),
            pl.BlockSpec((1, T_PAD, KV_LORA), lambda b, pt: (b, 0, 0)),
            pl.BlockSpec((1, QK_ROPE, NEW_PAD), lambda b, pt: (b, 0, 0)),
            pl.BlockSpec(wq.shape, lambda b, pt: (0, 0)),
            pl.BlockSpec(wuv.shape, lambda b, pt: (0, 0)),
            pl.BlockSpec(memory_space=pl.ANY),
            pl.BlockSpec(memory_space=pl.ANY),
        ],
        out_specs=pl.BlockSpec((1, T_PAD, D_HEADS), lambda b, pt: (b, 0, 0)),
        scratch_shapes=[
            pltpu.VMEM((2, rows, KV_LORA), F32),
            pltpu.VMEM((2, QK_ROPE, rows), F32),
            pltpu.VMEM((rows, KV_LORA), BF16),
            pltpu.VMEM((MLA_HEADS * T_PAD, rows), F32),
            pltpu.VMEM((MLA_HEADS * T_PAD, rows), BF16),
            pltpu.SemaphoreType.DMA((2, 2)),
        ],
    )
    return pl.pallas_call(
        functools.partial(_paged_kernel, n_pages=n_pages, n_new=n_new),
        grid_spec=grid_spec,
        out_shape=jax.ShapeDtypeStruct((nb, T_PAD, D_HEADS), BF16),
        compiler_params=_params("arbitrary"),
        name="paged_attention",
    )(page_table, q8, ckv_new8, kpe_new_t, wq, wuv, cache_ckv, cache_kpe_t)


def _group_norm_gate(o, gate, gain, bias):
    mu = jnp.mean(o, axis=-1, keepdims=True)
    d = o - mu
    var = jnp.mean(d * d, axis=-1, keepdims=True)
    return _silu(gate) * (d * lax.rsqrt(var + LN_EPS) * gain + bias)


def _ret_kernel(rq_ref, rk_ref, rv_ref, rg_ref, din_ref, qd_ref, kd_ref, gc_ref, gg_ref, gb_ref, s0_ref,
                ro_ref, so_ref, s_ref, *, chunks):
    i = pl.program_id(1)

    @pl.when(i == 0)
    def _():
        s_ref[...] = s0_ref[0]

    for hh in range(RET_HEADS):
        sl = slice(hh * RET_DK, (hh + 1) * RET_DK)
        din = din_ref[:, sl]
        qd = qd_ref[:, sl]
        kd = kd_ref[:, sl]
        gc = gc_ref[:, sl]
        gain = gg_ref[:, sl]
        bias = gb_ref[:, sl]
        s = s_ref[hh]
        for c in range(chunks):
            rows = slice(c * RET_CHUNK, (c + 1) * RET_CHUNK)
            qh = rq_ref[rows, sl]
            kh = rk_ref[rows, sl]
            vh = rv_ref[rows, sl]
            inner = lax.dot_general(qh.astype(BF16), kh.astype(BF16), _NT, preferred_element_type=F32) * din
            o = (jnp.dot(inner.astype(BF16), vh, preferred_element_type=F32)
                 + jnp.dot((qh * qd).astype(BF16), s.astype(BF16), preferred_element_type=F32))
            s = s * gc + lax.dot_general((kh * kd).astype(BF16), vh, _TN, preferred_element_type=F32)
            ro_ref[rows, sl] = _group_norm_gate(o, rg_ref[rows, sl], gain, bias).astype(BF16)
        s_ref[hh] = s

    @pl.when(i == pl.num_programs(1) - 1)
    def _():
        so_ref[0] = s_ref[...]


def _retention_prompt(rq, rk, rv, rg, dec, gn_gain, gn_bias, state0, batch, seq, chunks):
    rows = chunks * RET_CHUNK
    steps = seq // rows
    rspec = pl.BlockSpec((rows, D_RET), lambda b, i: (b * steps + i, 0))
    sspec = pl.BlockSpec((1, RET_HEADS, RET_DK, RET_DV), lambda b, i: (b, 0, 0, 0))
    consts = [dec["din"], dec["qdec"], dec["kdec"], dec["gc"], gn_gain, gn_bias]
    return pl.pallas_call(
        functools.partial(_ret_kernel, chunks=chunks),
        grid=(batch, steps),
        in_specs=[rspec] * 4 + [_const_spec(c.shape) for c in consts] + [sspec],
        out_specs=[rspec, sspec],
        out_shape=[jax.ShapeDtypeStruct((batch * seq, D_RET), BF16),
                   jax.ShapeDtypeStruct((batch, RET_HEADS, RET_DK, RET_DV), F32)],
        scratch_shapes=[pltpu.VMEM((RET_HEADS, RET_DK, RET_DV), F32)],
        compiler_params=_params("parallel", "arbitrary"),
        name="retention_prompt",
    )(rq, rk, rv, rg, *consts, state0)


RS_BATCH = 8


def _ret_sample_kernel(q_ref, k_ref, v_ref, rg_ref, kt_ref, qd_ref, kd_ref, din_ref, gc_ref, gg_ref, gb_ref,
                       s0_ref, ro_ref, so_ref, *, n_tok):
    def one_seq(bi, carry):
        q8 = q_ref[bi]
        k8 = k_ref[bi]
        v8 = v_ref[bi]
        g8 = rg_ref[bi]
        qs = (q8 * qd_ref[...]).astype(BF16)
        outs = []
        for hh in range(RET_HEADS):
            sl = slice(hh * RET_DK, (hh + 1) * RET_DK)
            o = jnp.dot(qs[:, sl], s0_ref[bi, hh].astype(BF16), preferred_element_type=F32)
            din = din_ref[hh]
            for m in range(n_tok):
                a = jnp.sum(q8[:, sl] * k8[m:m + 1, sl], axis=-1, keepdims=True) * din[:, m:m + 1]
                o = o + a * v8[m:m + 1, sl]
            outs.append(_group_norm_gate(o, g8[:, sl], gg_ref[:, sl], gb_ref[:, sl]))
        ro_ref[bi] = jnp.concatenate(outs, axis=1).astype(BF16)
        vbd = jnp.concatenate([v8 * kd_ref[hh] for hh in range(RET_HEADS)]
                              + [jnp.zeros((LANES - RET_HEADS * T_PAD, D_RET), F32)], axis=0)
        upd = jnp.dot(kt_ref[bi].astype(BF16), vbd.astype(BF16), preferred_element_type=F32)
        for hh in range(RET_HEADS):
            sl = slice(hh * RET_DV, (hh + 1) * RET_DV)
            so_ref[bi, hh] = s0_ref[bi, hh] * gc_ref[:, sl] + upd[:, sl]
        return carry

    lax.fori_loop(0, RS_BATCH, one_seq, 0)


def _retention_sample(q8, k8, v8, rg8, k_t, dec, gn_gain, gn_bias, state0, n_tok):
    nb = q8.shape[0]
    rspec = pl.BlockSpec((RS_BATCH, T_PAD, D_RET), lambda i: (i, 0, 0))
    sspec = pl.BlockSpec((RS_BATCH, RET_HEADS, RET_DK, RET_DV), lambda i: (i, 0, 0, 0))
    consts = [dec["qdec"], dec["kdec"], dec["din"], dec["gc"], gn_gain, gn_bias]
    return pl.pallas_call(
        functools.partial(_ret_sample_kernel, n_tok=n_tok),
        grid=(nb // RS_BATCH,),
        in_specs=[rspec] * 4 + [pl.BlockSpec((RS_BATCH, RET_DK, LANES), lambda i: (i, 0, 0))]
                 + [_const_spec(c.shape) for c in consts] + [sspec],
        out_specs=[rspec, sspec],
        out_shape=[jax.ShapeDtypeStruct((nb, T_PAD, D_RET), BF16),
                   jax.ShapeDtypeStruct((nb, RET_HEADS, RET_DK, RET_DV), F32)],
        compiler_params=_params("parallel"),
        name="retention_sample",
    )(q8, k8, v8, rg8, k_t, *consts, state0)


def _outproj_ln_kernel(a_ref, ro_ref, h_ref, wa_ref, wr_ref, g_ref, b_ref, o_ref):
    y = (jnp.dot(a_ref[...], wa_ref[...], preferred_element_type=F32)
         + jnp.dot(ro_ref[...], wr_ref[...], preferred_element_type=F32))
    o_ref[...] = _layer_norm(ALPHA * h_ref[...] + y, g_ref[...], b_ref[...])


def _outproj_ln(a, ro, h, wa, wr, g, b, tm):
    m = h.shape[0]

    def row(n):
        return pl.BlockSpec((tm, n), lambda i: (i, 0))

    return pl.pallas_call(
        _outproj_ln_kernel,
        grid=(m // tm,),
        in_specs=[row(D_HEADS), row(D_RET), row(D_MODEL), _const_spec(wa.shape), _const_spec(wr.shape),
                  _const_spec(g.shape), _const_spec(b.shape)],
        out_specs=row(D_MODEL),
        out_shape=jax.ShapeDtypeStruct((m, D_MODEL), F32),
        compiler_params=_params("parallel"),
        name="outproj_ln",
    )(a, ro, h, wa, wr, g, b)


HALO = 16


def _pool_tail(pooled_groups, x, pw_ref, pb_ref, ps_ref, g_ref, b_ref):
    ys = [jnp.dot(p.astype(BF16), pw_ref[gi], preferred_element_type=F32) for gi, p in enumerate(pooled_groups)]
    y = (jnp.concatenate(ys, axis=-1) + pb_ref[...]) * ps_ref[...]
    return _layer_norm(ALPHA * x + y, g_ref[...], b_ref[...])


def _pool_prompt_kernel(h_ref, halo_ref, pre_ref, pw_ref, pb_ref, ps_ref, g_ref, b_ref, o_ref, xs_ref,
                        *, tm, tiles, start):
    t = pl.program_id(0) % tiles
    x = h_ref[...]
    xs_ref[0:HALO, :] = jnp.where(t == 0, pre_ref[0], halo_ref[...])
    xs_ref[HALO:, :] = x
    pos = start + t * tm + lax.broadcasted_iota(jnp.int32, (tm, 1), 0)
    pooled = []
    for gi, wl in enumerate(POOL_WINDOWS):
        sl = slice(gi * POOL_GROUP, (gi + 1) * POOL_GROUP)
        acc = x[:, sl]
        for d in range(1, wl):
            acc = acc + xs_ref[HALO - d:HALO - d + tm, sl]
        cnt = jnp.minimum(pos + 1, wl).astype(F32)
        pooled.append(acc / cnt - x[:, sl])
    o_ref[...] = _pool_tail(pooled, x, pw_ref, pb_ref, ps_ref, g_ref, b_ref)


def _pool_prompt(h, prefix16, pw, pb, ps, g, b, batch, seq, start, tm):
    tiles = seq // tm
    per = tm // HALO
    consts = [pw, pb, ps, g, b]
    return pl.pallas_call(
        functools.partial(_pool_prompt_kernel, tm=tm, tiles=tiles, start=start),
        grid=(batch * tiles,),
        in_specs=[pl.BlockSpec((tm, D_MODEL), lambda i: (i, 0)),
                  pl.BlockSpec((HALO, D_MODEL), lambda i: (jnp.maximum(i * per - 1, 0), 0)),
                  pl.BlockSpec((1, HALO, D_MODEL), lambda i: (i // tiles, 0, 0))]
                 + [_const_spec(c.shape) for c in consts],
        out_specs=pl.BlockSpec((tm, D_MODEL), lambda i: (i, 0)),
        out_shape=jax.ShapeDtypeStruct((batch * seq, D_MODEL), F32),
        scratch_shapes=[pltpu.VMEM((HALO + tm, D_MODEL), F32)],
        compiler_params=_params("parallel"),
        name="pool_prompt",
    )(h, h, prefix16, *consts)


def _pool_sample_kernel(xs_ref, pw_ref, pb_ref, ps_ref, g_ref, b_ref, o_ref, *, n_tok, start):
    nb = xs_ref.shape[1]
    x = jnp.concatenate([xs_ref[HALO + t] for t in range(n_tok)], axis=0)
    pooled = []
    for gi, wl in enumerate(POOL_WINDOWS):
        sl = slice(gi * POOL_GROUP, (gi + 1) * POOL_GROUP)
        parts = []
        for t in range(n_tok):
            acc = xs_ref[HALO + t, :, sl]
            for d in range(1, wl):
                acc = acc + xs_ref[HALO + t - d, :, sl]
            parts.append(acc / float(min(start + t + 1, wl)))
        pooled.append(jnp.concatenate(parts, axis=0) - x[:, sl])
    y = _pool_tail(pooled, x, pw_ref, pb_ref, ps_ref, g_ref, b_ref)
    for t in range(n_tok):
        o_ref[t] = y[t * nb:(t + 1) * nb]


def _pool_sample(xs_t, pw, pb, ps, g, b, n_tok, start):
    nb = xs_t.shape[1]
    args = [xs_t, pw, pb, ps, g, b]
    return pl.pallas_call(
        functools.partial(_pool_sample_kernel, n_tok=n_tok, start=start),
        grid=(1,),
        in_specs=[_const_spec(a.shape) for a in args],
        out_specs=_const_spec((n_tok, nb, D_MODEL)),
        out_shape=jax.ShapeDtypeStruct((n_tok, nb, D_MODEL), F32),
        compiler_params=_params("arbitrary"),
        name="pool_sample",
    )(*args)


def _head_pad(w, width):
    r, nh, d = w.shape
    out = jnp.zeros((r, nh, HEAD_PAD), w.dtype).at[:, :, :d].set(w)
    return out.reshape(r, nh * HEAD_PAD)[:, :width]


def _mixer_weights(mix_w_in, q_norm, kv_norm, w_uq, w_uk, w_uv, mix_w_out):
    offs = np.concatenate([[0], np.cumsum(SPLIT_SIZES)])
    wq, wckv, wkpe, wrq, wrk, wrv, wrg = [mix_w_in[:, offs[i]:offs[i + 1]] for i in range(7)]
    half = QK_ROPE // 2
    z_lo = jnp.zeros((D_MODEL, QK_NOPE), F32)
    z_hi = jnp.zeros((D_MODEL, HEAD_PAD - QK_NOPE - QK_ROPE), F32)
    kpe_blk = jnp.concatenate([z_lo, wkpe, z_hi], axis=1)
    kpe_swp = jnp.concatenate([z_lo, -wkpe[:, half:], wkpe[:, :half], z_hi], axis=1)
    w_main = jnp.concatenate([wq, wckv, kpe_blk, kpe_swp, wrq, wrk, wrv, wrg], axis=1).astype(BF16)
    pe = w_uq[:, :, QK_NOPE:]
    uq1 = _head_pad(w_uq, D_HEADS)
    uq2 = _head_pad(jnp.concatenate([jnp.zeros_like(w_uq[:, :, :QK_NOPE]), -pe[:, :, half:], pe[:, :, :half]], axis=2),
                    D_HEADS)
    lane = np.arange(HEAD_PAD)
    sel = ((lane[:, None] == lane[None, :]) & (lane[:, None] >= QK_NOPE) & (lane[:, None] < QK_NOPE + QK_ROPE))
    e_mat = jnp.asarray(np.tile(sel.astype(np.float32), (1, MLA_HEADS)))
    wk = jnp.concatenate([_head_pad(w_uk, D_HEADS), e_mat], axis=0)
    wv = _head_pad(w_uv, D_HEADS)
    wa = jnp.zeros((MLA_HEADS, HEAD_PAD, D_MODEL), F32).at[:, :V_DIM, :].set(
        mix_w_out[:MLA_HEADS * V_DIM].reshape(MLA_HEADS, V_DIM, D_MODEL)).reshape(D_HEADS, D_MODEL)
    wr = mix_w_out[MLA_HEADS * V_DIM:]
    wuk_t = jnp.zeros((MLA_HEADS, HEAD_PAD, KV_LORA), F32).at[:, :QK_NOPE, :].set(
        jnp.transpose(w_uk, (1, 2, 0))).reshape(D_HEADS, KV_LORA)
    r = np.arange(D_HEADS) % HEAD_PAD
    e_pe = ((r[:, None] - QK_NOPE) == np.arange(LANES)[None, :]) & (r[:, None] >= QK_NOPE) & (r[:, None] < QK_NOPE + QK_ROPE)
    wq_abs = jnp.concatenate([wuk_t, jnp.asarray(e_pe.astype(np.float32))], axis=1)
    return {
        "w_main": w_main, "q_norm": q_norm[None, :], "kv_norm": kv_norm[None, :],
        "uq1": uq1.astype(BF16), "uq2": uq2.astype(BF16), "wk": wk.astype(BF16), "wv": wv.astype(BF16),
        "wv_t": wv.T.astype(BF16), "wa": wa.astype(BF16), "wr": wr.astype(BF16),
        "wq_abs": wq_abs.astype(BF16),
    }


def _rope_tables(pos):
    def angles(r):
        inv = 1.0 / (ROPE_BASE ** (jnp.arange(0, r, 2, dtype=F32) / r))
        return pos.astype(F32)[:, None] * inv[None, :]

    n = pos.shape[0]
    a = angles(QK_ROPE)
    c, s = jnp.cos(a), jnp.sin(a)
    hi = HEAD_PAD - QK_NOPE - QK_ROPE
    cq = jnp.concatenate([jnp.ones((n, QK_NOPE), F32), c, c, jnp.ones((n, hi), F32)], axis=1)
    sq = jnp.concatenate([jnp.zeros((n, QK_NOPE), F32), s, s, jnp.zeros((n, hi), F32)], axis=1)
    a = angles(RET_DK)
    c, s = jnp.cos(a), jnp.sin(a)
    return cq, sq, jnp.concatenate([c, c], axis=1), jnp.concatenate([-s, s], axis=1)


def _ret_log_decay():
    return jnp.log(1.0 - 2.0 ** (-5.0 - jnp.arange(RET_HEADS, dtype=F32)))


def _ret_decay_tables(chunk):
    log_g = _ret_log_decay()
    idx = jnp.arange(chunk, dtype=F32)
    diff = idx[:, None] - idx[None, :]
    d_in = jnp.where(diff >= 0, jnp.exp(jnp.maximum(diff, 0.0)[None] * log_g[:, None, None]), 0.0)
    q_dec = jnp.exp((idx + 1.0)[None, :] * log_g[:, None])
    k_dec = jnp.exp((chunk - 1.0 - idx)[None, :] * log_g[:, None])
    g_c = jnp.exp(chunk * log_g)
    lanes = (chunk, RET_HEADS * RET_DK)
    return {
        "din": jnp.transpose(d_in, (1, 0, 2)).reshape(chunk, RET_HEADS * chunk),
        "qdec": jnp.broadcast_to(q_dec.T[:, :, None], (chunk, RET_HEADS, RET_DK)).reshape(lanes),
        "kdec": jnp.broadcast_to(k_dec.T[:, :, None], (chunk, RET_HEADS, RET_DK)).reshape(lanes),
        "gc": jnp.broadcast_to(g_c[:, None], (RET_HEADS, RET_DV)).reshape(1, RET_HEADS * RET_DV),
    }


def _ret_sample_tables(n_tok):
    log_g = _ret_log_decay()
    idx = jnp.arange(n_tok, dtype=F32)
    diff = idx[:, None] - idx[None, :]
    d_in = jnp.where(diff >= 0, jnp.exp(jnp.maximum(diff, 0.0)[None] * log_g[:, None, None]), 0.0)
    q_dec = jnp.exp((idx + 1.0)[None, :] * log_g[:, None])
    k_dec = jnp.exp((n_tok - 1.0 - idx)[None, :] * log_g[:, None])
    g_c = jnp.exp(n_tok * log_g)
    pad_t = T_PAD - n_tok
    own_lanes = (np.arange(D_RET) // RET_DV)[None, None, :] == np.arange(RET_HEADS)[:, None, None]
    return {
        "qdec": jnp.pad(jnp.broadcast_to(q_dec.T[:, :, None], (n_tok, RET_HEADS, RET_DK)).reshape(n_tok, D_RET),
                        ((0, pad_t), (0, 0))),
        "kdec": jnp.pad(k_dec[:, :, None] * jnp.asarray(own_lanes, F32), ((0, 0), (0, pad_t), (0, 0))),
        "din": jnp.pad(d_in, ((0, 0), (0, pad_t), (0, LANES - n_tok))),
        "gc": jnp.broadcast_to(g_c[:, None], (RET_HEADS, RET_DV)).reshape(1, D_RET),
    }


def _trunk(x, start, ret_state0, pool_prefix, mla_cache, w, mw):
    batch, seq, _ = x.shape
    m = batch * seq
    prompt = mla_cache is None
    tm = min(512, m)
    h = x.reshape(m, D_MODEL)

    def ffn(h, layer, half):
        return _ffn_ln(h, w["wg"], w["wu"], w["wd"], layer, half,
                       w["ln_gain"][layer, 2 * half][None, :], w["ln_bias"][layer, 2 * half][None, :],
                       min(FFN_ROWS, m))

    h = ffn(h, 0, 0)
    if prompt:
        pos = start + jnp.arange(seq, dtype=jnp.int32)
    else:
        pos = start + (jnp.arange(m, dtype=jnp.int32) % seq)
    flash_blk = min(1024, seq) if prompt else tm
    q, k, vt, ckv, kpe, rq, rk, rv, rg = _mixer_prep(h, mw, _rope_tables(pos), tm, flash_blk)
    kpe = kpe[:, QK_NOPE:QK_NOPE + QK_ROPE]
    gn_gain = w["ret_gn_gain"][0][None, :]
    gn_bias = w["ret_gn_bias"][0][None, :]
    if prompt:
        a = _flash_attention(q, k, vt, batch, seq, flash_blk)
        chunk = RET_CHUNK if seq % RET_CHUNK == 0 else seq
        assert chunk == RET_CHUNK
        ro, ret_state = _retention_prompt(rq, rk, rv, rg, _ret_decay_tables(chunk), gn_gain, gn_bias,
                                          ret_state0, batch, seq, 8)
    else:
        assert seq <= T_PAD and seq % RET_CHUNK != 0
        cache_ckv, cache_kpe, page_table = mla_cache
        pad_t = ((0, 0), (0, T_PAD - seq), (0, 0))
        q8 = jnp.pad(q.reshape(batch, seq, D_HEADS), pad_t)
        cn8 = jnp.pad(ckv.reshape(batch, seq, KV_LORA), pad_t)
        kn_t = jnp.pad(jnp.swapaxes(kpe.reshape(batch, seq, QK_ROPE), 1, 2), ((0, 0), (0, 0), (0, NEW_PAD - seq)))
        a8 = _paged_attention(page_table, q8, cn8, kn_t, mw["wq_abs"], mw["wv"], cache_ckv,
                              jnp.swapaxes(cache_kpe, 1, 2), seq)
        a = a8[:, :seq].reshape(m, D_HEADS)

        def rows8(t):
            return jnp.pad(t.reshape(batch, seq, D_RET), pad_t)

        k8 = rows8(rk)
        k_t = jnp.transpose(k8.reshape(batch, T_PAD, RET_HEADS, RET_DK), (0, 3, 2, 1)).reshape(
            batch, RET_DK, RET_HEADS * T_PAD)
        k_t = jnp.pad(k_t, ((0, 0), (0, 0), (0, LANES - RET_HEADS * T_PAD)))
        ro8, ret_state = _retention_sample(rows8(rq), k8, rows8(rv.astype(F32)), rows8(rg), k_t,
                                           _ret_sample_tables(seq), gn_gain, gn_bias, ret_state0, seq)
        ro = ro8[:, :seq].reshape(m, D_RET)
    h = _outproj_ln(a, ro, h, mw["wa"], mw["wr"], w["ln_gain"][0, 1][None, :], w["ln_bias"][0, 1][None, :], tm)
    h = ffn(h, 0, 1)

    h = ffn(h, 1, 0)
    xp_tail = jnp.concatenate([pool_prefix, h.reshape(batch, seq, D_MODEL)], axis=1)[:, -POOL_PREFIX:]
    prefix16 = jnp.pad(pool_prefix, ((0, 0), (HALO - POOL_PREFIX, 0), (0, 0)))
    pool_args = (w["pool_w"], w["pool_b"][0][None, :], w["pool_scale"][0][None, :],
                 w["ln_gain"][1, 1][None, :], w["ln_bias"][1, 1][None, :])
    if prompt:
        h = _pool_prompt(h, prefix16, *pool_args, batch, seq, start, tm)
    else:
        xs_t = jnp.transpose(jnp.concatenate([prefix16, h.reshape(batch, seq, D_MODEL)], axis=1), (1, 0, 2))
        h = jnp.transpose(_pool_sample(xs_t, *pool_args, seq, start), (1, 0, 2)).reshape(m, D_MODEL)
    h = ffn(h, 1, 1)
    return (h.reshape(batch, seq, D_MODEL), ckv.reshape(1, batch, seq, KV_LORA),
            kpe.reshape(1, batch, seq, QK_ROPE), ret_state[None], xp_tail[None])


def kernel(x_prompt, x_sample, cache_mla_ckv, cache_mla_kpe, state_ret, state_pool, page_table, ffn_w_gate, ffn_w_up, ffn_w_down, ln_gain, ln_bias, mix_w_in, mla_q_norm, mla_kv_norm, mla_w_uq, mla_w_uk, mla_w_uv, ret_gn_gain, ret_gn_bias, mix_w_out, pool_w, pool_b, pool_scale):
    assert DEPTH == 2 and mix_w_in.shape[0] == 1 and pool_w.shape[0] == 1
    w = {
        "wg": ffn_w_gate.astype(BF16), "wu": ffn_w_up.astype(BF16), "wd": ffn_w_down.astype(BF16),
        "ln_gain": ln_gain, "ln_bias": ln_bias, "ret_gn_gain": ret_gn_gain, "ret_gn_bias": ret_gn_bias,
        "pool_w": pool_w[0].astype(BF16), "pool_b": pool_b, "pool_scale": pool_scale,
    }
    mw = _mixer_weights(mix_w_in[0], mla_q_norm[0], mla_kv_norm[0], mla_w_uq[0], mla_w_uk[0], mla_w_uv[0],
                        mix_w_out[0])
    bp = x_prompt.shape[0]
    zero_ret = jnp.zeros((bp, RET_HEADS, RET_DK, RET_DV), F32)
    zero_pool = jnp.zeros((bp, POOL_PREFIX, D_MODEL), x_prompt.dtype)
    y_p, ckv_p, kpe_p, ret_p, pool_p = _trunk(x_prompt, 0, zero_ret, zero_pool, None, w, mw)
    y_s, ckv_s, kpe_s, ret_s, pool_s = _trunk(x_sample, PAST_LEN, state_ret[0], state_pool[0],
                                              (cache_mla_ckv[0], cache_mla_kpe[0], page_table), w, mw)
    return (y_p, y_s, ckv_p, kpe_p, ckv_s, kpe_s, ret_p, ret_s, pool_p, pool_s)
```

```python
import functools

import numpy as np
import jax
import jax.numpy as jnp
from jax import lax
from jax.experimental import pallas as pl
from jax.experimental.pallas import tpu as pltpu

F32 = jnp.float32
BF16 = jnp.bfloat16

D_MODEL = 1024
DEPTH = 2
PAST_LEN = 8192
PAGE_SIZE = 128
ALPHA = (2 * DEPTH) ** 0.25
D_FF = 2816
MLA_HEADS = 8
Q_LORA = 512
KV_LORA = 256
QK_NOPE = 64
QK_ROPE = 32
V_DIM = 64
RET_HEADS = 4
RET_DK = 128
RET_DV = 128
RET_CHUNK = 128
POOL_WINDOWS = (2, 4, 8, 16)
POOL_GROUPS = 4
POOL_GROUP = D_MODEL // POOL_GROUPS
POOL_PREFIX = 15
ROPE_BASE = 10000.0
LN_EPS = 1e-5
RMS_EPS = 1e-6
SPLIT_SIZES = (Q_LORA, KV_LORA, QK_ROPE, RET_HEADS * RET_DK, RET_HEADS * RET_DK,
               RET_HEADS * RET_DV, RET_HEADS * RET_DV)
ATT_SCALE = (QK_NOPE + QK_ROPE) ** -0.5 * 1.4426950408889634

LANES = 128
HEAD_PAD = LANES
D_HEADS = MLA_HEADS * HEAD_PAD
D_ATT = MLA_HEADS * V_DIM
D_RET = RET_HEADS * RET_DV
NEG = -1e30
VMEM_LIMIT = 56 * 1024 * 1024

_NT = (((1,), (1,)), ((), ()))
_TN = (((0,), (0,)), ((), ()))


def _params(*sem):
    return pltpu.CompilerParams(dimension_semantics=sem, vmem_limit_bytes=VMEM_LIMIT)


def _const_spec(shape):
    nd = len(shape)
    return pl.BlockSpec(shape, lambda *_: (0,) * nd, pipeline_mode=pl.Buffered(1))


def _layer_norm(y, g, b):
    mu = jnp.mean(y, axis=-1, keepdims=True)
    d = y - mu
    var = jnp.mean(d * d, axis=-1, keepdims=True)
    return d * lax.rsqrt(var + LN_EPS) * g + b


def _silu(x):
    return x * jax.nn.sigmoid(x)


FFN_CHUNK = 256
FFN_ROWS = 512


def _ffn_ln_kernel(xp_ref, xs_ref, wg_ref, wu_ref, wd_ref, g_ref, b_ref, op_ref, os_ref, *, n_prompt):
    is_prompt = pl.program_id(0) < n_prompt

    def half_step(x_ref, o_ref):
        x = x_ref[...]
        xb = x.astype(BF16)
        acc = None
        for c in range(D_FF // FFN_CHUNK):
            sl = slice(c * FFN_CHUNK, (c + 1) * FFN_CHUNK)
            g = jnp.dot(xb, wg_ref[:, sl], preferred_element_type=F32)
            u = jnp.dot(xb, wu_ref[:, sl], preferred_element_type=F32)
            a = (_silu(g) * u).astype(BF16)
            d = jnp.dot(a, wd_ref[sl, :], preferred_element_type=F32)
            acc = d if acc is None else acc + d
        o_ref[...] = _layer_norm(ALPHA * x + 0.5 * acc, g_ref[...], b_ref[...])

    pl.when(is_prompt)(functools.partial(half_step, xp_ref, op_ref))
    pl.when(jnp.logical_not(is_prompt))(functools.partial(half_step, xs_ref, os_ref))


def _ffn_ln(xp, xs, wg, wu, wd, layer, half, g, b):
    tm = FFN_ROWS
    assert xp.shape[0] % tm == 0 and xs.shape[0] % tm == 0
    n_p, n_s = xp.shape[0] // tm, xs.shape[0] // tm
    pspec = pl.BlockSpec((tm, D_MODEL), lambda i: (jnp.minimum(i, n_p - 1), 0))
    sspec = pl.BlockSpec((tm, D_MODEL), lambda i: (jnp.maximum(i - n_p, 0), 0))

    def wspec(w):
        return pl.BlockSpec((None, None) + w.shape[2:], lambda i: (layer, half, 0, 0),
                            pipeline_mode=pl.Buffered(1))

    return pl.pallas_call(
        functools.partial(_ffn_ln_kernel, n_prompt=n_p),
        grid=(n_p + n_s,),
        in_specs=[pspec, sspec, wspec(wg), wspec(wu), wspec(wd), _const_spec(g.shape), _const_spec(b.shape)],
        out_specs=[pspec, sspec],
        out_shape=[jax.ShapeDtypeStruct(xp.shape, F32), jax.ShapeDtypeStruct(xs.shape, F32)],
        compiler_params=_params("arbitrary"),
        name="ffn_ln",
    )(xp, xs, wg, wu, wd, g, b)


_C_QL, _C_CKV, _C_KPE, _C_KPS, _C_RQ, _C_RK, _C_RV, _C_RG, _C_END = (
    0, 512, 768, 896, 1024, 1536, 2048, 2560, 3072)


def _prep_kernel(h_ref, w_ref, qn_ref, kvn_ref, uq1_ref, uq2_ref, wk_ref, wvt_ref,
                 cq_ref, sq_ref, cr_ref, sr_ref,
                 q_ref, k_ref, vt_ref, ckv_ref, kpe_ref, rq_ref, rk_ref, rv_ref, rg_ref):
    xb = h_ref[...].astype(BF16)

    def proj(a, b):
        return jnp.dot(xb, w_ref[:, a:b], preferred_element_type=F32)

    cq = cq_ref[...]
    sq = sq_ref[...]
    ql = proj(_C_QL, _C_CKV)
    qn = (ql * lax.rsqrt(jnp.mean(ql * ql, axis=-1, keepdims=True) + RMS_EPS) * qn_ref[...]).astype(BF16)
    for hh in range(MLA_HEADS):
        sl = slice(hh * HEAD_PAD, (hh + 1) * HEAD_PAD)
        a = jnp.dot(qn, uq1_ref[:, sl], preferred_element_type=F32)
        b = jnp.dot(qn, uq2_ref[:, sl], preferred_element_type=F32)
        q_ref[:, sl] = ((a * cq + b * sq) * ATT_SCALE).astype(BF16)
    c = proj(_C_CKV, _C_KPE)
    ckv = c * lax.rsqrt(jnp.mean(c * c, axis=-1, keepdims=True) + RMS_EPS) * kvn_ref[...]
    ckv_ref[...] = ckv
    cb = ckv.astype(BF16)
    kp = proj(_C_KPE, _C_RQ)
    kpe = kp[:, :HEAD_PAD] * cq + kp[:, HEAD_PAD:] * sq
    kpe_ref[...] = kpe
    kb = kpe.astype(BF16)
    k = (jnp.dot(cb, wk_ref[0:KV_LORA, :], preferred_element_type=F32)
         + jnp.dot(kb, wk_ref[KV_LORA:KV_LORA + HEAD_PAD, :], preferred_element_type=F32))
    k_ref[...] = k.astype(BF16)
    vt = lax.dot_general(wvt_ref[...], cb, _NT, preferred_element_type=F32)
    head_row = lax.broadcasted_iota(jnp.int32, vt.shape, 0) % HEAD_PAD
    vt_ref[0] = jnp.where(head_row == V_DIM, 1.0, vt).astype(BF16)
    cr = cr_ref[...]
    sr = sr_ref[...]
    rq = proj(_C_RQ, _C_RK)
    rk = proj(_C_RK, _C_RV)
    for hh in range(RET_HEADS):
        sl = slice(hh * RET_DK, (hh + 1) * RET_DK)
        xq = rq[:, sl]
        xk = rk[:, sl]
        rq_ref[:, sl] = xq * cr + pltpu.roll(xq, RET_DK // 2, 1) * sr
        rk_ref[:, sl] = (xk * cr + pltpu.roll(xk, RET_DK // 2, 1) * sr) * (RET_DK ** -0.5)
    rv_ref[...] = proj(_C_RV, _C_RG).astype(BF16)
    rg_ref[...] = proj(_C_RG, _C_END)


def _mixer_prep(h, mw, tabs, tm, vt_blk):
    m = h.shape[0]
    cq, sq, cr, sr = tabs
    tab_blocks = cq.shape[0] // tm
    per = vt_blk // tm

    def row(n):
        return pl.BlockSpec((tm, n), lambda i: (i, 0))

    tab = pl.BlockSpec((tm, LANES), lambda i: (i % tab_blocks, 0))
    consts = [mw["w_main"], mw["q_norm"], mw["kv_norm"], mw["uq1"], mw["uq2"], mw["wk"], mw["wv_t"]]
    out_shape = [
        jax.ShapeDtypeStruct((m, D_HEADS), BF16),
        jax.ShapeDtypeStruct((m, D_HEADS), BF16),
        jax.ShapeDtypeStruct((m // vt_blk, D_HEADS, vt_blk), BF16),
        jax.ShapeDtypeStruct((m, KV_LORA), F32),
        jax.ShapeDtypeStruct((m, LANES), F32),
        jax.ShapeDtypeStruct((m, D_RET), F32),
        jax.ShapeDtypeStruct((m, D_RET), F32),
        jax.ShapeDtypeStruct((m, D_RET), BF16),
        jax.ShapeDtypeStruct((m, D_RET), F32),
    ]
    return pl.pallas_call(
        _prep_kernel,
        grid=(m // tm,),
        in_specs=[row(D_MODEL)] + [_const_spec(c.shape) for c in consts] + [tab] * 4,
        out_specs=[pl.BlockSpec((1, D_HEADS, tm), lambda i: (i // per, 0, i % per)) if len(s.shape) == 3
                   else row(s.shape[1]) for s in out_shape],
        out_shape=out_shape,
        compiler_params=_params("parallel"),
        name="mixer_prep",
    )(h, *consts, cq, sq, cr, sr)


FLASH_HEADS = 2


def _flash_kernel(q_ref, k_ref, vt_ref, o_ref, *, blk):
    i = pl.program_id(2)
    lanes = [slice(a * HEAD_PAD, (a + 1) * HEAD_PAD) for a in range(FLASH_HEADS)]
    qs = [q_ref[:, sl] for sl in lanes]

    def scores_t(j):
        rows = pl.ds(pl.multiple_of(j * blk, blk), blk)
        return tuple(lax.dot_general(k_ref[rows, sl], q, _NT, preferred_element_type=F32)
                     for q, sl in zip(qs, lanes))

    def update(st, j, m, acc, sl):
        m_new = jnp.maximum(m, jnp.max(st, axis=0, keepdims=True))
        p = jnp.exp2(st - m_new).astype(BF16)
        acc = acc * jnp.exp2(m - m_new) + jnp.dot(vt_ref[j, sl, :], p, preferred_element_type=F32)
        return m_new, acc

    def body(j, carry):
        ms, accs = carry
        new = [update(st, j, m, acc, sl) for st, m, acc, sl in zip(scores_t(j), ms, accs, lanes)]
        return tuple(n[0] for n in new), tuple(n[1] for n in new)

    m0 = tuple(jnp.full((1, blk), NEG, F32) for _ in lanes)
    acc0 = tuple(jnp.zeros((HEAD_PAD, blk), F32) for _ in lanes)
    ms, accs = lax.fori_loop(0, i, body, (m0, acc0))
    key = lax.broadcasted_iota(jnp.int32, (blk, blk), 0)
    qry = lax.broadcasted_iota(jnp.int32, (blk, blk), 1)
    outs = []
    for st, m, acc, sl in zip(scores_t(i), ms, accs, lanes):
        m, acc = update(jnp.where(key <= qry, st, NEG), i, m, acc, sl)
        outs.append((acc / acc[V_DIM:V_DIM + 1, :])[:V_DIM])
    o_ref[...] = jnp.concatenate(outs, axis=0).T.astype(BF16)


def _flash_attention(q, k, vt, batch, seq, blk):
    nq = seq // blk
    width = FLASH_HEADS * HEAD_PAD
    qspec = pl.BlockSpec((blk, width), lambda b, h, i: (b * nq + i, h))
    kspec = pl.BlockSpec((seq, width), lambda b, h, i: (b, h))
    vspec = pl.BlockSpec((nq, width, blk), lambda b, h, i: (b, h, 0))
    return pl.pallas_call(
        functools.partial(_flash_kernel, blk=blk),
        grid=(batch, MLA_HEADS // FLASH_HEADS, nq),
        in_specs=[qspec, kspec, vspec],
        out_specs=pl.BlockSpec((blk, FLASH_HEADS * V_DIM), lambda b, h, i: (b * nq + i, h)),
        out_shape=jax.ShapeDtypeStruct((batch * seq, D_ATT), BF16),
        compiler_params=_params("parallel", "parallel", "arbitrary"),
        name="flash_attention",
    )(q, k, vt)


T_PAD = 8
NEW_PAD = PAGE_SIZE
KEY_CHUNK = 1024


def _paged_kernel(pt_ref, q_ref, cn_ref, kn_ref, wq_ref, wuv_ref, ckv_hbm, kpe_hbm, o_ref,
                  ckv_buf, kpe_buf, kb_ref, s_ref, p_ref, sem, *, n_pages, n_new):
    b = pl.program_id(0)
    nb = pl.num_programs(0)
    past = n_pages * PAGE_SIZE
    slot = b % 2

    def page_copies(bb, sl, p):
        page = pt_ref[bb, p]
        rows = pl.ds(p * PAGE_SIZE, PAGE_SIZE)
        return (pltpu.make_async_copy(ckv_hbm.at[page], ckv_buf.at[sl, rows, :], sem.at[sl, 0]),
                pltpu.make_async_copy(kpe_hbm.at[page], kpe_buf.at[sl, :, rows], sem.at[sl, 1]))

    def start_fetch(bb, sl):
        for p in range(n_pages):
            for cp in page_copies(bb, sl, p):
                cp.start()

    def wait_fetch(bb, sl):
        for p in range(n_pages):
            for cp in page_copies(bb, sl, p):
                cp.wait()

    @pl.when(b == 0)
    def _():
        ckv_buf[:, past:, :] = jnp.zeros((2, NEW_PAD, KV_LORA), F32)
        start_fetch(0, 0)

    @pl.when(b + 1 < nb)
    def _():
        start_fetch(b + 1, 1 - slot)

    q8 = q_ref[0]
    qrep = jnp.concatenate([q8] * MLA_HEADS, axis=0)
    n_rows = MLA_HEADS * T_PAD
    row_h = lax.broadcasted_iota(jnp.int32, (n_rows, D_HEADS), 0) // T_PAD
    col_h = lax.broadcasted_iota(jnp.int32, (n_rows, D_HEADS), 1) // HEAD_PAD
    qm = jnp.where(row_h == col_h, qrep, jnp.zeros_like(qrep))
    ql = jnp.dot(qm, wq_ref[...], preferred_element_type=F32)
    q_lat = ql[:, :KV_LORA].astype(BF16)
    q_pe = ql[:, KV_LORA:KV_LORA + QK_ROPE].astype(BF16)

    wait_fetch(b, slot)
    ckv_buf[slot, past:past + T_PAD, :] = cn_ref[0]
    kpe_buf[slot, :, past:] = kn_ref[0]

    chunk = min(KEY_CHUNK, past)
    bounds = [(c * chunk, chunk) for c in range(past // chunk)] + [(past, NEW_PAD)]
    qt = lax.broadcasted_iota(jnp.int32, (n_rows, NEW_PAD), 0) % T_PAD
    kt = lax.broadcasted_iota(jnp.int32, (n_rows, NEW_PAD), 1)
    for r0, n in bounds:
        kb_ref[r0:r0 + n, :] = ckv_buf[slot, r0:r0 + n, :].astype(BF16)
    s_ref[...] = (lax.dot_general(q_lat, kb_ref[...], _NT, preferred_element_type=F32)
                  + jnp.dot(q_pe, kpe_buf[slot].astype(BF16), preferred_element_type=F32))
    s_ref[:, past:] = jnp.where((kt <= qt) & (kt < n_new), s_ref[:, past:], NEG)
    m = jnp.max(s_ref[...], axis=-1, keepdims=True)
    l = jnp.zeros((n_rows, 1), F32)
    for r0, n in bounds:
        p = jnp.exp2(s_ref[:, r0:r0 + n] - m)
        l = l + jnp.sum(p, axis=-1, keepdims=True)
        p_ref[:, r0:r0 + n] = p.astype(BF16)
    o = jnp.dot(p_ref[...], kb_ref[...], preferred_element_type=F32)
    o_lat = (o / l).astype(BF16)
    pv = jnp.dot(o_lat, wuv_ref[...], preferred_element_type=F32)
    out_row_h = lax.broadcasted_iota(jnp.int32, (n_rows, D_ATT), 0) // T_PAD
    out_col_h = lax.broadcasted_iota(jnp.int32, (n_rows, D_ATT), 1) // V_DIM
    pv = jnp.where(out_row_h == out_col_h, pv, 0.0)
    out = pv[0:T_PAD]
    for hh in range(1, MLA_HEADS):
        out = out + pv[hh * T_PAD:(hh + 1) * T_PAD]
    o_ref[0] = out.astype(BF16)


def _paged_attention(page_table, q8, ckv_new8, kpe_new_t, wq, wuv, cache_ckv, cache_kpe_t, n_new):
    nb, n_pages = page_table.shape
    rows = n_pages * PAGE_SIZE + NEW_PAD
    grid_spec = pltpu.PrefetchScalarGridSpec(
        num_scalar_prefetch=1,
        grid=(nb,),
        in_specs=[
            pl.BlockSpec((1, T_PAD, D_HEADS), lambda b, pt: (b, 0, 0)),
            pl.BlockSpec((1, T_PAD, KV_LORA), lambda b, pt: (b, 0, 0)),
            pl.BlockSpec((1, QK_ROPE, NEW_PAD), lambda b, pt: (b, 0, 0)),
            pl.BlockSpec(wq.shape, lambda b, pt: (0, 0)),
            pl.BlockSpec(wuv.shape, lambda b, pt: (0, 0)),
            pl.BlockSpec(memory_space=pl.ANY),
            pl.BlockSpec(memory_space=pl.ANY),
        ],
        out_specs=pl.BlockSpec((1, T_PAD, D_ATT), lambda b, pt: (b, 0, 0)),
        scratch_shapes=[
            pltpu.VMEM((2, rows, KV_LORA), F32),
            pltpu.VMEM((2, QK_ROPE, rows), F32),
            pltpu.VMEM((rows, KV_LORA), BF16),
            pltpu.VMEM((MLA_HEADS * T_PAD, rows), F32),
            pltpu.VMEM((MLA_HEADS * T_PAD, rows), BF16),
            pltpu.SemaphoreType.DMA((2, 2)),
        ],
    )
    return pl.pallas_call(
        functools.partial(_paged_kernel, n_pages=n_pages, n_new=n_new),
        grid_spec=grid_spec,
        out_shape=jax.ShapeDtypeStruct((nb, T_PAD, D_ATT), BF16),
        compiler_params=_params("arbitrary"),
        name="paged_attention",
    )(page_table, q8, ckv_new8, kpe_new_t, wq, wuv, cache_ckv, cache_kpe_t)


def _group_norm_gate(o, gate, gain, bias):
    mu = jnp.mean(o, axis=-1, keepdims=True)
    d = o - mu
    var = jnp.mean(d * d, axis=-1, keepdims=True)
    return _silu(gate) * (d * lax.rsqrt(var + LN_EPS) * gain + bias)


def _ret_kernel(rq_ref, rk_ref, rv_ref, rg_ref, din_ref, qd_ref, kd_ref, gc_ref, gg_ref, gb_ref, s0_ref,
                ro_ref, so_ref, s_ref, *, chunks):
    i = pl.program_id(1)

    @pl.when(i == 0)
    def _():
        s_ref[...] = s0_ref[0]

    for hh in range(RET_HEADS):
        sl = slice(hh * RET_DK, (hh + 1) * RET_DK)
        din = din_ref[:, sl]
        qd = qd_ref[:, sl]
        kd = kd_ref[:, sl]
        gc = gc_ref[:, sl]
        gain = gg_ref[:, sl]
        bias = gb_ref[:, sl]
        s = s_ref[hh]
        for c in range(chunks):
            rows = slice(c * RET_CHUNK, (c + 1) * RET_CHUNK)
            qh = rq_ref[rows, sl]
            kh = rk_ref[rows, sl]
            vh = rv_ref[rows, sl]
            inner = lax.dot_general(qh.astype(BF16), kh.astype(BF16), _NT, preferred_element_type=F32) * din
            o = (jnp.dot(inner.astype(BF16), vh, preferred_element_type=F32)
                 + jnp.dot((qh * qd).astype(BF16), s.astype(BF16), preferred_element_type=F32))
            s = s * gc + lax.dot_general((kh * kd).astype(BF16), vh, _TN, preferred_element_type=F32)
            ro_ref[rows, sl] = _group_norm_gate(o, rg_ref[rows, sl], gain, bias).astype(BF16)
        s_ref[hh] = s

    @pl.when(i == pl.num_programs(1) - 1)
    def _():
        so_ref[0] = s_ref[...]


def _retention_prompt(rq, rk, rv, rg, dec, gn_gain, gn_bias, state0, batch, seq, chunks):
    rows = chunks * RET_CHUNK
    steps = seq // rows
    rspec = pl.BlockSpec((rows, D_RET), lambda b, i: (b * steps + i, 0))
    sspec = pl.BlockSpec((1, RET_HEADS, RET_DK, RET_DV), lambda b, i: (b, 0, 0, 0))
    consts = [dec["din"], dec["qdec"], dec["kdec"], dec["gc"], gn_gain, gn_bias]
    return pl.pallas_call(
        functools.partial(_ret_kernel, chunks=chunks),
        grid=(batch, steps),
        in_specs=[rspec] * 4 + [_const_spec(c.shape) for c in consts] + [sspec],
        out_specs=[rspec, sspec],
        out_shape=[jax.ShapeDtypeStruct((batch * seq, D_RET), BF16),
                   jax.ShapeDtypeStruct((batch, RET_HEADS, RET_DK, RET_DV), F32)],
        scratch_shapes=[pltpu.VMEM((RET_HEADS, RET_DK, RET_DV), F32)],
        compiler_params=_params("parallel", "arbitrary"),
        name="retention_prompt",
    )(rq, rk, rv, rg, *consts, state0)


RS_BATCH = 8


def _ret_sample_kernel(q_ref, k_ref, v_ref, rg_ref, kt_ref, qd_ref, kd_ref, din_ref, gc_ref, gg_ref, gb_ref,
                       s0_ref, ro_ref, so_ref, *, n_tok):
    def one_seq(bi, carry):
        q8 = q_ref[bi]
        k8 = k_ref[bi]
        v8 = v_ref[bi]
        g8 = rg_ref[bi]
        qs = (q8 * qd_ref[...]).astype(BF16)
        outs = []
        for hh in range(RET_HEADS):
            sl = slice(hh * RET_DK, (hh + 1) * RET_DK)
            o = jnp.dot(qs[:, sl], s0_ref[bi, hh].astype(BF16), preferred_element_type=F32)
            din = din_ref[hh]
            for m in range(n_tok):
                a = jnp.sum(q8[:, sl] * k8[m:m + 1, sl], axis=-1, keepdims=True) * din[:, m:m + 1]
                o = o + a * v8[m:m + 1, sl]
            outs.append(_group_norm_gate(o, g8[:, sl], gg_ref[:, sl], gb_ref[:, sl]))
        ro_ref[bi] = jnp.concatenate(outs, axis=1).astype(BF16)
        vbd = jnp.concatenate([v8 * kd_ref[hh] for hh in range(RET_HEADS)]
                              + [jnp.zeros((LANES - RET_HEADS * T_PAD, D_RET), F32)], axis=0)
        upd = jnp.dot(kt_ref[bi].astype(BF16), vbd.astype(BF16), preferred_element_type=F32)
        for hh in range(RET_HEADS):
            sl = slice(hh * RET_DV, (hh + 1) * RET_DV)
            so_ref[bi, hh] = s0_ref[bi, hh] * gc_ref[:, sl] + upd[:, sl]
        return carry

    lax.fori_loop(0, RS_BATCH, one_seq, 0)


def _retention_sample(q8, k8, v8, rg8, k_t, dec, gn_gain, gn_bias, state0, n_tok):
    nb = q8.shape[0]
    rspec = pl.BlockSpec((RS_BATCH, T_PAD, D_RET), lambda i: (i, 0, 0))
    sspec = pl.BlockSpec((RS_BATCH, RET_HEADS, RET_DK, RET_DV), lambda i: (i, 0, 0, 0))
    consts = [dec["qdec"], dec["kdec"], dec["din"], dec["gc"], gn_gain, gn_bias]
    return pl.pallas_call(
        functools.partial(_ret_sample_kernel, n_tok=n_tok),
        grid=(nb // RS_BATCH,),
        in_specs=[rspec] * 4 + [pl.BlockSpec((RS_BATCH, RET_DK, LANES), lambda i: (i, 0, 0))]
                 + [_const_spec(c.shape) for c in consts] + [sspec],
        out_specs=[rspec, sspec],
        out_shape=[jax.ShapeDtypeStruct((nb, T_PAD, D_RET), BF16),
                   jax.ShapeDtypeStruct((nb, RET_HEADS, RET_DK, RET_DV), F32)],
        compiler_params=_params("parallel"),
        name="retention_sample",
    )(q8, k8, v8, rg8, k_t, *consts, state0)


def _outproj_ln_kernel(a_ref, ro_ref, h_ref, wa_ref, wr_ref, g_ref, b_ref, o_ref):
    y = (jnp.dot(a_ref[...], wa_ref[...], preferred_element_type=F32)
         + jnp.dot(ro_ref[...], wr_ref[...], preferred_element_type=F32))
    o_ref[...] = _layer_norm(ALPHA * h_ref[...] + y, g_ref[...], b_ref[...])


def _outproj_ln(a, ro, h, wa, wr, g, b, tm):
    m = h.shape[0]

    def row(n):
        return pl.BlockSpec((tm, n), lambda i: (i, 0))

    return pl.pallas_call(
        _outproj_ln_kernel,
        grid=(m // tm,),
        in_specs=[row(D_ATT), row(D_RET), row(D_MODEL), _const_spec(wa.shape), _const_spec(wr.shape),
                  _const_spec(g.shape), _const_spec(b.shape)],
        out_specs=row(D_MODEL),
        out_shape=jax.ShapeDtypeStruct((m, D_MODEL), F32),
        compiler_params=_params("parallel"),
        name="outproj_ln",
    )(a, ro, h, wa, wr, g, b)


HALO = 16


def _pool_tail(pooled_groups, x, pw_ref, pb_ref, ps_ref, g_ref, b_ref):
    ys = [jnp.dot(p.astype(BF16), pw_ref[gi], preferred_element_type=F32) for gi, p in enumerate(pooled_groups)]
    y = (jnp.concatenate(ys, axis=-1) + pb_ref[...]) * ps_ref[...]
    return _layer_norm(ALPHA * x + y, g_ref[...], b_ref[...])


HALO_P = 32


def _pool_prompt_kernel(h_ref, halo_ref, pre_ref, pw_ref, pb_ref, ps_ref, g_ref, b_ref, o_ref,
                        xs_ref, a_ref, b2_ref, *, tm, tiles, start):
    t = pl.program_id(0) % tiles
    x = h_ref[...]
    xs_ref[0:HALO_P, :] = jnp.where(t == 0, pre_ref[0], halo_ref[...])
    xs_ref[HALO_P:, :] = x
    n = HALO_P + tm
    pos = start + t * tm + lax.broadcasted_iota(jnp.int32, (tm, 1), 0)
    pooled = []
    src = xs_ref
    for k, wl in enumerate(POOL_WINDOWS, start=1):
        lo = (k - 1) * POOL_GROUP
        r0, shift = 8 * k, wl // 2
        level = src[r0:n, lo:] + src[r0 - shift:n - shift, lo:]
        cnt = jnp.minimum(pos + 1, wl).astype(F32)
        pooled.append(level[HALO_P - r0:, :POOL_GROUP] / cnt - x[:, lo:lo + POOL_GROUP])
        if k < len(POOL_WINDOWS):
            dst = a_ref if k % 2 else b2_ref
            dst[r0:n, lo + POOL_GROUP:] = level[:, POOL_GROUP:]
            src = dst
    o_ref[...] = _pool_tail(pooled, x, pw_ref, pb_ref, ps_ref, g_ref, b_ref)


def _pool_prompt(h, prefix, pw, pb, ps, g, b, batch, seq, start, tm):
    assert all(w == 2 ** (i + 1) for i, w in enumerate(POOL_WINDOWS)) and 8 * len(POOL_WINDOWS) <= HALO_P
    tiles = seq // tm
    per = tm // HALO_P
    consts = [pw, pb, ps, g, b]
    buf = pltpu.VMEM((HALO_P + tm, D_MODEL), F32)
    return pl.pallas_call(
        functools.partial(_pool_prompt_kernel, tm=tm, tiles=tiles, start=start),
        grid=(batch * tiles,),
        in_specs=[pl.BlockSpec((tm, D_MODEL), lambda i: (i, 0)),
                  pl.BlockSpec((HALO_P, D_MODEL), lambda i: (jnp.maximum(i * per - 1, 0), 0)),
                  pl.BlockSpec((1, HALO_P, D_MODEL), lambda i: (i // tiles, 0, 0))]
                 + [_const_spec(c.shape) for c in consts],
        out_specs=pl.BlockSpec((tm, D_MODEL), lambda i: (i, 0)),
        out_shape=jax.ShapeDtypeStruct((batch * seq, D_MODEL), F32),
        scratch_shapes=[buf, buf, buf],
        compiler_params=_params("parallel"),
        name="pool_prompt",
    )(h, h, prefix, *consts)


def _pool_sample_kernel(xs_ref, pw_ref, pb_ref, ps_ref, g_ref, b_ref, o_ref, *, n_tok, start):
    nb = xs_ref.shape[1]
    x = jnp.concatenate([xs_ref[HALO + t] for t in range(n_tok)], axis=0)
    pooled = []
    for gi, wl in enumerate(POOL_WINDOWS):
        sl = slice(gi * POOL_GROUP, (gi + 1) * POOL_GROUP)
        parts = []
        for t in range(n_tok):
            acc = xs_ref[HALO + t, :, sl]
            for d in range(1, wl):
                acc = acc + xs_ref[HALO + t - d, :, sl]
            parts.append(acc / float(min(start + t + 1, wl)))
        pooled.append(jnp.concatenate(parts, axis=0) - x[:, sl])
    y = _pool_tail(pooled, x, pw_ref, pb_ref, ps_ref, g_ref, b_ref)
    for t in range(n_tok):
        o_ref[t] = y[t * nb:(t + 1) * nb]


def _pool_sample(xs_t, pw, pb, ps, g, b, n_tok, start):
    nb = xs_t.shape[1]
    args = [xs_t, pw, pb, ps, g, b]
    return pl.pallas_call(
        functools.partial(_pool_sample_kernel, n_tok=n_tok, start=start),
        grid=(1,),
        in_specs=[_const_spec(a.shape) for a in args],
        out_specs=_const_spec((n_tok, nb, D_MODEL)),
        out_shape=jax.ShapeDtypeStruct((n_tok, nb, D_MODEL), F32),
        compiler_params=_params("arbitrary"),
        name="pool_sample",
    )(*args)


def _head_pad(w, width):
    r, nh, d = w.shape
    out = jnp.zeros((r, nh, HEAD_PAD), w.dtype).at[:, :, :d].set(w)
    return out.reshape(r, nh * HEAD_PAD)[:, :width]


def _mixer_weights(mix_w_in, q_norm, kv_norm, w_uq, w_uk, w_uv, mix_w_out):
    offs = np.concatenate([[0], np.cumsum(SPLIT_SIZES)])
    wq, wckv, wkpe, wrq, wrk, wrv, wrg = [mix_w_in[:, offs[i]:offs[i + 1]] for i in range(7)]
    half = QK_ROPE // 2
    z_lo = jnp.zeros((D_MODEL, QK_NOPE), F32)
    z_hi = jnp.zeros((D_MODEL, HEAD_PAD - QK_NOPE - QK_ROPE), F32)
    kpe_blk = jnp.concatenate([z_lo, wkpe, z_hi], axis=1)
    kpe_swp = jnp.concatenate([z_lo, -wkpe[:, half:], wkpe[:, :half], z_hi], axis=1)
    w_main = jnp.concatenate([wq, wckv, kpe_blk, kpe_swp, wrq, wrk, wrv, wrg], axis=1).astype(BF16)
    pe = w_uq[:, :, QK_NOPE:]
    uq1 = _head_pad(w_uq, D_HEADS)
    uq2 = _head_pad(jnp.concatenate([jnp.zeros_like(w_uq[:, :, :QK_NOPE]), -pe[:, :, half:], pe[:, :, :half]], axis=2),
                    D_HEADS)
    lane = np.arange(HEAD_PAD)
    sel = ((lane[:, None] == lane[None, :]) & (lane[:, None] >= QK_NOPE) & (lane[:, None] < QK_NOPE + QK_ROPE))
    e_mat = jnp.asarray(np.tile(sel.astype(np.float32), (1, MLA_HEADS)))
    wk = jnp.concatenate([_head_pad(w_uk, D_HEADS), e_mat], axis=0)
    wv = _head_pad(w_uv, D_HEADS)
    wa = mix_w_out[:D_ATT]
    wr = mix_w_out[MLA_HEADS * V_DIM:]
    wuk_t = jnp.zeros((MLA_HEADS, HEAD_PAD, KV_LORA), F32).at[:, :QK_NOPE, :].set(
        jnp.transpose(w_uk, (1, 2, 0))).reshape(D_HEADS, KV_LORA)
    r = np.arange(D_HEADS) % HEAD_PAD
    e_pe = ((r[:, None] - QK_NOPE) == np.arange(LANES)[None, :]) & (r[:, None] >= QK_NOPE) & (r[:, None] < QK_NOPE + QK_ROPE)
    wq_abs = jnp.concatenate([wuk_t, jnp.asarray(e_pe.astype(np.float32))], axis=1)
    return {
        "w_main": w_main, "q_norm": q_norm[None, :], "kv_norm": kv_norm[None, :],
        "uq1": uq1.astype(BF16), "uq2": uq2.astype(BF16), "wk": wk.astype(BF16),
        "wv_c": w_uv.reshape(KV_LORA, D_ATT).astype(BF16),
        "wv_t": wv.T.astype(BF16), "wa": wa.astype(BF16), "wr": wr.astype(BF16),
        "wq_abs": wq_abs.astype(BF16),
    }


def _rope_tables(pos):
    def angles(r):
        inv = 1.0 / (ROPE_BASE ** (jnp.arange(0, r, 2, dtype=F32) / r))
        return pos.astype(F32)[:, None] * inv[None, :]

    n = pos.shape[0]
    a = angles(QK_ROPE)
    c, s = jnp.cos(a), jnp.sin(a)
    hi = HEAD_PAD - QK_NOPE - QK_ROPE
    cq = jnp.concatenate([jnp.ones((n, QK_NOPE), F32), c, c, jnp.ones((n, hi), F32)], axis=1)
    sq = jnp.concatenate([jnp.zeros((n, QK_NOPE), F32), s, s, jnp.zeros((n, hi), F32)], axis=1)
    a = angles(RET_DK)
    c, s = jnp.cos(a), jnp.sin(a)
    return cq, sq, jnp.concatenate([c, c], axis=1), jnp.concatenate([-s, s], axis=1)


def _ret_log_decay():
    return jnp.log(1.0 - 2.0 ** (-5.0 - jnp.arange(RET_HEADS, dtype=F32)))


def _ret_decay_tables(chunk):
    log_g = _ret_log_decay()
    idx = jnp.arange(chunk, dtype=F32)
    diff = idx[:, None] - idx[None, :]
    d_in = jnp.where(diff >= 0, jnp.exp(jnp.maximum(diff, 0.0)[None] * log_g[:, None, None]), 0.0)
    q_dec = jnp.exp((idx + 1.0)[None, :] * log_g[:, None])
    k_dec = jnp.exp((chunk - 1.0 - idx)[None, :] * log_g[:, None])
    g_c = jnp.exp(chunk * log_g)
    lanes = (chunk, RET_HEADS * RET_DK)
    return {
        "din": jnp.transpose(d_in, (1, 0, 2)).reshape(chunk, RET_HEADS * chunk),
        "qdec": jnp.broadcast_to(q_dec.T[:, :, None], (chunk, RET_HEADS, RET_DK)).reshape(lanes),
        "kdec": jnp.broadcast_to(k_dec.T[:, :, None], (chunk, RET_HEADS, RET_DK)).reshape(lanes),
        "gc": jnp.broadcast_to(g_c[:, None], (RET_HEADS, RET_DV)).reshape(1, RET_HEADS * RET_DV),
    }


def _ret_sample_tables(n_tok):
    log_g = _ret_log_decay()
    idx = jnp.arange(n_tok, dtype=F32)
    diff = idx[:, None] - idx[None, :]
    d_in = jnp.where(diff >= 0, jnp.exp(jnp.maximum(diff, 0.0)[None] * log_g[:, None, None]), 0.0)
    q_dec = jnp.exp((idx + 1.0)[None, :] * log_g[:, None])
    k_dec = jnp.exp((n_tok - 1.0 - idx)[None, :] * log_g[:, None])
    g_c = jnp.exp(n_tok * log_g)
    pad_t = T_PAD - n_tok
    own_lanes = (np.arange(D_RET) // RET_DV)[None, None, :] == np.arange(RET_HEADS)[:, None, None]
    return {
        "qdec": jnp.pad(jnp.broadcast_to(q_dec.T[:, :, None], (n_tok, RET_HEADS, RET_DK)).reshape(n_tok, D_RET),
                        ((0, pad_t), (0, 0))),
        "kdec": jnp.pad(k_dec[:, :, None] * jnp.asarray(own_lanes, F32), ((0, 0), (0, pad_t), (0, 0))),
        "din": jnp.pad(d_in, ((0, 0), (0, pad_t), (0, LANES - n_tok))),
        "gc": jnp.broadcast_to(g_c[:, None], (RET_HEADS, RET_DV)).reshape(1, D_RET),
    }


FFN_ORDER = ((0, 0), (0, 1), (1, 0), (1, 1))


def _trunk(x, start, ret_state0, pool_prefix, mla_cache, w, mw):
    batch, seq, _ = x.shape
    m = batch * seq
    prompt = mla_cache is None
    tm = min(512, m)
    h = x.reshape(m, D_MODEL)

    h = yield h
    if prompt:
        pos = start + jnp.arange(seq, dtype=jnp.int32)
    else:
        pos = start + (jnp.arange(m, dtype=jnp.int32) % seq)
    flash_blk = min(1024, seq) if prompt else tm
    q, k, vt, ckv, kpe, rq, rk, rv, rg = _mixer_prep(h, mw, _rope_tables(pos), tm, flash_blk)
    kpe = kpe[:, QK_NOPE:QK_NOPE + QK_ROPE]
    gn_gain = w["ret_gn_gain"][0][None, :]
    gn_bias = w["ret_gn_bias"][0][None, :]
    if prompt:
        a = _flash_attention(q, k, vt, batch, seq, flash_blk)
        chunk = RET_CHUNK if seq % RET_CHUNK == 0 else seq
        assert chunk == RET_CHUNK
        ro, ret_state = _retention_prompt(rq, rk, rv, rg, _ret_decay_tables(chunk), gn_gain, gn_bias,
                                          ret_state0, batch, seq, 8)
    else:
        assert seq <= T_PAD and seq % RET_CHUNK != 0
        cache_ckv, cache_kpe, page_table = mla_cache
        pad_t = ((0, 0), (0, T_PAD - seq), (0, 0))
        q8 = jnp.pad(q.reshape(batch, seq, D_HEADS), pad_t)
        cn8 = jnp.pad(ckv.reshape(batch, seq, KV_LORA), pad_t)
        kn_t = jnp.pad(jnp.swapaxes(kpe.reshape(batch, seq, QK_ROPE), 1, 2), ((0, 0), (0, 0), (0, NEW_PAD - seq)))
        a8 = _paged_attention(page_table, q8, cn8, kn_t, mw["wq_abs"], mw["wv_c"], cache_ckv,
                              jnp.swapaxes(cache_kpe, 1, 2), seq)
        a = a8[:, :seq].reshape(m, D_ATT)

        def rows8(t):
            return jnp.pad(t.reshape(batch, seq, D_RET), pad_t)

        k8 = rows8(rk)
        k_t = jnp.transpose(k8.reshape(batch, T_PAD, RET_HEADS, RET_DK), (0, 3, 2, 1)).reshape(
            batch, RET_DK, RET_HEADS * T_PAD)
        k_t = jnp.pad(k_t, ((0, 0), (0, 0), (0, LANES - RET_HEADS * T_PAD)))
        ro8, ret_state = _retention_sample(rows8(rq), k8, rows8(rv.astype(F32)), rows8(rg), k_t,
                                           _ret_sample_tables(seq), gn_gain, gn_bias, ret_state0, seq)
        ro = ro8[:, :seq].reshape(m, D_RET)
    h = _outproj_ln(a, ro, h, mw["wa"], mw["wr"], w["ln_gain"][0, 1][None, :], w["ln_bias"][0, 1][None, :], tm)
    h = yield h

    h = yield h
    xp_tail = jnp.concatenate([pool_prefix, h.reshape(batch, seq, D_MODEL)], axis=1)[:, -POOL_PREFIX:]
    halo = HALO_P if prompt else HALO
    prefix16 = jnp.pad(pool_prefix, ((0, 0), (halo - POOL_PREFIX, 0), (0, 0)))
    pool_args = (w["pool_w"], w["pool_b"][0][None, :], w["pool_scale"][0][None, :],
                 w["ln_gain"][1, 1][None, :], w["ln_bias"][1, 1][None, :])
    if prompt:
        h = _pool_prompt(h, prefix16, *pool_args, batch, seq, start, tm)
    else:
        xs_t = jnp.transpose(jnp.concatenate([prefix16, h.reshape(batch, seq, D_MODEL)], axis=1), (1, 0, 2))
        h = jnp.transpose(_pool_sample(xs_t, *pool_args, seq, start), (1, 0, 2)).reshape(m, D_MODEL)
    h = yield h
    return (h.reshape(batch, seq, D_MODEL), ckv.reshape(1, batch, seq, KV_LORA),
            kpe.reshape(1, batch, seq, QK_ROPE), ret_state[None], xp_tail[None])


def _finish(trunk, last):
    try:
        trunk.send(last)
    except StopIteration as done:
        return done.value
    raise AssertionError("trunk yielded more FFN requests than FFN_ORDER")


def kernel(x_prompt, x_sample, cache_mla_ckv, cache_mla_kpe, state_ret, state_pool, page_table, ffn_w_gate, ffn_w_up, ffn_w_down, ln_gain, ln_bias, mix_w_in, mla_q_norm, mla_kv_norm, mla_w_uq, mla_w_uk, mla_w_uv, ret_gn_gain, ret_gn_bias, mix_w_out, pool_w, pool_b, pool_scale):
    assert DEPTH == 2 and mix_w_in.shape[0] == 1 and pool_w.shape[0] == 1
    w = {
        "wg": ffn_w_gate.astype(BF16), "wu": ffn_w_up.astype(BF16), "wd": ffn_w_down.astype(BF16),
        "ln_gain": ln_gain, "ln_bias": ln_bias, "ret_gn_gain": ret_gn_gain, "ret_gn_bias": ret_gn_bias,
        "pool_w": pool_w[0].astype(BF16), "pool_b": pool_b, "pool_scale": pool_scale,
    }
    mw = _mixer_weights(mix_w_in[0], mla_q_norm[0], mla_kv_norm[0], mla_w_uq[0], mla_w_uk[0], mla_w_uv[0],
                        mix_w_out[0])
    bp = x_prompt.shape[0]
    zero_ret = jnp.zeros((bp, RET_HEADS, RET_DK, RET_DV), F32)
    zero_pool = jnp.zeros((bp, POOL_PREFIX, D_MODEL), x_prompt.dtype)
    trunk_p = _trunk(x_prompt, 0, zero_ret, zero_pool, None, w, mw)
    trunk_s = _trunk(x_sample, PAST_LEN, state_ret[0], state_pool[0],
                     (cache_mla_ckv[0], cache_mla_kpe[0], page_table), w, mw)
    h_p, h_s = next(trunk_p), next(trunk_s)
    for n, (layer, half) in enumerate(FFN_ORDER):
        f_p, f_s = _ffn_ln(h_p, h_s, w["wg"], w["wu"], w["wd"], layer, half,
                           ln_gain[layer, 2 * half][None, :], ln_bias[layer, 2 * half][None, :])
        if n + 1 < len(FFN_ORDER):
            h_p, h_s = trunk_p.send(f_p), trunk_s.send(f_s)
    y_p, ckv_p, kpe_p, ret_p, pool_p = _finish(trunk_p, f_p)
    y_s, ckv_s, kpe_s, ret_s, pool_s = _finish(trunk_s, f_s)
    return (y_p, y_s, ckv_p, kpe_p, ckv_s, kpe_s, ret_p, ret_s, pool_p, pool_s)
```

```python
import functools

import numpy as np
import jax
import jax.numpy as jnp
from jax import lax
from jax.experimental import pallas as pl
from jax.experimental.pallas import tpu as pltpu

F32 = jnp.float32
BF16 = jnp.bfloat16

D_MODEL = 1024
DEPTH = 2
PAST_LEN = 8192
PAGE_SIZE = 128
ALPHA = (2 * DEPTH) ** 0.25
D_FF = 2816
MLA_HEADS = 8
Q_LORA = 512
KV_LORA = 256
QK_NOPE = 64
QK_ROPE = 32
V_DIM = 64
RET_HEADS = 4
RET_DK = 128
RET_DV = 128
RET_CHUNK = 128
POOL_WINDOWS = (2, 4, 8, 16)
POOL_GROUPS = 4
POOL_GROUP = D_MODEL // POOL_GROUPS
POOL_PREFIX = 15
ROPE_BASE = 10000.0
LN_EPS = 1e-5
RMS_EPS = 1e-6
SPLIT_SIZES = (Q_LORA, KV_LORA, QK_ROPE, RET_HEADS * RET_DK, RET_HEADS * RET_DK,
               RET_HEADS * RET_DV, RET_HEADS * RET_DV)
ATT_SCALE = (QK_NOPE + QK_ROPE) ** -0.5 * 1.4426950408889634

LANES = 128
HEAD_PAD = LANES
D_HEADS = MLA_HEADS * HEAD_PAD
D_ATT = MLA_HEADS * V_DIM
D_RET = RET_HEADS * RET_DV
NEG = -1e30
VMEM_LIMIT = 56 * 1024 * 1024

_NT = (((1,), (1,)), ((), ()))
_TN = (((0,), (0,)), ((), ()))


def _params(*sem):
    return pltpu.CompilerParams(dimension_semantics=sem, vmem_limit_bytes=VMEM_LIMIT)


def _const_spec(shape):
    nd = len(shape)
    return pl.BlockSpec(shape, lambda *_: (0,) * nd, pipeline_mode=pl.Buffered(1))


def _layer_norm(y, g, b):
    mu = jnp.mean(y, axis=-1, keepdims=True)
    d = y - mu
    var = jnp.mean(d * d, axis=-1, keepdims=True)
    return d * lax.rsqrt(var + LN_EPS) * g + b


def _silu(x):
    return x * jax.nn.sigmoid(x)


FFN_CHUNK = 256
FFN_ROWS = 512


def _ffn_ln_kernel(xp_ref, xs_ref, wg_ref, wu_ref, wd_ref, g_ref, b_ref, op_ref, os_ref, *, n_prompt):
    is_prompt = pl.program_id(0) < n_prompt

    def half_step(x_ref, o_ref):
        x = x_ref[...]
        xb = x.astype(BF16)
        acc = None
        for c in range(D_FF // FFN_CHUNK):
            sl = slice(c * FFN_CHUNK, (c + 1) * FFN_CHUNK)
            g = jnp.dot(xb, wg_ref[:, sl].astype(BF16), preferred_element_type=F32)
            u = jnp.dot(xb, wu_ref[:, sl].astype(BF16), preferred_element_type=F32)
            a = (_silu(g) * u).astype(BF16)
            d = jnp.dot(a, wd_ref[sl, :].astype(BF16), preferred_element_type=F32)
            acc = d if acc is None else acc + d
        o_ref[...] = _layer_norm(ALPHA * x + 0.5 * acc, g_ref[...], b_ref[...])

    pl.when(is_prompt)(functools.partial(half_step, xp_ref, op_ref))
    pl.when(jnp.logical_not(is_prompt))(functools.partial(half_step, xs_ref, os_ref))


def _ffn_ln(xp, xs, wg, wu, wd, layer, half, g, b):
    tm = FFN_ROWS
    assert xp.shape[0] % tm == 0 and xs.shape[0] % tm == 0
    n_p, n_s = xp.shape[0] // tm, xs.shape[0] // tm
    pspec = pl.BlockSpec((tm, D_MODEL), lambda i: (jnp.minimum(i, n_p - 1), 0))
    sspec = pl.BlockSpec((tm, D_MODEL), lambda i: (jnp.maximum(i - n_p, 0), 0))

    def wspec(w):
        return pl.BlockSpec((None, None) + w.shape[2:], lambda i: (layer, half, 0, 0),
                            pipeline_mode=pl.Buffered(1))

    return pl.pallas_call(
        functools.partial(_ffn_ln_kernel, n_prompt=n_p),
        grid=(n_p + n_s,),
        in_specs=[pspec, sspec, wspec(wg), wspec(wu), wspec(wd), _const_spec(g.shape), _const_spec(b.shape)],
        out_specs=[pspec, sspec],
        out_shape=[jax.ShapeDtypeStruct(xp.shape, F32), jax.ShapeDtypeStruct(xs.shape, F32)],
        compiler_params=_params("arbitrary"),
        name="ffn_ln",
    )(xp, xs, wg, wu, wd, g, b)


_C_QL, _C_CKV, _C_KPE, _C_KPS, _C_RQ, _C_RK, _C_RV, _C_RG, _C_END = (
    0, 512, 768, 896, 1024, 1536, 2048, 2560, 3072)


def _prep_kernel(h_ref, w_ref, qn_ref, kvn_ref, uq1_ref, uq2_ref, wk_ref, wvt_ref,
                 cq_ref, sq_ref, cr_ref, sr_ref,
                 q_ref, k_ref, vt_ref, ckv_ref, kpe_ref, rq_ref, rk_ref, rv_ref, rg_ref):
    xb = h_ref[...].astype(BF16)

    def proj(a, b):
        return jnp.dot(xb, w_ref[:, a:b], preferred_element_type=F32)

    cq = cq_ref[...]
    sq = sq_ref[...]
    ql = proj(_C_QL, _C_CKV)
    qn = (ql * lax.rsqrt(jnp.mean(ql * ql, axis=-1, keepdims=True) + RMS_EPS) * qn_ref[...]).astype(BF16)
    for hh in range(MLA_HEADS):
        sl = slice(hh * HEAD_PAD, (hh + 1) * HEAD_PAD)
        a = jnp.dot(qn, uq1_ref[:, sl], preferred_element_type=F32)
        b = jnp.dot(qn, uq2_ref[:, sl], preferred_element_type=F32)
        q_ref[:, sl] = ((a * cq + b * sq) * ATT_SCALE).astype(BF16)
    c = proj(_C_CKV, _C_KPE)
    ckv = c * lax.rsqrt(jnp.mean(c * c, axis=-1, keepdims=True) + RMS_EPS) * kvn_ref[...]
    ckv_ref[...] = ckv
    cb = ckv.astype(BF16)
    kp = proj(_C_KPE, _C_RQ)
    kpe = kp[:, :HEAD_PAD] * cq + kp[:, HEAD_PAD:] * sq
    kpe_ref[...] = kpe
    kb = kpe.astype(BF16)
    k = (jnp.dot(cb, wk_ref[0:KV_LORA, :], preferred_element_type=F32)
         + jnp.dot(kb, wk_ref[KV_LORA:KV_LORA + HEAD_PAD, :], preferred_element_type=F32))
    k_ref[...] = k.astype(BF16)
    vt = lax.dot_general(wvt_ref[...], cb, _NT, preferred_element_type=F32)
    head_row = lax.broadcasted_iota(jnp.int32, vt.shape, 0) % HEAD_PAD
    vt_ref[0] = jnp.where(head_row == V_DIM, 1.0, vt).astype(BF16)
    cr = cr_ref[...]
    sr = sr_ref[...]
    rq = proj(_C_RQ, _C_RK)
    rk = proj(_C_RK, _C_RV)
    for hh in range(RET_HEADS):
        sl = slice(hh * RET_DK, (hh + 1) * RET_DK)
        xq = rq[:, sl]
        xk = rk[:, sl]
        rq_ref[:, sl] = xq * cr + pltpu.roll(xq, RET_DK // 2, 1) * sr
        rk_ref[:, sl] = (xk * cr + pltpu.roll(xk, RET_DK // 2, 1) * sr) * (RET_DK ** -0.5)
    rv_ref[...] = proj(_C_RV, _C_RG).astype(BF16)
    rg_ref[...] = proj(_C_RG, _C_END)


def _mixer_prep(h, mw, tabs, tm, vt_blk):
    m = h.shape[0]
    cq, sq, cr, sr = tabs
    tab_blocks = cq.shape[0] // tm
    per = vt_blk // tm

    def row(n):
        return pl.BlockSpec((tm, n), lambda i: (i, 0))

    tab = pl.BlockSpec((tm, LANES), lambda i: (i % tab_blocks, 0))
    consts = [mw["w_main"], mw["q_norm"], mw["kv_norm"], mw["uq1"], mw["uq2"], mw["wk"], mw["wv_t"]]
    out_shape = [
        jax.ShapeDtypeStruct((m, D_HEADS), BF16),
        jax.ShapeDtypeStruct((m, D_HEADS), BF16),
        jax.ShapeDtypeStruct((m // vt_blk, D_HEADS, vt_blk), BF16),
        jax.ShapeDtypeStruct((m, KV_LORA), F32),
        jax.ShapeDtypeStruct((m, LANES), F32),
        jax.ShapeDtypeStruct((m, D_RET), F32),
        jax.ShapeDtypeStruct((m, D_RET), F32),
        jax.ShapeDtypeStruct((m, D_RET), BF16),
        jax.ShapeDtypeStruct((m, D_RET), F32),
    ]
    return pl.pallas_call(
        _prep_kernel,
        grid=(m // tm,),
        in_specs=[row(D_MODEL)] + [_const_spec(c.shape) for c in consts] + [tab] * 4,
        out_specs=[pl.BlockSpec((1, D_HEADS, tm), lambda i: (i // per, 0, i % per)) if len(s.shape) == 3
                   else row(s.shape[1]) for s in out_shape],
        out_shape=out_shape,
        compiler_params=_params("parallel"),
        name="mixer_prep",
    )(h, *consts, cq, sq, cr, sr)


FLASH_HEADS = 2


def _flash_kernel(q_ref, k_ref, vt_ref, o_ref, *, blk):
    i = pl.program_id(2)
    lanes = [slice(a * HEAD_PAD, (a + 1) * HEAD_PAD) for a in range(FLASH_HEADS)]
    qs = [q_ref[:, sl] for sl in lanes]

    def scores_t(j):
        rows = pl.ds(pl.multiple_of(j * blk, blk), blk)
        return tuple(lax.dot_general(k_ref[rows, sl], q, _NT, preferred_element_type=F32)
                     for q, sl in zip(qs, lanes))

    def update(st, j, m, acc, sl):
        m_new = jnp.maximum(m, jnp.max(st, axis=0, keepdims=True))
        p = jnp.exp2(st - m_new).astype(BF16)
        acc = acc * jnp.exp2(m - m_new) + jnp.dot(vt_ref[j, sl, :], p, preferred_element_type=F32)
        return m_new, acc

    def body(j, carry):
        ms, accs = carry
        new = [update(st, j, m, acc, sl) for st, m, acc, sl in zip(scores_t(j), ms, accs, lanes)]
        return tuple(n[0] for n in new), tuple(n[1] for n in new)

    m0 = tuple(jnp.full((1, blk), NEG, F32) for _ in lanes)
    acc0 = tuple(jnp.zeros((HEAD_PAD, blk), F32) for _ in lanes)
    ms, accs = lax.fori_loop(0, i, body, (m0, acc0))
    key = lax.broadcasted_iota(jnp.int32, (blk, blk), 0)
    qry = lax.broadcasted_iota(jnp.int32, (blk, blk), 1)
    outs = []
    for st, m, acc, sl in zip(scores_t(i), ms, accs, lanes):
        m, acc = update(jnp.where(key <= qry, st, NEG), i, m, acc, sl)
        outs.append((acc / acc[V_DIM:V_DIM + 1, :])[:V_DIM])
    o_ref[...] = jnp.concatenate(outs, axis=0).T.astype(BF16)


def _flash_attention(q, k, vt, batch, seq, blk):
    nq = seq // blk
    width = FLASH_HEADS * HEAD_PAD
    qspec = pl.BlockSpec((blk, width), lambda b, h, i: (b * nq + i, h))
    kspec = pl.BlockSpec((seq, width), lambda b, h, i: (b, h))
    vspec = pl.BlockSpec((nq, width, blk), lambda b, h, i: (b, h, 0))
    return pl.pallas_call(
        functools.partial(_flash_kernel, blk=blk),
        grid=(batch, MLA_HEADS // FLASH_HEADS, nq),
        in_specs=[qspec, kspec, vspec],
        out_specs=pl.BlockSpec((blk, FLASH_HEADS * V_DIM), lambda b, h, i: (b * nq + i, h)),
        out_shape=jax.ShapeDtypeStruct((batch * seq, D_ATT), BF16),
        compiler_params=_params("parallel", "parallel", "arbitrary"),
        name="flash_attention",
    )(q, k, vt)


T_PAD = 8
NEW_PAD = PAGE_SIZE
KEY_CHUNK = 1024


def _paged_kernel(pt_ref, q_ref, cn_ref, kn_ref, wq_ref, wuv_ref, ckv_hbm, kpe_hbm, o_ref,
                  ckv_buf, kpe_buf, kb_ref, s_ref, p_ref, sem, *, n_pages, n_new):
    b = pl.program_id(0)
    nb = pl.num_programs(0)
    past = n_pages * PAGE_SIZE
    slot = b % 2

    def page_copies(bb, sl, p):
        page = pt_ref[bb, p]
        rows = pl.ds(p * PAGE_SIZE, PAGE_SIZE)
        return (pltpu.make_async_copy(ckv_hbm.at[page], ckv_buf.at[sl, rows, :], sem.at[sl, 0]),
                pltpu.make_async_copy(kpe_hbm.at[page], kpe_buf.at[sl, :, rows], sem.at[sl, 1]))

    def start_fetch(bb, sl):
        for p in range(n_pages):
            for cp in page_copies(bb, sl, p):
                cp.start()

    def wait_fetch(bb, sl):
        for p in range(n_pages):
            for cp in page_copies(bb, sl, p):
                cp.wait()

    @pl.when(b == 0)
    def _():
        ckv_buf[:, past:, :] = jnp.zeros((2, NEW_PAD, KV_LORA), F32)
        start_fetch(0, 0)

    @pl.when(b + 1 < nb)
    def _():
        start_fetch(b + 1, 1 - slot)

    q8 = q_ref[0]
    qrep = jnp.concatenate([q8] * MLA_HEADS, axis=0)
    n_rows = MLA_HEADS * T_PAD
    row_h = lax.broadcasted_iota(jnp.int32, (n_rows, D_HEADS), 0) // T_PAD
    col_h = lax.broadcasted_iota(jnp.int32, (n_rows, D_HEADS), 1) // HEAD_PAD
    qm = jnp.where(row_h == col_h, qrep, jnp.zeros_like(qrep))
    ql = jnp.dot(qm, wq_ref[...], preferred_element_type=F32)
    q_lat = ql[:, :KV_LORA].astype(BF16)
    q_pe = ql[:, KV_LORA:KV_LORA + QK_ROPE].astype(BF16)

    wait_fetch(b, slot)
    ckv_buf[slot, past:past + T_PAD, :] = cn_ref[0]
    kpe_buf[slot, :, past:] = kn_ref[0]

    chunk = min(KEY_CHUNK, past)
    bounds = [(c * chunk, chunk) for c in range(past // chunk)] + [(past, NEW_PAD)]
    qt = lax.broadcasted_iota(jnp.int32, (n_rows, NEW_PAD), 0) % T_PAD
    kt = lax.broadcasted_iota(jnp.int32, (n_rows, NEW_PAD), 1)
    for r0, n in bounds:
        kb_ref[r0:r0 + n, :] = ckv_buf[slot, r0:r0 + n, :].astype(BF16)
    s_ref[...] = (lax.dot_general(q_lat, kb_ref[...], _NT, preferred_element_type=F32)
                  + jnp.dot(q_pe, kpe_buf[slot].astype(BF16), preferred_element_type=F32))
    s_ref[:, past:] = jnp.where((kt <= qt) & (kt < n_new), s_ref[:, past:], NEG)
    m = jnp.max(s_ref[...], axis=-1, keepdims=True)
    l = jnp.zeros((n_rows, 1), F32)
    for r0, n in bounds:
        p = jnp.exp2(s_ref[:, r0:r0 + n] - m)
        l = l + jnp.sum(p, axis=-1, keepdims=True)
        p_ref[:, r0:r0 + n] = p.astype(BF16)
    o = jnp.dot(p_ref[...], kb_ref[...], preferred_element_type=F32)
    o_lat = (o / l).astype(BF16)
    pv = jnp.dot(o_lat, wuv_ref[...], preferred_element_type=F32)
    out_row_h = lax.broadcasted_iota(jnp.int32, (n_rows, D_ATT), 0) // T_PAD
    out_col_h = lax.broadcasted_iota(jnp.int32, (n_rows, D_ATT), 1) // V_DIM
    pv = jnp.where(out_row_h == out_col_h, pv, 0.0)
    out = pv[0:T_PAD]
    for hh in range(1, MLA_HEADS):
        out = out + pv[hh * T_PAD:(hh + 1) * T_PAD]
    o_ref[0] = out.astype(BF16)


def _paged_attention(page_table, q8, ckv_new8, kpe_new_t, wq, wuv, cache_ckv, cache_kpe_t, n_new):
    nb, n_pages = page_table.shape
    rows = n_pages * PAGE_SIZE + NEW_PAD
    grid_spec = pltpu.PrefetchScalarGridSpec(
        num_scalar_prefetch=1,
        grid=(nb,),
        in_specs=[
            pl.BlockSpec((1, T_PAD, D_HEADS), lambda b, pt: (b, 0, 0)),
            pl.BlockSpec((1, T_PAD, KV_LORA), lambda b, pt: (b, 0, 0)),
            pl.BlockSpec((1, QK_ROPE, NEW_PAD), lambda b, pt: (b, 0, 0)),
            pl.BlockSpec(wq.shape, lambda b, pt: (0, 0)),
            pl.BlockSpec(wuv.shape, lambda b, pt: (0, 0)),
            pl.BlockSpec(memory_space=pl.ANY),
            pl.BlockSpec(memory_space=pl.ANY),
        ],
        out_specs=pl.BlockSpec((1, T_PAD, D_ATT), lambda b, pt: (b, 0, 0)),
        scratch_shapes=[
            pltpu.VMEM((2, rows, KV_LORA), F32),
            pltpu.VMEM((2, QK_ROPE, rows), F32),
            pltpu.VMEM((rows, KV_LORA), BF16),
            pltpu.VMEM((MLA_HEADS * T_PAD, rows), F32),
            pltpu.VMEM((MLA_HEADS * T_PAD, rows), BF16),
            pltpu.SemaphoreType.DMA((2, 2)),
        ],
    )
    return pl.pallas_call(
        functools.partial(_paged_kernel, n_pages=n_pages, n_new=n_new),
        grid_spec=grid_spec,
        out_shape=jax.ShapeDtypeStruct((nb, T_PAD, D_ATT), BF16),
        compiler_params=_params("arbitrary"),
        name="paged_attention",
    )(page_table, q8, ckv_new8, kpe_new_t, wq, wuv, cache_ckv, cache_kpe_t)


def _group_norm_gate(o, gate, gain, bias):
    mu = jnp.mean(o, axis=-1, keepdims=True)
    d = o - mu
    var = jnp.mean(d * d, axis=-1, keepdims=True)
    return _silu(gate) * (d * lax.rsqrt(var + LN_EPS) * gain + bias)


def _ret_kernel(rq_ref, rk_ref, rv_ref, rg_ref, din_ref, qd_ref, kd_ref, gc_ref, gg_ref, gb_ref, s0_ref,
                ro_ref, so_ref, s_ref, *, chunks):
    i = pl.program_id(1)

    @pl.when(i == 0)
    def _():
        s_ref[...] = s0_ref[0]

    for hh in range(RET_HEADS):
        sl = slice(hh * RET_DK, (hh + 1) * RET_DK)
        din = din_ref[:, sl]
        qd = qd_ref[:, sl]
        kd = kd_ref[:, sl]
        gc = gc_ref[:, sl]
        gain = gg_ref[:, sl]
        bias = gb_ref[:, sl]
        s = s_ref[hh]
        for c in range(chunks):
            rows = slice(c * RET_CHUNK, (c + 1) * RET_CHUNK)
            qh = rq_ref[rows, sl]
            kh = rk_ref[rows, sl]
            vh = rv_ref[rows, sl]
            inner = lax.dot_general(qh.astype(BF16), kh.astype(BF16), _NT, preferred_element_type=F32) * din
            o = (jnp.dot(inner.astype(BF16), vh, preferred_element_type=F32)
                 + jnp.dot((qh * qd).astype(BF16), s.astype(BF16), preferred_element_type=F32))
            s = s * gc + lax.dot_general((kh * kd).astype(BF16), vh, _TN, preferred_element_type=F32)
            ro_ref[rows, sl] = _group_norm_gate(o, rg_ref[rows, sl], gain, bias).astype(BF16)
        s_ref[hh] = s

    @pl.when(i == pl.num_programs(1) - 1)
    def _():
        so_ref[0] = s_ref[...]


def _retention_prompt(rq, rk, rv, rg, dec, gn_gain, gn_bias, state0, batch, seq, chunks):
    rows = chunks * RET_CHUNK
    steps = seq // rows
    rspec = pl.BlockSpec((rows, D_RET), lambda b, i: (b * steps + i, 0))
    sspec = pl.BlockSpec((1, RET_HEADS, RET_DK, RET_DV), lambda b, i: (b, 0, 0, 0))
    consts = [dec["din"], dec["qdec"], dec["kdec"], dec["gc"], gn_gain, gn_bias]
    return pl.pallas_call(
        functools.partial(_ret_kernel, chunks=chunks),
        grid=(batch, steps),
        in_specs=[rspec] * 4 + [_const_spec(c.shape) for c in consts] + [sspec],
        out_specs=[rspec, sspec],
        out_shape=[jax.ShapeDtypeStruct((batch * seq, D_RET), BF16),
                   jax.ShapeDtypeStruct((batch, RET_HEADS, RET_DK, RET_DV), F32)],
        scratch_shapes=[pltpu.VMEM((RET_HEADS, RET_DK, RET_DV), F32)],
        compiler_params=_params("parallel", "arbitrary"),
        name="retention_prompt",
    )(rq, rk, rv, rg, *consts, state0)


RS_BATCH = 8


def _ret_sample_kernel(q_ref, k_ref, v_ref, rg_ref, kt_ref, qd_ref, kd_ref, din_ref, gc_ref, gg_ref, gb_ref,
                       s0_ref, ro_ref, so_ref, *, n_tok):
    def one_seq(bi, carry):
        q8 = q_ref[bi]
        k8 = k_ref[bi]
        v8 = v_ref[bi]
        g8 = rg_ref[bi]
        qs = (q8 * qd_ref[...]).astype(BF16)
        outs = []
        for hh in range(RET_HEADS):
            sl = slice(hh * RET_DK, (hh + 1) * RET_DK)
            o = jnp.dot(qs[:, sl], s0_ref[bi, hh].astype(BF16), preferred_element_type=F32)
            din = din_ref[hh]
            for m in range(n_tok):
                a = jnp.sum(q8[:, sl] * k8[m:m + 1, sl], axis=-1, keepdims=True) * din[:, m:m + 1]
                o = o + a * v8[m:m + 1, sl]
            outs.append(_group_norm_gate(o, g8[:, sl], gg_ref[:, sl], gb_ref[:, sl]))
        ro_ref[bi] = jnp.concatenate(outs, axis=1).astype(BF16)
        vbd = jnp.concatenate([v8 * kd_ref[hh] for hh in range(RET_HEADS)]
                              + [jnp.zeros((LANES - RET_HEADS * T_PAD, D_RET), F32)], axis=0)
        upd = jnp.dot(kt_ref[bi].astype(BF16), vbd.astype(BF16), preferred_element_type=F32)
        for hh in range(RET_HEADS):
            sl = slice(hh * RET_DV, (hh + 1) * RET_DV)
            so_ref[bi, hh] = s0_ref[bi, hh] * gc_ref[:, sl] + upd[:, sl]
        return carry

    lax.fori_loop(0, RS_BATCH, one_seq, 0)


def _retention_sample(q8, k8, v8, rg8, k_t, dec, gn_gain, gn_bias, state0, n_tok):
    nb = q8.shape[0]
    rspec = pl.BlockSpec((RS_BATCH, T_PAD, D_RET), lambda i: (i, 0, 0))
    sspec = pl.BlockSpec((RS_BATCH, RET_HEADS, RET_DK, RET_DV), lambda i: (i, 0, 0, 0))
    consts = [dec["qdec"], dec["kdec"], dec["din"], dec["gc"], gn_gain, gn_bias]
    return pl.pallas_call(
        functools.partial(_ret_sample_kernel, n_tok=n_tok),
        grid=(nb // RS_BATCH,),
        in_specs=[rspec] * 4 + [pl.BlockSpec((RS_BATCH, RET_DK, LANES), lambda i: (i, 0, 0))]
                 + [_const_spec(c.shape) for c in consts] + [sspec],
        out_specs=[rspec, sspec],
        out_shape=[jax.ShapeDtypeStruct((nb, T_PAD, D_RET), BF16),
                   jax.ShapeDtypeStruct((nb, RET_HEADS, RET_DK, RET_DV), F32)],
        compiler_params=_params("parallel"),
        name="retention_sample",
    )(q8, k8, v8, rg8, k_t, *consts, state0)


def _outproj_ln_kernel(a_ref, ro_ref, h_ref, wa_ref, wr_ref, g_ref, b_ref, o_ref):
    y = (jnp.dot(a_ref[...], wa_ref[...], preferred_element_type=F32)
         + jnp.dot(ro_ref[...], wr_ref[...], preferred_element_type=F32))
    o_ref[...] = _layer_norm(ALPHA * h_ref[...] + y, g_ref[...], b_ref[...])


def _outproj_ln(a, ro, h, wa, wr, g, b, tm):
    m = h.shape[0]

    def row(n):
        return pl.BlockSpec((tm, n), lambda i: (i, 0))

    return pl.pallas_call(
        _outproj_ln_kernel,
        grid=(m // tm,),
        in_specs=[row(D_ATT), row(D_RET), row(D_MODEL), _const_spec(wa.shape), _const_spec(wr.shape),
                  _const_spec(g.shape), _const_spec(b.shape)],
        out_specs=row(D_MODEL),
        out_shape=jax.ShapeDtypeStruct((m, D_MODEL), F32),
        compiler_params=_params("parallel"),
        name="outproj_ln",
    )(a, ro, h, wa, wr, g, b)


HALO = 16


def _pool_tail(pooled_groups, x, pw_ref, pb_ref, ps_ref, g_ref, b_ref):
    ys = [jnp.dot(p.astype(BF16), pw_ref[gi], preferred_element_type=F32) for gi, p in enumerate(pooled_groups)]
    y = (jnp.concatenate(ys, axis=-1) + pb_ref[...]) * ps_ref[...]
    return _layer_norm(ALPHA * x + y, g_ref[...], b_ref[...])


HALO_P = 32


def _pool_prompt_kernel(h_ref, halo_ref, pre_ref, pw_ref, pb_ref, ps_ref, g_ref, b_ref, o_ref,
                        xs_ref, a_ref, b2_ref, *, tm, tiles, start):
    t = pl.program_id(0) % tiles
    x = h_ref[...]
    xs_ref[0:HALO_P, :] = jnp.where(t == 0, pre_ref[0], halo_ref[...])
    xs_ref[HALO_P:, :] = x
    n = HALO_P + tm
    pos = start + t * tm + lax.broadcasted_iota(jnp.int32, (tm, 1), 0)
    pooled = []
    src = xs_ref
    for k, wl in enumerate(POOL_WINDOWS, start=1):
        lo = (k - 1) * POOL_GROUP
        r0, shift = 8 * k, wl // 2
        level = src[r0:n, lo:] + src[r0 - shift:n - shift, lo:]
        cnt = jnp.minimum(pos + 1, wl).astype(F32)
        pooled.append(level[HALO_P - r0:, :POOL_GROUP] / cnt - x[:, lo:lo + POOL_GROUP])
        if k < len(POOL_WINDOWS):
            dst = a_ref if k % 2 else b2_ref
            dst[r0:n, lo + POOL_GROUP:] = level[:, POOL_GROUP:]
            src = dst
    o_ref[...] = _pool_tail(pooled, x, pw_ref, pb_ref, ps_ref, g_ref, b_ref)


def _pool_prompt(h, prefix, pw, pb, ps, g, b, batch, seq, start, tm):
    assert all(w == 2 ** (i + 1) for i, w in enumerate(POOL_WINDOWS)) and 8 * len(POOL_WINDOWS) <= HALO_P
    tiles = seq // tm
    per = tm // HALO_P
    consts = [pw, pb, ps, g, b]
    buf = pltpu.VMEM((HALO_P + tm, D_MODEL), F32)
    return pl.pallas_call(
        functools.partial(_pool_prompt_kernel, tm=tm, tiles=tiles, start=start),
        grid=(batch * tiles,),
        in_specs=[pl.BlockSpec((tm, D_MODEL), lambda i: (i, 0)),
                  pl.BlockSpec((HALO_P, D_MODEL), lambda i: (jnp.maximum(i * per - 1, 0), 0)),
                  pl.BlockSpec((1, HALO_P, D_MODEL), lambda i: (i // tiles, 0, 0))]
                 + [_const_spec(c.shape) for c in consts],
        out_specs=pl.BlockSpec((tm, D_MODEL), lambda i: (i, 0)),
        out_shape=jax.ShapeDtypeStruct((batch * seq, D_MODEL), F32),
        scratch_shapes=[buf, buf, buf],
        compiler_params=_params("parallel"),
        name="pool_prompt",
    )(h, h, prefix, *consts)


def _pool_sample_kernel(xs_ref, pw_ref, pb_ref, ps_ref, g_ref, b_ref, o_ref, *, n_tok, start):
    nb = xs_ref.shape[1]
    x = jnp.concatenate([xs_ref[HALO + t] for t in range(n_tok)], axis=0)
    pooled = []
    for gi, wl in enumerate(POOL_WINDOWS):
        sl = slice(gi * POOL_GROUP, (gi + 1) * POOL_GROUP)
        parts = []
        for t in range(n_tok):
            acc = xs_ref[HALO + t, :, sl]
            for d in range(1, wl):
                acc = acc + xs_ref[HALO + t - d, :, sl]
            parts.append(acc / float(min(start + t + 1, wl)))
        pooled.append(jnp.concatenate(parts, axis=0) - x[:, sl])
    y = _pool_tail(pooled, x, pw_ref, pb_ref, ps_ref, g_ref, b_ref)
    for t in range(n_tok):
        o_ref[t] = y[t * nb:(t + 1) * nb]


def _pool_sample(xs_t, pw, pb, ps, g, b, n_tok, start):
    nb = xs_t.shape[1]
    args = [xs_t, pw, pb, ps, g, b]
    return pl.pallas_call(
        functools.partial(_pool_sample_kernel, n_tok=n_tok, start=start),
        grid=(1,),
        in_specs=[_const_spec(a.shape) for a in args],
        out_specs=_const_spec((n_tok, nb, D_MODEL)),
        out_shape=jax.ShapeDtypeStruct((n_tok, nb, D_MODEL), F32),
        compiler_params=_params("arbitrary"),
        name="pool_sample",
    )(*args)


def _head_pad(w, width):
    r, nh, d = w.shape
    out = jnp.zeros((r, nh, HEAD_PAD), w.dtype).at[:, :, :d].set(w)
    return out.reshape(r, nh * HEAD_PAD)[:, :width]


def _mixer_weights(mix_w_in, q_norm, kv_norm, w_uq, w_uk, w_uv, mix_w_out):
    offs = np.concatenate([[0], np.cumsum(SPLIT_SIZES)])
    wq, wckv, wkpe, wrq, wrk, wrv, wrg = [mix_w_in[:, offs[i]:offs[i + 1]] for i in range(7)]
    half = QK_ROPE // 2
    z_lo = jnp.zeros((D_MODEL, QK_NOPE), F32)
    z_hi = jnp.zeros((D_MODEL, HEAD_PAD - QK_NOPE - QK_ROPE), F32)
    kpe_blk = jnp.concatenate([z_lo, wkpe, z_hi], axis=1)
    kpe_swp = jnp.concatenate([z_lo, -wkpe[:, half:], wkpe[:, :half], z_hi], axis=1)
    w_main = jnp.concatenate([wq, wckv, kpe_blk, kpe_swp, wrq, wrk, wrv, wrg], axis=1).astype(BF16)
    pe = w_uq[:, :, QK_NOPE:]
    uq1 = _head_pad(w_uq, D_HEADS)
    uq2 = _head_pad(jnp.concatenate([jnp.zeros_like(w_uq[:, :, :QK_NOPE]), -pe[:, :, half:], pe[:, :, :half]], axis=2),
                    D_HEADS)
    lane = np.arange(HEAD_PAD)
    sel = ((lane[:, None] == lane[None, :]) & (lane[:, None] >= QK_NOPE) & (lane[:, None] < QK_NOPE + QK_ROPE))
    e_mat = jnp.asarray(np.tile(sel.astype(np.float32), (1, MLA_HEADS)))
    wk = jnp.concatenate([_head_pad(w_uk, D_HEADS), e_mat], axis=0)
    wv = _head_pad(w_uv, D_HEADS)
    wa = mix_w_out[:D_ATT]
    wr = mix_w_out[MLA_HEADS * V_DIM:]
    wuk_t = jnp.zeros((MLA_HEADS, HEAD_PAD, KV_LORA), F32).at[:, :QK_NOPE, :].set(
        jnp.transpose(w_uk, (1, 2, 0))).reshape(D_HEADS, KV_LORA)
    r = np.arange(D_HEADS) % HEAD_PAD
    e_pe = ((r[:, None] - QK_NOPE) == np.arange(LANES)[None, :]) & (r[:, None] >= QK_NOPE) & (r[:, None] < QK_NOPE + QK_ROPE)
    wq_abs = jnp.concatenate([wuk_t, jnp.asarray(e_pe.astype(np.float32))], axis=1)
    return {
        "w_main": w_main, "q_norm": q_norm[None, :], "kv_norm": kv_norm[None, :],
        "uq1": uq1.astype(BF16), "uq2": uq2.astype(BF16), "wk": wk.astype(BF16),
        "wv_c": w_uv.reshape(KV_LORA, D_ATT).astype(BF16),
        "wv_t": wv.T.astype(BF16), "wa": wa.astype(BF16), "wr": wr.astype(BF16),
        "wq_abs": wq_abs.astype(BF16),
    }


def _rope_tables(pos):
    def angles(r):
        inv = 1.0 / (ROPE_BASE ** (jnp.arange(0, r, 2, dtype=F32) / r))
        return pos.astype(F32)[:, None] * inv[None, :]

    n = pos.shape[0]
    a = angles(QK_ROPE)
    c, s = jnp.cos(a), jnp.sin(a)
    hi = HEAD_PAD - QK_NOPE - QK_ROPE
    cq = jnp.concatenate([jnp.ones((n, QK_NOPE), F32), c, c, jnp.ones((n, hi), F32)], axis=1)
    sq = jnp.concatenate([jnp.zeros((n, QK_NOPE), F32), s, s, jnp.zeros((n, hi), F32)], axis=1)
    a = angles(RET_DK)
    c, s = jnp.cos(a), jnp.sin(a)
    return cq, sq, jnp.concatenate([c, c], axis=1), jnp.concatenate([-s, s], axis=1)


def _ret_log_decay():
    return jnp.log(1.0 - 2.0 ** (-5.0 - jnp.arange(RET_HEADS, dtype=F32)))


def _ret_decay_tables(chunk):
    log_g = _ret_log_decay()
    idx = jnp.arange(chunk, dtype=F32)
    diff = idx[:, None] - idx[None, :]
    d_in = jnp.where(diff >= 0, jnp.exp(jnp.maximum(diff, 0.0)[None] * log_g[:, None, None]), 0.0)
    q_dec = jnp.exp((idx + 1.0)[None, :] * log_g[:, None])
    k_dec = jnp.exp((chunk - 1.0 - idx)[None, :] * log_g[:, None])
    g_c = jnp.exp(chunk * log_g)
    lanes = (chunk, RET_HEADS * RET_DK)
    return {
        "din": jnp.transpose(d_in, (1, 0, 2)).reshape(chunk, RET_HEADS * chunk),
        "qdec": jnp.broadcast_to(q_dec.T[:, :, None], (chunk, RET_HEADS, RET_DK)).reshape(lanes),
        "kdec": jnp.broadcast_to(k_dec.T[:, :, None], (chunk, RET_HEADS, RET_DK)).reshape(lanes),
        "gc": jnp.broadcast_to(g_c[:, None], (RET_HEADS, RET_DV)).reshape(1, RET_HEADS * RET_DV),
    }


def _ret_sample_tables(n_tok):
    log_g = _ret_log_decay()
    idx = jnp.arange(n_tok, dtype=F32)
    diff = idx[:, None] - idx[None, :]
    d_in = jnp.where(diff >= 0, jnp.exp(jnp.maximum(diff, 0.0)[None] * log_g[:, None, None]), 0.0)
    q_dec = jnp.exp((idx + 1.0)[None, :] * log_g[:, None])
    k_dec = jnp.exp((n_tok - 1.0 - idx)[None, :] * log_g[:, None])
    g_c = jnp.exp(n_tok * log_g)
    pad_t = T_PAD - n_tok
    own_lanes = (np.arange(D_RET) // RET_DV)[None, None, :] == np.arange(RET_HEADS)[:, None, None]
    return {
        "qdec": jnp.pad(jnp.broadcast_to(q_dec.T[:, :, None], (n_tok, RET_HEADS, RET_DK)).reshape(n_tok, D_RET),
                        ((0, pad_t), (0, 0))),
        "kdec": jnp.pad(k_dec[:, :, None] * jnp.asarray(own_lanes, F32), ((0, 0), (0, pad_t), (0, 0))),
        "din": jnp.pad(d_in, ((0, 0), (0, pad_t), (0, LANES - n_tok))),
        "gc": jnp.broadcast_to(g_c[:, None], (RET_HEADS, RET_DV)).reshape(1, D_RET),
    }


FFN_ORDER = ((0, 0), (0, 1), (1, 0), (1, 1))


def _trunk(x, start, ret_state0, pool_prefix, mla_cache, w, mw):
    batch, seq, _ = x.shape
    m = batch * seq
    prompt = mla_cache is None
    tm = min(512, m)
    h = x.reshape(m, D_MODEL)

    h = yield h
    if prompt:
        pos = start + jnp.arange(seq, dtype=jnp.int32)
    else:
        pos = start + (jnp.arange(m, dtype=jnp.int32) % seq)
    flash_blk = min(1024, seq) if prompt else tm
    q, k, vt, ckv, kpe, rq, rk, rv, rg = _mixer_prep(h, mw, _rope_tables(pos), tm, flash_blk)
    kpe = kpe[:, QK_NOPE:QK_NOPE + QK_ROPE]
    gn_gain = w["ret_gn_gain"][0][None, :]
    gn_bias = w["ret_gn_bias"][0][None, :]
    if prompt:
        a = _flash_attention(q, k, vt, batch, seq, flash_blk)
        chunk = RET_CHUNK if seq % RET_CHUNK == 0 else seq
        assert chunk == RET_CHUNK
        ro, ret_state = _retention_prompt(rq, rk, rv, rg, _ret_decay_tables(chunk), gn_gain, gn_bias,
                                          ret_state0, batch, seq, 8)
    else:
        assert seq <= T_PAD and seq % RET_CHUNK != 0
        cache_ckv, cache_kpe, page_table = mla_cache
        pad_t = ((0, 0), (0, T_PAD - seq), (0, 0))
        q8 = jnp.pad(q.reshape(batch, seq, D_HEADS), pad_t)
        cn8 = jnp.pad(ckv.reshape(batch, seq, KV_LORA), pad_t)
        kn_t = jnp.pad(jnp.swapaxes(kpe.reshape(batch, seq, QK_ROPE), 1, 2), ((0, 0), (0, 0), (0, NEW_PAD - seq)))
        a8 = _paged_attention(page_table, q8, cn8, kn_t, mw["wq_abs"], mw["wv_c"], cache_ckv,
                              jnp.swapaxes(cache_kpe, 1, 2), seq)
        a = a8[:, :seq].reshape(m, D_ATT)

        def rows8(t):
            return jnp.pad(t.reshape(batch, seq, D_RET), pad_t)

        k8 = rows8(rk)
        k_t = jnp.transpose(k8.reshape(batch, T_PAD, RET_HEADS, RET_DK), (0, 3, 2, 1)).reshape(
            batch, RET_DK, RET_HEADS * T_PAD)
        k_t = jnp.pad(k_t, ((0, 0), (0, 0), (0, LANES - RET_HEADS * T_PAD)))
        ro8, ret_state = _retention_sample(rows8(rq), k8, rows8(rv.astype(F32)), rows8(rg), k_t,
                                           _ret_sample_tables(seq), gn_gain, gn_bias, ret_state0, seq)
        ro = ro8[:, :seq].reshape(m, D_RET)
    h = _outproj_ln(a, ro, h, mw["wa"], mw["wr"], w["ln_gain"][0, 1][None, :], w["ln_bias"][0, 1][None, :], tm)
    h = yield h

    h = yield h
    xp_tail = jnp.concatenate([pool_prefix, h.reshape(batch, seq, D_MODEL)], axis=1)[:, -POOL_PREFIX:]
    halo = HALO_P if prompt else HALO
    prefix16 = jnp.pad(pool_prefix, ((0, 0), (halo - POOL_PREFIX, 0), (0, 0)))
    pool_args = (w["pool_w"], w["pool_b"][0][None, :], w["pool_scale"][0][None, :],
                 w["ln_gain"][1, 1][None, :], w["ln_bias"][1, 1][None, :])
    if prompt:
        h = _pool_prompt(h, prefix16, *pool_args, batch, seq, start, tm)
    else:
        xs_t = jnp.transpose(jnp.concatenate([prefix16, h.reshape(batch, seq, D_MODEL)], axis=1), (1, 0, 2))
        h = jnp.transpose(_pool_sample(xs_t, *pool_args, seq, start), (1, 0, 2)).reshape(m, D_MODEL)
    h = yield h
    return (h.reshape(batch, seq, D_MODEL), ckv.reshape(1, batch, seq, KV_LORA),
            kpe.reshape(1, batch, seq, QK_ROPE), ret_state[None], xp_tail[None])


def _finish(trunk, last):
    try:
        trunk.send(last)
    except StopIteration as done:
        return done.value
    raise AssertionError("trunk yielded more FFN requests than FFN_ORDER")


def kernel(x_prompt, x_sample, cache_mla_ckv, cache_mla_kpe, state_ret, state_pool, page_table, ffn_w_gate, ffn_w_up, ffn_w_down, ln_gain, ln_bias, mix_w_in, mla_q_norm, mla_kv_norm, mla_w_uq, mla_w_uk, mla_w_uv, ret_gn_gain, ret_gn_bias, mix_w_out, pool_w, pool_b, pool_scale):
    assert DEPTH == 2 and mix_w_in.shape[0] == 1 and pool_w.shape[0] == 1
    w = {
        "wg": ffn_w_gate, "wu": ffn_w_up, "wd": ffn_w_down,
        "ln_gain": ln_gain, "ln_bias": ln_bias, "ret_gn_gain": ret_gn_gain, "ret_gn_bias": ret_gn_bias,
        "pool_w": pool_w[0].astype(BF16), "pool_b": pool_b, "pool_scale": pool_scale,
    }
    mw = _mixer_weights(mix_w_in[0], mla_q_norm[0], mla_kv_norm[0], mla_w_uq[0], mla_w_uk[0], mla_w_uv[0],
                        mix_w_out[0])
    bp = x_prompt.shape[0]
    zero_ret = jnp.zeros((bp, RET_HEADS, RET_DK, RET_DV), F32)
    zero_pool = jnp.zeros((bp, POOL_PREFIX, D_MODEL), x_prompt.dtype)
    trunk_p = _trunk(x_prompt, 0, zero_ret, zero_pool, None, w, mw)
    trunk_s = _trunk(x_sample, PAST_LEN, state_ret[0], state_pool[0],
                     (cache_mla_ckv[0], cache_mla_kpe[0], page_table), w, mw)
    h_p, h_s = next(trunk_p), next(trunk_s)
    for n, (layer, half) in enumerate(FFN_ORDER):
        f_p, f_s = _ffn_ln(h_p, h_s, w["wg"], w["wu"], w["wd"], layer, half,
                           ln_gain[layer, 2 * half][None, :], ln_bias[layer, 2 * half][None, :])
        if n + 1 < len(FFN_ORDER):
            h_p, h_s = trunk_p.send(f_p), trunk_s.send(f_s)
    y_p, ckv_p, kpe_p, ret_p, pool_p = _finish(trunk_p, f_p)
    y_s, ckv_s, kpe_s, ret_s, pool_s = _finish(trunk_s, f_s)
    return (y_p, y_s, ckv_p, kpe_p, ckv_s, kpe_s, ret_p, ret_s, pool_p, pool_s)
```

```python
import functools

import numpy as np
import jax
import jax.numpy as jnp
from jax import lax
from jax.experimental import pallas as pl
from jax.experimental.pallas import tpu as pltpu

F32 = jnp.float32
BF16 = jnp.bfloat16

D_MODEL = 1024
DEPTH = 2
PAST_LEN = 8192
PAGE_SIZE = 128
ALPHA = (2 * DEPTH) ** 0.25
D_FF = 2816
MLA_HEADS = 8
Q_LORA = 512
KV_LORA = 256
QK_NOPE = 64
QK_ROPE = 32
V_DIM = 64
RET_HEADS = 4
RET_DK = 128
RET_DV = 128
RET_CHUNK = 128
POOL_WINDOWS = (2, 4, 8, 16)
POOL_GROUPS = 4
POOL_GROUP = D_MODEL // POOL_GROUPS
POOL_PREFIX = 15
ROPE_BASE = 10000.0
LN_EPS = 1e-5
RMS_EPS = 1e-6
SPLIT_SIZES = (Q_LORA, KV_LORA, QK_ROPE, RET_HEADS * RET_DK, RET_HEADS * RET_DK,
               RET_HEADS * RET_DV, RET_HEADS * RET_DV)
ATT_SCALE = (QK_NOPE + QK_ROPE) ** -0.5 * 1.4426950408889634

LANES = 128
HEAD_PAD = LANES
D_HEADS = MLA_HEADS * HEAD_PAD
D_ATT = MLA_HEADS * V_DIM
D_RET = RET_HEADS * RET_DV
NEG = -1e30
VMEM_LIMIT = 56 * 1024 * 1024

_NT = (((1,), (1,)), ((), ()))
_TN = (((0,), (0,)), ((), ()))


def _params(*sem):
    return pltpu.CompilerParams(dimension_semantics=sem, vmem_limit_bytes=VMEM_LIMIT)


def _const_spec(shape):
    nd = len(shape)
    return pl.BlockSpec(shape, lambda *_: (0,) * nd, pipeline_mode=pl.Buffered(1))


def _layer_norm(y, g, b):
    mu = jnp.mean(y, axis=-1, keepdims=True)
    d = y - mu
    var = jnp.mean(d * d, axis=-1, keepdims=True)
    return d * lax.rsqrt(var + LN_EPS) * g + b


def _silu(x):
    return x * jax.nn.sigmoid(x)


FFN_CHUNK = 256
FFN_ROWS = 512


def _ffn_ln_kernel(xp_ref, xs_ref, wg_ref, wu_ref, wd_ref, g_ref, b_ref, op_ref, os_ref, *, n_prompt):
    is_prompt = pl.program_id(0) < n_prompt

    def half_step(x_ref, o_ref):
        x = x_ref[...]
        xb = x.astype(BF16)
        acc = None
        for c in range(D_FF // FFN_CHUNK):
            sl = slice(c * FFN_CHUNK, (c + 1) * FFN_CHUNK)
            g = jnp.dot(xb, wg_ref[:, sl].astype(BF16), preferred_element_type=F32)
            u = jnp.dot(xb, wu_ref[:, sl].astype(BF16), preferred_element_type=F32)
            a = (_silu(g) * u).astype(BF16)
            d = jnp.dot(a, wd_ref[sl, :].astype(BF16), preferred_element_type=F32)
            acc = d if acc is None else acc + d
        o_ref[...] = _layer_norm(ALPHA * x + 0.5 * acc, g_ref[...], b_ref[...])

    pl.when(is_prompt)(functools.partial(half_step, xp_ref, op_ref))
    pl.when(jnp.logical_not(is_prompt))(functools.partial(half_step, xs_ref, os_ref))


def _ffn_ln(xp, xs, wg, wu, wd, layer, half, g, b):
    tm = FFN_ROWS
    assert xp.shape[0] % tm == 0 and xs.shape[0] % tm == 0
    n_p, n_s = xp.shape[0] // tm, xs.shape[0] // tm
    pspec = pl.BlockSpec((tm, D_MODEL), lambda i: (jnp.minimum(i, n_p - 1), 0))
    sspec = pl.BlockSpec((tm, D_MODEL), lambda i: (jnp.maximum(i - n_p, 0), 0))

    def wspec(w):
        return pl.BlockSpec((None, None) + w.shape[2:], lambda i: (layer, half, 0, 0),
                            pipeline_mode=pl.Buffered(1))

    return pl.pallas_call(
        functools.partial(_ffn_ln_kernel, n_prompt=n_p),
        grid=(n_p + n_s,),
        in_specs=[pspec, sspec, wspec(wg), wspec(wu), wspec(wd), _const_spec(g.shape), _const_spec(b.shape)],
        out_specs=[pspec, sspec],
        out_shape=[jax.ShapeDtypeStruct(xp.shape, F32), jax.ShapeDtypeStruct(xs.shape, F32)],
        compiler_params=_params("arbitrary"),
        name="ffn_ln",
    )(xp, xs, wg, wu, wd, g, b)


_C_QL, _C_CKV, _C_KPE, _C_KPS, _C_RQ, _C_RK, _C_RV, _C_RG, _C_END = (
    0, 512, 768, 896, 1024, 1536, 2048, 2560, 3072)


def _prep_kernel(h_ref, w_ref, qn_ref, kvn_ref, uq1_ref, uq2_ref, wk_ref, wvt_ref,
                 cq_ref, sq_ref, cr_ref, sr_ref,
                 q_ref, k_ref, vt_ref, ckv_ref, kpe_ref, rq_ref, rk_ref, rv_ref, rg_ref):
    xb = h_ref[...].astype(BF16)

    def proj(a, b):
        return jnp.dot(xb, w_ref[:, a:b], preferred_element_type=F32)

    cq = cq_ref[...]
    sq = sq_ref[...]
    ql = proj(_C_QL, _C_CKV)
    c = proj(_C_CKV, _C_KPE)
    kp = proj(_C_KPE, _C_RQ)
    rq = proj(_C_RQ, _C_RK)
    rk = proj(_C_RK, _C_RV)
    rv_ref[...] = proj(_C_RV, _C_RG).astype(BF16)
    rg_ref[...] = proj(_C_RG, _C_END)
    qn = (ql * lax.rsqrt(jnp.mean(ql * ql, axis=-1, keepdims=True) + RMS_EPS) * qn_ref[...]).astype(BF16)
    ckv = c * lax.rsqrt(jnp.mean(c * c, axis=-1, keepdims=True) + RMS_EPS) * kvn_ref[...]
    ckv_ref[...] = ckv
    cb = ckv.astype(BF16)
    kpe = kp[:, :HEAD_PAD] * cq + kp[:, HEAD_PAD:] * sq
    kpe_ref[...] = kpe
    qa = jnp.dot(qn, uq1_ref[...], preferred_element_type=F32)
    qb = jnp.dot(qn, uq2_ref[...], preferred_element_type=F32)
    kn = jnp.dot(cb, wk_ref[...], preferred_element_type=F32)
    vt = lax.dot_general(wvt_ref[...], cb, _NT, preferred_element_type=F32)
    for hh in range(MLA_HEADS):
        sl = slice(hh * HEAD_PAD, (hh + 1) * HEAD_PAD)
        q_ref[:, sl] = ((qa[:, sl] * cq + qb[:, sl] * sq) * ATT_SCALE).astype(BF16)
        k_ref[:, sl] = (kn[:, sl] + kpe).astype(BF16)
    head_row = lax.broadcasted_iota(jnp.int32, vt.shape, 0) % HEAD_PAD
    vt_ref[0] = jnp.where(head_row == V_DIM, 1.0, vt).astype(BF16)
    cr = cr_ref[...]
    sr = sr_ref[...]
    for hh in range(RET_HEADS):
        sl = slice(hh * RET_DK, (hh + 1) * RET_DK)
        xq = rq[:, sl]
        xk = rk[:, sl]
        rq_ref[:, sl] = xq * cr + pltpu.roll(xq, RET_DK // 2, 1) * sr
        rk_ref[:, sl] = (xk * cr + pltpu.roll(xk, RET_DK // 2, 1) * sr) * (RET_DK ** -0.5)


def _mixer_prep(h, mw, tabs, tm, vt_blk):
    m = h.shape[0]
    cq, sq, cr, sr = tabs
    tab_blocks = cq.shape[0] // tm
    per = vt_blk // tm

    def row(n):
        return pl.BlockSpec((tm, n), lambda i: (i, 0))

    tab = pl.BlockSpec((tm, LANES), lambda i: (i % tab_blocks, 0))
    consts = [mw["w_main"], mw["q_norm"], mw["kv_norm"], mw["uq1"], mw["uq2"], mw["wk"], mw["wv_t"]]
    out_shape = [
        jax.ShapeDtypeStruct((m, D_HEADS), BF16),
        jax.ShapeDtypeStruct((m, D_HEADS), BF16),
        jax.ShapeDtypeStruct((m // vt_blk, D_HEADS, vt_blk), BF16),
        jax.ShapeDtypeStruct((m, KV_LORA), F32),
        jax.ShapeDtypeStruct((m, LANES), F32),
        jax.ShapeDtypeStruct((m, D_RET), F32),
        jax.ShapeDtypeStruct((m, D_RET), F32),
        jax.ShapeDtypeStruct((m, D_RET), BF16),
        jax.ShapeDtypeStruct((m, D_RET), F32),
    ]
    return pl.pallas_call(
        _prep_kernel,
        grid=(m // tm,),
        in_specs=[row(D_MODEL)] + [_const_spec(c.shape) for c in consts] + [tab] * 4,
        out_specs=[pl.BlockSpec((1, D_HEADS, tm), lambda i: (i // per, 0, i % per)) if len(s.shape) == 3
                   else row(s.shape[1]) for s in out_shape],
        out_shape=out_shape,
        compiler_params=_params("parallel"),
        name="mixer_prep",
    )(h, *consts, cq, sq, cr, sr)


FLASH_HEADS = 2
FLASH_KEYS = 512


def _flash_kernel(q_ref, k_ref, vt_ref, o_ref, *, blk):
    i = pl.program_id(2)
    lanes = [slice(a * HEAD_PAD, (a + 1) * HEAD_PAD) for a in range(FLASH_HEADS)]
    qs = [q_ref[:, sl] for sl in lanes]

    def scores_t(j):
        rows = pl.ds(pl.multiple_of(j * blk, blk), blk)
        return tuple(lax.dot_general(k_ref[rows, sl], q, _NT, preferred_element_type=F32)
                     for q, sl in zip(qs, lanes))

    def update(st, j, m, acc, sl):
        m_new = jnp.maximum(m, jnp.max(st, axis=0, keepdims=True))
        acc = acc * jnp.exp2(m - m_new)
        for k0 in range(0, blk, FLASH_KEYS):
            p = jnp.exp2(st[k0:k0 + FLASH_KEYS] - m_new).astype(BF16)
            acc = acc + jnp.dot(vt_ref[j, sl, k0:k0 + FLASH_KEYS], p, preferred_element_type=F32)
        return m_new, acc

    def body(j, carry):
        ms, accs = carry
        new = [update(st, j, m, acc, sl) for st, m, acc, sl in zip(scores_t(j), ms, accs, lanes)]
        return tuple(n[0] for n in new), tuple(n[1] for n in new)

    m0 = tuple(jnp.full((1, blk), NEG, F32) for _ in lanes)
    acc0 = tuple(jnp.zeros((HEAD_PAD, blk), F32) for _ in lanes)
    ms, accs = lax.fori_loop(0, i, body, (m0, acc0))
    key = lax.broadcasted_iota(jnp.int32, (blk, blk), 0)
    qry = lax.broadcasted_iota(jnp.int32, (blk, blk), 1)
    outs = []
    for st, m, acc, sl in zip(scores_t(i), ms, accs, lanes):
        m, acc = update(jnp.where(key <= qry, st, NEG), i, m, acc, sl)
        outs.append((acc / acc[V_DIM:V_DIM + 1, :])[:V_DIM])
    o_ref[...] = jnp.concatenate(outs, axis=0).T.astype(BF16)


def _flash_attention(q, k, vt, batch, seq, blk):
    nq = seq // blk
    width = FLASH_HEADS * HEAD_PAD
    qspec = pl.BlockSpec((blk, width), lambda b, h, i: (b * nq + i, h))
    kspec = pl.BlockSpec((seq, width), lambda b, h, i: (b, h))
    vspec = pl.BlockSpec((nq, width, blk), lambda b, h, i: (b, h, 0))
    return pl.pallas_call(
        functools.partial(_flash_kernel, blk=blk),
        grid=(batch, MLA_HEADS // FLASH_HEADS, nq),
        in_specs=[qspec, kspec, vspec],
        out_specs=pl.BlockSpec((blk, FLASH_HEADS * V_DIM), lambda b, h, i: (b * nq + i, h)),
        out_shape=jax.ShapeDtypeStruct((batch * seq, D_ATT), BF16),
        compiler_params=_params("parallel", "parallel", "arbitrary"),
        name="flash_attention",
    )(q, k, vt)


T_PAD = 8
NEW_PAD = PAGE_SIZE
KEY_CHUNK = 1024


def _paged_kernel(pt_ref, q_ref, cn_ref, kn_ref, wq_ref, wuv_ref, ckv_hbm, kpe_hbm, o_ref,
                  ckv_buf, kpe_buf, kb_ref, s_ref, p_ref, sem, *, n_pages, n_new):
    b = pl.program_id(0)
    nb = pl.num_programs(0)
    past = n_pages * PAGE_SIZE
    slot = b % 2

    def page_copies(bb, sl, p):
        page = pt_ref[bb, p]
        rows = pl.ds(p * PAGE_SIZE, PAGE_SIZE)
        return (pltpu.make_async_copy(ckv_hbm.at[page], ckv_buf.at[sl, rows, :], sem.at[sl, 0]),
                pltpu.make_async_copy(kpe_hbm.at[page], kpe_buf.at[sl, :, rows], sem.at[sl, 1]))

    def start_fetch(bb, sl):
        for p in range(n_pages):
            for cp in page_copies(bb, sl, p):
                cp.start()

    def wait_fetch(bb, sl):
        for p in range(n_pages):
            for cp in page_copies(bb, sl, p):
                cp.wait()

    @pl.when(b == 0)
    def _():
        ckv_buf[:, past:, :] = jnp.zeros((2, NEW_PAD, KV_LORA), F32)
        start_fetch(0, 0)

    @pl.when(b + 1 < nb)
    def _():
        start_fetch(b + 1, 1 - slot)

    q8 = q_ref[0]
    qrep = jnp.concatenate([q8] * MLA_HEADS, axis=0)
    n_rows = MLA_HEADS * T_PAD
    row_h = lax.broadcasted_iota(jnp.int32, (n_rows, D_HEADS), 0) // T_PAD
    col_h = lax.broadcasted_iota(jnp.int32, (n_rows, D_HEADS), 1) // HEAD_PAD
    qm = jnp.where(row_h == col_h, qrep, jnp.zeros_like(qrep))
    ql = jnp.dot(qm, wq_ref[...], preferred_element_type=F32)
    q_lat = ql[:, :KV_LORA].astype(BF16)
    q_pe = ql[:, KV_LORA:KV_LORA + QK_ROPE].astype(BF16)

    wait_fetch(b, slot)
    ckv_buf[slot, past:past + T_PAD, :] = cn_ref[0]
    kpe_buf[slot, :, past:] = kn_ref[0]

    chunk = min(KEY_CHUNK, past)
    bounds = [(c * chunk, chunk) for c in range(past // chunk)] + [(past, NEW_PAD)]
    qt = lax.broadcasted_iota(jnp.int32, (n_rows, NEW_PAD), 0) % T_PAD
    kt = lax.broadcasted_iota(jnp.int32, (n_rows, NEW_PAD), 1)
    for r0, n in bounds:
        kb_ref[r0:r0 + n, :] = ckv_buf[slot, r0:r0 + n, :].astype(BF16)
    s_ref[...] = (lax.dot_general(q_lat, kb_ref[...], _NT, preferred_element_type=F32)
                  + jnp.dot(q_pe, kpe_buf[slot].astype(BF16), preferred_element_type=F32))
    s_ref[:, past:] = jnp.where((kt <= qt) & (kt < n_new), s_ref[:, past:], NEG)
    m = jnp.max(s_ref[...], axis=-1, keepdims=True)
    l = jnp.zeros((n_rows, 1), F32)
    for r0, n in bounds:
        p = jnp.exp2(s_ref[:, r0:r0 + n] - m)
        l = l + jnp.sum(p, axis=-1, keepdims=True)
        p_ref[:, r0:r0 + n] = p.astype(BF16)
    o = jnp.dot(p_ref[...], kb_ref[...], preferred_element_type=F32)
    o_lat = (o / l).astype(BF16)
    pv = jnp.dot(o_lat, wuv_ref[...], preferred_element_type=F32)
    out_row_h = lax.broadcasted_iota(jnp.int32, (n_rows, D_ATT), 0) // T_PAD
    out_col_h = lax.broadcasted_iota(jnp.int32, (n_rows, D_ATT), 1) // V_DIM
    pv = jnp.where(out_row_h == out_col_h, pv, 0.0)
    out = pv[0:T_PAD]
    for hh in range(1, MLA_HEADS):
        out = out + pv[hh * T_PAD:(hh + 1) * T_PAD]
    o_ref[0] = out.astype(BF16)


def _paged_attention(page_table, q8, ckv_new8, kpe_new_t, wq, wuv, cache_ckv, cache_kpe_t, n_new):
    nb, n_pages = page_table.shape
    rows = n_pages * PAGE_SIZE + NEW_PAD
    grid_spec = pltpu.PrefetchScalarGridSpec(
        num_scalar_prefetch=1,
        grid=(nb,),
        in_specs=[
            pl.BlockSpec((1, T_PAD, D_HEADS), lambda b, pt: (b, 0, 0)),
            pl.BlockSpec((1, T_PAD, KV_LORA), lambda b, pt: (b, 0, 0)),
            pl.BlockSpec((1, QK_ROPE, NEW_PAD), lambda b, pt: (b, 0, 0)),
            pl.BlockSpec(wq.shape, lambda b, pt: (0, 0)),
            pl.BlockSpec(wuv.shape, lambda b, pt: (0, 0)),
            pl.BlockSpec(memory_space=pl.ANY),
            pl.BlockSpec(memory_space=pl.ANY),
        ],
        out_specs=pl.BlockSpec((1, T_PAD, D_ATT), lambda b, pt: (b, 0, 0)),
        scratch_shapes=[
            pltpu.VMEM((2, rows, KV_LORA), F32),
            pltpu.VMEM((2, QK_ROPE, rows), F32),
            pltpu.VMEM((rows, KV_LORA), BF16),
            pltpu.VMEM((MLA_HEADS * T_PAD, rows), F32),
            pltpu.VMEM((MLA_HEADS * T_PAD, rows), BF16),
            pltpu.SemaphoreType.DMA((2, 2)),
        ],
    )
    return pl.pallas_call(
        functools.partial(_paged_kernel, n_pages=n_pages, n_new=n_new),
        grid_spec=grid_spec,
        out_shape=jax.ShapeDtypeStruct((nb, T_PAD, D_ATT), BF16),
        compiler_params=_params("arbitrary"),
        name="paged_attention",
    )(page_table, q8, ckv_new8, kpe_new_t, wq, wuv, cache_ckv, cache_kpe_t)


def _group_norm_gate(o, gate, gain, bias):
    mu = jnp.mean(o, axis=-1, keepdims=True)
    d = o - mu
    var = jnp.mean(d * d, axis=-1, keepdims=True)
    return _silu(gate) * (d * lax.rsqrt(var + LN_EPS) * gain + bias)


def _ret_kernel(rq_ref, rk_ref, rv_ref, rg_ref, din_ref, qd_ref, kd_ref, gc_ref, gg_ref, gb_ref, s0_ref,
                ro_ref, so_ref, s_ref, *, chunks):
    i = pl.program_id(1)

    @pl.when(i == 0)
    def _():
        s_ref[...] = s0_ref[0]

    units = [(c, hh) for c in range(chunks) for hh in range(RET_HEADS)]

    def tile(ref, c, hh):
        return ref[c * RET_CHUNK:(c + 1) * RET_CHUNK, hh * RET_DK:(hh + 1) * RET_DK]

    def lanes(ref, hh):
        return ref[:, hh * RET_DK:(hh + 1) * RET_DK]

    inner = {u: (lax.dot_general(tile(rq_ref, *u).astype(BF16), tile(rk_ref, *u).astype(BF16), _NT,
                                 preferred_element_type=F32) * lanes(din_ref, u[1])).astype(BF16)
             for u in units}
    kv = {u: lax.dot_general((tile(rk_ref, *u) * lanes(kd_ref, u[1])).astype(BF16), tile(rv_ref, *u), _TN,
                             preferred_element_type=F32)
          for u in units}
    o_intra = {u: jnp.dot(inner[u], tile(rv_ref, *u), preferred_element_type=F32) for u in units}
    states = [s_ref[hh] for hh in range(RET_HEADS)]
    for c, hh in units:
        s = states[hh]
        o = o_intra[c, hh] + jnp.dot((tile(rq_ref, c, hh) * lanes(qd_ref, hh)).astype(BF16), s.astype(BF16),
                                     preferred_element_type=F32)
        states[hh] = s * lanes(gc_ref, hh) + kv[c, hh]
        ro_ref[c * RET_CHUNK:(c + 1) * RET_CHUNK, hh * RET_DK:(hh + 1) * RET_DK] = _group_norm_gate(
            o, tile(rg_ref, c, hh), lanes(gg_ref, hh), lanes(gb_ref, hh)).astype(BF16)
    for hh in range(RET_HEADS):
        s_ref[hh] = states[hh]

    @pl.when(i == pl.num_programs(1) - 1)
    def _():
        so_ref[0] = s_ref[...]


def _retention_prompt(rq, rk, rv, rg, dec, gn_gain, gn_bias, state0, batch, seq, chunks):
    rows = chunks * RET_CHUNK
    steps = seq // rows
    rspec = pl.BlockSpec((rows, D_RET), lambda b, i: (b * steps + i, 0))
    sspec = pl.BlockSpec((1, RET_HEADS, RET_DK, RET_DV), lambda b, i: (b, 0, 0, 0))
    consts = [dec["din"], dec["qdec"], dec["kdec"], dec["gc"], gn_gain, gn_bias]
    return pl.pallas_call(
        functools.partial(_ret_kernel, chunks=chunks),
        grid=(batch, steps),
        in_specs=[rspec] * 4 + [_const_spec(c.shape) for c in consts] + [sspec],
        out_specs=[rspec, sspec],
        out_shape=[jax.ShapeDtypeStruct((batch * seq, D_RET), BF16),
                   jax.ShapeDtypeStruct((batch, RET_HEADS, RET_DK, RET_DV), F32)],
        scratch_shapes=[pltpu.VMEM((RET_HEADS, RET_DK, RET_DV), F32)],
        compiler_params=_params("parallel", "arbitrary"),
        name="retention_prompt",
    )(rq, rk, rv, rg, *consts, state0)


RS_BATCH = 8


def _ret_sample_kernel(q_ref, k_ref, v_ref, rg_ref, kt_ref, qd_ref, kd_ref, din_ref, gc_ref, gg_ref, gb_ref,
                       s0_ref, ro_ref, so_ref, *, n_tok):
    def one_seq(bi, carry):
        q8 = q_ref[bi]
        k8 = k_ref[bi]
        v8 = v_ref[bi]
        g8 = rg_ref[bi]
        qs = (q8 * qd_ref[...]).astype(BF16)
        heads = [slice(hh * RET_DK, (hh + 1) * RET_DK) for hh in range(RET_HEADS)]
        cross = [jnp.dot(qs[:, sl], s0_ref[bi, hh].astype(BF16), preferred_element_type=F32)
                 for hh, sl in enumerate(heads)]
        vbd = jnp.concatenate([v8 * kd_ref[hh] for hh in range(RET_HEADS)]
                              + [jnp.zeros((LANES - RET_HEADS * T_PAD, D_RET), F32)], axis=0)
        upd = jnp.dot(kt_ref[bi].astype(BF16), vbd.astype(BF16), preferred_element_type=F32)
        outs = []
        for hh, sl in enumerate(heads):
            o = cross[hh]
            din = din_ref[hh]
            for m in range(n_tok):
                a = jnp.sum(q8[:, sl] * k8[m:m + 1, sl], axis=-1, keepdims=True) * din[:, m:m + 1]
                o = o + a * v8[m:m + 1, sl]
            outs.append(_group_norm_gate(o, g8[:, sl], gg_ref[:, sl], gb_ref[:, sl]))
            so_ref[bi, hh] = s0_ref[bi, hh] * gc_ref[:, sl] + upd[:, sl]
        ro_ref[bi] = jnp.concatenate(outs, axis=1).astype(BF16)
        return carry

    lax.fori_loop(0, RS_BATCH, one_seq, 0)


def _retention_sample(q8, k8, v8, rg8, k_t, dec, gn_gain, gn_bias, state0, n_tok):
    nb = q8.shape[0]
    rspec = pl.BlockSpec((RS_BATCH, T_PAD, D_RET), lambda i: (i, 0, 0))
    sspec = pl.BlockSpec((RS_BATCH, RET_HEADS, RET_DK, RET_DV), lambda i: (i, 0, 0, 0))
    consts = [dec["qdec"], dec["kdec"], dec["din"], dec["gc"], gn_gain, gn_bias]
    return pl.pallas_call(
        functools.partial(_ret_sample_kernel, n_tok=n_tok),
        grid=(nb // RS_BATCH,),
        in_specs=[rspec] * 4 + [pl.BlockSpec((RS_BATCH, RET_DK, LANES), lambda i: (i, 0, 0))]
                 + [_const_spec(c.shape) for c in consts] + [sspec],
        out_specs=[rspec, sspec],
        out_shape=[jax.ShapeDtypeStruct((nb, T_PAD, D_RET), BF16),
                   jax.ShapeDtypeStruct((nb, RET_HEADS, RET_DK, RET_DV), F32)],
        compiler_params=_params("parallel"),
        name="retention_sample",
    )(q8, k8, v8, rg8, k_t, *consts, state0)


def _outproj_ln_kernel(a_ref, ro_ref, h_ref, wa_ref, wr_ref, g_ref, b_ref, o_ref):
    y = (jnp.dot(a_ref[...], wa_ref[...], preferred_element_type=F32)
         + jnp.dot(ro_ref[...], wr_ref[...], preferred_element_type=F32))
    o_ref[...] = _layer_norm(ALPHA * h_ref[...] + y, g_ref[...], b_ref[...])


def _outproj_ln(a, ro, h, wa, wr, g, b, tm):
    m = h.shape[0]

    def row(n):
        return pl.BlockSpec((tm, n), lambda i: (i, 0))

    return pl.pallas_call(
        _outproj_ln_kernel,
        grid=(m // tm,),
        in_specs=[row(D_ATT), row(D_RET), row(D_MODEL), _const_spec(wa.shape), _const_spec(wr.shape),
                  _const_spec(g.shape), _const_spec(b.shape)],
        out_specs=row(D_MODEL),
        out_shape=jax.ShapeDtypeStruct((m, D_MODEL), F32),
        compiler_params=_params("parallel"),
        name="outproj_ln",
    )(a, ro, h, wa, wr, g, b)


HALO = 16


def _pool_tail(pooled_groups, x, pw_ref, pb_ref, ps_ref, g_ref, b_ref):
    ys = [jnp.dot(p.astype(BF16), pw_ref[gi], preferred_element_type=F32) for gi, p in enumerate(pooled_groups)]
    y = (jnp.concatenate(ys, axis=-1) + pb_ref[...]) * ps_ref[...]
    return _layer_norm(ALPHA * x + y, g_ref[...], b_ref[...])


HALO_P = 32


def _pool_prompt_kernel(h_ref, halo_ref, pre_ref, pw_ref, pb_ref, ps_ref, g_ref, b_ref, o_ref,
                        xs_ref, a_ref, b2_ref, *, tm, tiles, start):
    t = pl.program_id(0) % tiles
    x = h_ref[...]
    xs_ref[0:HALO_P, :] = jnp.where(t == 0, pre_ref[0], halo_ref[...])
    xs_ref[HALO_P:, :] = x
    n = HALO_P + tm
    pos = start + t * tm + lax.broadcasted_iota(jnp.int32, (tm, 1), 0)
    pooled = []
    src = xs_ref
    for k, wl in enumerate(POOL_WINDOWS, start=1):
        lo = (k - 1) * POOL_GROUP
        r0, shift = 8 * k, wl // 2
        level = src[r0:n, lo:] + src[r0 - shift:n - shift, lo:]
        cnt = jnp.minimum(pos + 1, wl).astype(F32)
        pooled.append(level[HALO_P - r0:, :POOL_GROUP] / cnt - x[:, lo:lo + POOL_GROUP])
        if k < len(POOL_WINDOWS):
            dst = a_ref if k % 2 else b2_ref
            dst[r0:n, lo + POOL_GROUP:] = level[:, POOL_GROUP:]
            src = dst
    o_ref[...] = _pool_tail(pooled, x, pw_ref, pb_ref, ps_ref, g_ref, b_ref)


def _pool_prompt(h, prefix, pw, pb, ps, g, b, batch, seq, start, tm):
    assert all(w == 2 ** (i + 1) for i, w in enumerate(POOL_WINDOWS)) and 8 * len(POOL_WINDOWS) <= HALO_P
    tiles = seq // tm
    per = tm // HALO_P
    consts = [pw, pb, ps, g, b]
    buf = pltpu.VMEM((HALO_P + tm, D_MODEL), F32)
    return pl.pallas_call(
        functools.partial(_pool_prompt_kernel, tm=tm, tiles=tiles, start=start),
        grid=(batch * tiles,),
        in_specs=[pl.BlockSpec((tm, D_MODEL), lambda i: (i, 0)),
                  pl.BlockSpec((HALO_P, D_MODEL), lambda i: (jnp.maximum(i * per - 1, 0), 0)),
                  pl.BlockSpec((1, HALO_P, D_MODEL), lambda i: (i // tiles, 0, 0))]
                 + [_const_spec(c.shape) for c in consts],
        out_specs=pl.BlockSpec((tm, D_MODEL), lambda i: (i, 0)),
        out_shape=jax.ShapeDtypeStruct((batch * seq, D_MODEL), F32),
        scratch_shapes=[buf, buf, buf],
        compiler_params=_params("parallel"),
        name="pool_prompt",
    )(h, h, prefix, *consts)


def _pool_sample_kernel(xs_ref, pw_ref, pb_ref, ps_ref, g_ref, b_ref, o_ref, *, n_tok, start):
    nb = xs_ref.shape[1]
    x = jnp.concatenate([xs_ref[HALO + t] for t in range(n_tok)], axis=0)
    pooled = []
    for gi, wl in enumerate(POOL_WINDOWS):
        sl = slice(gi * POOL_GROUP, (gi + 1) * POOL_GROUP)
        parts = []
        for t in range(n_tok):
            acc = xs_ref[HALO + t, :, sl]
            for d in range(1, wl):
                acc = acc + xs_ref[HALO + t - d, :, sl]
            parts.append(acc / float(min(start + t + 1, wl)))
        pooled.append(jnp.concatenate(parts, axis=0) - x[:, sl])
    y = _pool_tail(pooled, x, pw_ref, pb_ref, ps_ref, g_ref, b_ref)
    for t in range(n_tok):
        o_ref[t] = y[t * nb:(t + 1) * nb]


def _pool_sample(xs_t, pw, pb, ps, g, b, n_tok, start):
    nb = xs_t.shape[1]
    args = [xs_t, pw, pb, ps, g, b]
    return pl.pallas_call(
        functools.partial(_pool_sample_kernel, n_tok=n_tok, start=start),
        grid=(1,),
        in_specs=[_const_spec(a.shape) for a in args],
        out_specs=_const_spec((n_tok, nb, D_MODEL)),
        out_shape=jax.ShapeDtypeStruct((n_tok, nb, D_MODEL), F32),
        compiler_params=_params("arbitrary"),
        name="pool_sample",
    )(*args)


def _head_pad(w, width):
    r, nh, d = w.shape
    out = jnp.zeros((r, nh, HEAD_PAD), w.dtype).at[:, :, :d].set(w)
    return out.reshape(r, nh * HEAD_PAD)[:, :width]


def _mixer_weights(mix_w_in, q_norm, kv_norm, w_uq, w_uk, w_uv, mix_w_out):
    offs = np.concatenate([[0], np.cumsum(SPLIT_SIZES)])
    wq, wckv, wkpe, wrq, wrk, wrv, wrg = [mix_w_in[:, offs[i]:offs[i + 1]] for i in range(7)]
    half = QK_ROPE // 2
    z_lo = jnp.zeros((D_MODEL, QK_NOPE), F32)
    z_hi = jnp.zeros((D_MODEL, HEAD_PAD - QK_NOPE - QK_ROPE), F32)
    kpe_blk = jnp.concatenate([z_lo, wkpe, z_hi], axis=1)
    kpe_swp = jnp.concatenate([z_lo, -wkpe[:, half:], wkpe[:, :half], z_hi], axis=1)
    w_main = jnp.concatenate([wq, wckv, kpe_blk, kpe_swp, wrq, wrk, wrv, wrg], axis=1).astype(BF16)
    pe = w_uq[:, :, QK_NOPE:]
    uq1 = _head_pad(w_uq, D_HEADS)
    uq2 = _head_pad(jnp.concatenate([jnp.zeros_like(w_uq[:, :, :QK_NOPE]), -pe[:, :, half:], pe[:, :, :half]], axis=2),
                    D_HEADS)
    wk = _head_pad(w_uk, D_HEADS)
    wv = _head_pad(w_uv, D_HEADS)
    wa = mix_w_out[:D_ATT]
    wr = mix_w_out[MLA_HEADS * V_DIM:]
    wuk_t = jnp.zeros((MLA_HEADS, HEAD_PAD, KV_LORA), F32).at[:, :QK_NOPE, :].set(
        jnp.transpose(w_uk, (1, 2, 0))).reshape(D_HEADS, KV_LORA)
    r = np.arange(D_HEADS) % HEAD_PAD
    e_pe = ((r[:, None] - QK_NOPE) == np.arange(LANES)[None, :]) & (r[:, None] >= QK_NOPE) & (r[:, None] < QK_NOPE + QK_ROPE)
    wq_abs = jnp.concatenate([wuk_t, jnp.asarray(e_pe.astype(np.float32))], axis=1)
    return {
        "w_main": w_main, "q_norm": q_norm[None, :], "kv_norm": kv_norm[None, :],
        "uq1": uq1.astype(BF16), "uq2": uq2.astype(BF16), "wk": wk.astype(BF16),
        "wv_c": w_uv.reshape(KV_LORA, D_ATT).astype(BF16),
        "wv_t": wv.T.astype(BF16), "wa": wa.astype(BF16), "wr": wr.astype(BF16),
        "wq_abs": wq_abs.astype(BF16),
    }


def _rope_tables(pos):
    def angles(r):
        inv = 1.0 / (ROPE_BASE ** (np.arange(0, r, 2, dtype=np.float64) / r))
        return pos.astype(np.float64)[:, None] * inv[None, :]

    n = pos.shape[0]
    a = angles(QK_ROPE)
    c, s = np.cos(a), np.sin(a)
    hi = HEAD_PAD - QK_NOPE - QK_ROPE
    cq = np.concatenate([np.ones((n, QK_NOPE)), c, c, np.ones((n, hi))], axis=1)
    sq = np.concatenate([np.zeros((n, QK_NOPE)), s, s, np.zeros((n, hi))], axis=1)
    a = angles(RET_DK)
    c, s = np.cos(a), np.sin(a)
    tabs = (cq, sq, np.concatenate([c, c], axis=1), np.concatenate([-s, s], axis=1))
    return tuple(jnp.asarray(t, F32) for t in tabs)


def _ret_log_decay():
    return jnp.log(1.0 - 2.0 ** (-5.0 - jnp.arange(RET_HEADS, dtype=F32)))


def _ret_decay_tables(chunk):
    log_g = _ret_log_decay()
    idx = jnp.arange(chunk, dtype=F32)
    diff = idx[:, None] - idx[None, :]
    d_in = jnp.where(diff >= 0, jnp.exp(jnp.maximum(diff, 0.0)[None] * log_g[:, None, None]), 0.0)
    q_dec = jnp.exp((idx + 1.0)[None, :] * log_g[:, None])
    k_dec = jnp.exp((chunk - 1.0 - idx)[None, :] * log_g[:, None])
    g_c = jnp.exp(chunk * log_g)
    lanes = (chunk, RET_HEADS * RET_DK)
    return {
        "din": jnp.transpose(d_in, (1, 0, 2)).reshape(chunk, RET_HEADS * chunk),
        "qdec": jnp.broadcast_to(q_dec.T[:, :, None], (chunk, RET_HEADS, RET_DK)).reshape(lanes),
        "kdec": jnp.broadcast_to(k_dec.T[:, :, None], (chunk, RET_HEADS, RET_DK)).reshape(lanes),
        "gc": jnp.broadcast_to(g_c[:, None], (RET_HEADS, RET_DV)).reshape(1, RET_HEADS * RET_DV),
    }


def _ret_sample_tables(n_tok):
    log_g = _ret_log_decay()
    idx = jnp.arange(n_tok, dtype=F32)
    diff = idx[:, None] - idx[None, :]
    d_in = jnp.where(diff >= 0, jnp.exp(jnp.maximum(diff, 0.0)[None] * log_g[:, None, None]), 0.0)
    q_dec = jnp.exp((idx + 1.0)[None, :] * log_g[:, None])
    k_dec = jnp.exp((n_tok - 1.0 - idx)[None, :] * log_g[:, None])
    g_c = jnp.exp(n_tok * log_g)
    pad_t = T_PAD - n_tok
    own_lanes = (np.arange(D_RET) // RET_DV)[None, None, :] == np.arange(RET_HEADS)[:, None, None]
    return {
        "qdec": jnp.pad(jnp.broadcast_to(q_dec.T[:, :, None], (n_tok, RET_HEADS, RET_DK)).reshape(n_tok, D_RET),
                        ((0, pad_t), (0, 0))),
        "kdec": jnp.pad(k_dec[:, :, None] * jnp.asarray(own_lanes, F32), ((0, 0), (0, pad_t), (0, 0))),
        "din": jnp.pad(d_in, ((0, 0), (0, pad_t), (0, LANES - n_tok))),
        "gc": jnp.broadcast_to(g_c[:, None], (RET_HEADS, RET_DV)).reshape(1, D_RET),
    }


FFN_ORDER = ((0, 0), (0, 1), (1, 0), (1, 1))


def _trunk(x, start, ret_state0, pool_prefix, mla_cache, w, mw):
    batch, seq, _ = x.shape
    m = batch * seq
    prompt = mla_cache is None
    tm = min(512, m)
    h = x.reshape(m, D_MODEL)

    h = yield h
    pos = start + (np.arange(seq) if prompt else np.arange(m) % seq)
    flash_blk = min(1024, seq) if prompt else tm
    q, k, vt, ckv, kpe, rq, rk, rv, rg = _mixer_prep(h, mw, _rope_tables(pos), tm, flash_blk)
    kpe = kpe[:, QK_NOPE:QK_NOPE + QK_ROPE]
    gn_gain = w["ret_gn_gain"][0][None, :]
    gn_bias = w["ret_gn_bias"][0][None, :]
    if prompt:
        a = _flash_attention(q, k, vt, batch, seq, flash_blk)
        chunk = RET_CHUNK if seq % RET_CHUNK == 0 else seq
        assert chunk == RET_CHUNK
        ro, ret_state = _retention_prompt(rq, rk, rv, rg, _ret_decay_tables(chunk), gn_gain, gn_bias,
                                          ret_state0, batch, seq, 8)
    else:
        assert seq <= T_PAD and seq % RET_CHUNK != 0
        cache_ckv, cache_kpe, page_table = mla_cache
        pad_t = ((0, 0), (0, T_PAD - seq), (0, 0))
        q8 = jnp.pad(q.reshape(batch, seq, D_HEADS), pad_t)
        cn8 = jnp.pad(ckv.reshape(batch, seq, KV_LORA), pad_t)
        kn_t = jnp.pad(jnp.swapaxes(kpe.reshape(batch, seq, QK_ROPE), 1, 2), ((0, 0), (0, 0), (0, NEW_PAD - seq)))
        a8 = _paged_attention(page_table, q8, cn8, kn_t, mw["wq_abs"], mw["wv_c"], cache_ckv,
                              jnp.swapaxes(cache_kpe, 1, 2), seq)
        a = a8[:, :seq].reshape(m, D_ATT)

        def rows8(t):
            return jnp.pad(t.reshape(batch, seq, D_RET), pad_t)

        k8 = rows8(rk)
        k_t = jnp.transpose(k8.reshape(batch, T_PAD, RET_HEADS, RET_DK), (0, 3, 2, 1)).reshape(
            batch, RET_DK, RET_HEADS * T_PAD)
        k_t = jnp.pad(k_t, ((0, 0), (0, 0), (0, LANES - RET_HEADS * T_PAD)))
        ro8, ret_state = _retention_sample(rows8(rq), k8, rows8(rv.astype(F32)), rows8(rg), k_t,
                                           _ret_sample_tables(seq), gn_gain, gn_bias, ret_state0, seq)
        ro = ro8[:, :seq].reshape(m, D_RET)
    h = _outproj_ln(a, ro, h, mw["wa"], mw["wr"], w["ln_gain"][0, 1][None, :], w["ln_bias"][0, 1][None, :], tm)
    h = yield h

    h = yield h
    xp_tail = jnp.concatenate([pool_prefix, h.reshape(batch, seq, D_MODEL)], axis=1)[:, -POOL_PREFIX:]
    halo = HALO_P if prompt else HALO
    prefix16 = jnp.pad(pool_prefix, ((0, 0), (halo - POOL_PREFIX, 0), (0, 0)))
    pool_args = (w["pool_w"], w["pool_b"][0][None, :], w["pool_scale"][0][None, :],
                 w["ln_gain"][1, 1][None, :], w["ln_bias"][1, 1][None, :])
    if prompt:
        h = _pool_prompt(h, prefix16, *pool_args, batch, seq, start, tm)
    else:
        xs_t = jnp.transpose(jnp.concatenate([prefix16, h.reshape(batch, seq, D_MODEL)], axis=1), (1, 0, 2))
        h = jnp.transpose(_pool_sample(xs_t, *pool_args, seq, start), (1, 0, 2)).reshape(m, D_MODEL)
    h = yield h
    return (h.reshape(batch, seq, D_MODEL), ckv.reshape(1, batch, seq, KV_LORA),
            kpe.reshape(1, batch, seq, QK_ROPE), ret_state[None], xp_tail[None])


def _finish(trunk, last):
    try:
        trunk.send(last)
    except StopIteration as done:
        return done.value
    raise AssertionError("trunk yielded more FFN requests than FFN_ORDER")


def kernel(x_prompt, x_sample, cache_mla_ckv, cache_mla_kpe, state_ret, state_pool, page_table, ffn_w_gate, ffn_w_up, ffn_w_down, ln_gain, ln_bias, mix_w_in, mla_q_norm, mla_kv_norm, mla_w_uq, mla_w_uk, mla_w_uv, ret_gn_gain, ret_gn_bias, mix_w_out, pool_w, pool_b, pool_scale):
    assert DEPTH == 2 and mix_w_in.shape[0] == 1 and pool_w.shape[0] == 1
    w = {
        "wg": ffn_w_gate, "wu": ffn_w_up, "wd": ffn_w_down,
        "ln_gain": ln_gain, "ln_bias": ln_bias, "ret_gn_gain": ret_gn_gain, "ret_gn_bias": ret_gn_bias,
        "pool_w": pool_w[0].astype(BF16), "pool_b": pool_b, "pool_scale": pool_scale,
    }
    mw = _mixer_weights(mix_w_in[0], mla_q_norm[0], mla_kv_norm[0], mla_w_uq[0], mla_w_uk[0], mla_w_uv[0],
                        mix_w_out[0])
    bp = x_prompt.shape[0]
    zero_ret = jnp.zeros((bp, RET_HEADS, RET_DK, RET_DV), F32)
    zero_pool = jnp.zeros((bp, POOL_PREFIX, D_MODEL), x_prompt.dtype)
    trunk_p = _trunk(x_prompt, 0, zero_ret, zero_pool, None, w, mw)
    trunk_s = _trunk(x_sample, PAST_LEN, state_ret[0], state_pool[0],
                     (cache_mla_ckv[0], cache_mla_kpe[0], page_table), w, mw)
    h_p, h_s = next(trunk_p), next(trunk_s)
    for n, (layer, half) in enumerate(FFN_ORDER):
        f_p, f_s = _ffn_ln(h_p, h_s, w["wg"], w["wu"], w["wd"], layer, half,
                           ln_gain[layer, 2 * half][None, :], ln_bias[layer, 2 * half][None, :])
        if n + 1 < len(FFN_ORDER):
            h_p, h_s = trunk_p.send(f_p), trunk_s.send(f_s)
    y_p, ckv_p, kpe_p, ret_p, pool_p = _finish(trunk_p, f_p)
    y_s, ckv_s, kpe_s, ret_s, pool_s = _finish(trunk_s, f_s)
    return (y_p, y_s, ckv_p, kpe_p, ckv_s, kpe_s, ret_p, ret_s, pool_p, pool_s)
```

```python
import functools

import numpy as np
import jax
import jax.numpy as jnp
from jax import lax
from jax.experimental import pallas as pl
from jax.experimental.pallas import tpu as pltpu

F32 = jnp.float32
BF16 = jnp.bfloat16

D_MODEL = 1024
DEPTH = 2
PAST_LEN = 8192
PAGE_SIZE = 128
ALPHA = (2 * DEPTH) ** 0.25
D_FF = 2816
MLA_HEADS = 8
Q_LORA = 512
KV_LORA = 256
QK_NOPE = 64
QK_ROPE = 32
V_DIM = 64
RET_HEADS = 4
RET_DK = 128
RET_DV = 128
RET_CHUNK = 128
POOL_WINDOWS = (2, 4, 8, 16)
POOL_GROUPS = 4
POOL_GROUP = D_MODEL // POOL_GROUPS
POOL_PREFIX = 15
ROPE_BASE = 10000.0
LN_EPS = 1e-5
RMS_EPS = 1e-6
SPLIT_SIZES = (Q_LORA, KV_LORA, QK_ROPE, RET_HEADS * RET_DK, RET_HEADS * RET_DK,
               RET_HEADS * RET_DV, RET_HEADS * RET_DV)
ATT_SCALE = (QK_NOPE + QK_ROPE) ** -0.5 * 1.4426950408889634

LANES = 128
HEAD_PAD = LANES
D_HEADS = MLA_HEADS * HEAD_PAD
D_ATT = MLA_HEADS * V_DIM
D_RET = RET_HEADS * RET_DV
NEG = -1e30
VMEM_LIMIT = 56 * 1024 * 1024

_NT = (((1,), (1,)), ((), ()))
_TN = (((0,), (0,)), ((), ()))


def _params(*sem):
    return pltpu.CompilerParams(dimension_semantics=sem, vmem_limit_bytes=VMEM_LIMIT)


def _const_spec(shape):
    nd = len(shape)
    return pl.BlockSpec(shape, lambda *_: (0,) * nd, pipeline_mode=pl.Buffered(1))


def _layer_norm(y, g, b):
    mu = jnp.mean(y, axis=-1, keepdims=True)
    d = y - mu
    var = jnp.mean(d * d, axis=-1, keepdims=True)
    return d * lax.rsqrt(var + LN_EPS) * g + b


def _silu(x):
    return x * jax.nn.sigmoid(x)


FFN_CHUNK = 256
FFN_ROWS = 512


def _ffn_ln_kernel(xp_ref, xs_ref, wg_ref, wu_ref, wd_ref, g_ref, b_ref, op_ref, os_ref, *, n_prompt):
    is_prompt = pl.program_id(0) < n_prompt

    def half_step(x_ref, o_ref):
        x = x_ref[...]
        xb = x.astype(BF16)
        acc = None
        for c in range(D_FF // FFN_CHUNK):
            sl = slice(c * FFN_CHUNK, (c + 1) * FFN_CHUNK)
            g = jnp.dot(xb, wg_ref[:, sl].astype(BF16), preferred_element_type=F32)
            u = jnp.dot(xb, wu_ref[:, sl].astype(BF16), preferred_element_type=F32)
            a = (_silu(g) * u).astype(BF16)
            d = jnp.dot(a, wd_ref[sl, :].astype(BF16), preferred_element_type=F32)
            acc = d if acc is None else acc + d
        o_ref[...] = _layer_norm(ALPHA * x + 0.5 * acc, g_ref[...], b_ref[...])

    pl.when(is_prompt)(functools.partial(half_step, xp_ref, op_ref))
    pl.when(jnp.logical_not(is_prompt))(functools.partial(half_step, xs_ref, os_ref))


def _ffn_ln(xp, xs, wg, wu, wd, layer, half, g, b):
    tm = FFN_ROWS
    assert xp.shape[0] % tm == 0 and xs.shape[0] % tm == 0
    n_p, n_s = xp.shape[0] // tm, xs.shape[0] // tm
    pspec = pl.BlockSpec((tm, D_MODEL), lambda i: (jnp.minimum(i, n_p - 1), 0))
    sspec = pl.BlockSpec((tm, D_MODEL), lambda i: (jnp.maximum(i - n_p, 0), 0))

    def wspec(w):
        return pl.BlockSpec((None, None) + w.shape[2:], lambda i: (layer, half, 0, 0),
                            pipeline_mode=pl.Buffered(1))

    return pl.pallas_call(
        functools.partial(_ffn_ln_kernel, n_prompt=n_p),
        grid=(n_p + n_s,),
        in_specs=[pspec, sspec, wspec(wg), wspec(wu), wspec(wd), _const_spec(g.shape), _const_spec(b.shape)],
        out_specs=[pspec, sspec],
        out_shape=[jax.ShapeDtypeStruct(xp.shape, F32), jax.ShapeDtypeStruct(xs.shape, F32)],
        compiler_params=_params("arbitrary"),
        name="ffn_ln",
    )(xp, xs, wg, wu, wd, g, b)


_C_QL, _C_CKV, _C_KPE, _C_KPS, _C_RQ, _C_RK, _C_RV, _C_RG, _C_END = (
    0, 512, 768, 896, 1024, 1536, 2048, 2560, 3072)


def _prep_kernel(h_ref, w_ref, qn_ref, kvn_ref, uq1_ref, uq2_ref, wk_ref, wvt_ref,
                 cq_ref, sq_ref, cr_ref, sr_ref,
                 q_ref, k_ref, vt_ref, ckv_ref, kpe_ref, rq_ref, rk_ref, rv_ref, rg_ref):
    xb = h_ref[...].astype(BF16)

    def proj(a, b):
        return jnp.dot(xb, w_ref[:, a:b], preferred_element_type=F32)

    cq = cq_ref[...]
    sq = sq_ref[...]
    ql = proj(_C_QL, _C_CKV)
    c = proj(_C_CKV, _C_KPE)
    kp = proj(_C_KPE, _C_RQ)
    rq = proj(_C_RQ, _C_RK)
    rk = proj(_C_RK, _C_RV)
    rv_ref[...] = proj(_C_RV, _C_RG).astype(BF16)
    rg_ref[...] = proj(_C_RG, _C_END)
    qn = (ql * lax.rsqrt(jnp.mean(ql * ql, axis=-1, keepdims=True) + RMS_EPS) * qn_ref[...]).astype(BF16)
    ckv = c * lax.rsqrt(jnp.mean(c * c, axis=-1, keepdims=True) + RMS_EPS) * kvn_ref[...]
    ckv_ref[...] = ckv
    cb = ckv.astype(BF16)
    kpe = kp[:, :HEAD_PAD] * cq + kp[:, HEAD_PAD:] * sq
    kpe_ref[...] = kpe
    qa = jnp.dot(qn, uq1_ref[...], preferred_element_type=F32)
    qb = jnp.dot(qn, uq2_ref[...], preferred_element_type=F32)
    kn = jnp.dot(cb, wk_ref[...], preferred_element_type=F32)
    vt = lax.dot_general(wvt_ref[...], cb, _NT, preferred_element_type=F32)
    for hh in range(MLA_HEADS):
        sl = slice(hh * HEAD_PAD, (hh + 1) * HEAD_PAD)
        q_ref[:, sl] = ((qa[:, sl] * cq + qb[:, sl] * sq) * ATT_SCALE).astype(BF16)
        k_ref[:, sl] = (kn[:, sl] + kpe).astype(BF16)
    head_row = lax.broadcasted_iota(jnp.int32, vt.shape, 0) % HEAD_PAD
    vt_ref[0] = jnp.where(head_row == V_DIM, 1.0, vt).astype(BF16)
    cr = cr_ref[...]
    sr = sr_ref[...]
    for hh in range(RET_HEADS):
        sl = slice(hh * RET_DK, (hh + 1) * RET_DK)
        xq = rq[:, sl]
        xk = rk[:, sl]
        rq_ref[:, sl] = xq * cr + pltpu.roll(xq, RET_DK // 2, 1) * sr
        rk_ref[:, sl] = (xk * cr + pltpu.roll(xk, RET_DK // 2, 1) * sr) * (RET_DK ** -0.5)


def _mixer_prep(h, mw, tabs, tm, vt_blk):
    m = h.shape[0]
    cq, sq, cr, sr = tabs
    tab_blocks = cq.shape[0] // tm
    per = vt_blk // tm

    def row(n):
        return pl.BlockSpec((tm, n), lambda i: (i, 0))

    tab = pl.BlockSpec((tm, LANES), lambda i: (i % tab_blocks, 0))
    consts = [mw["w_main"], mw["q_norm"], mw["kv_norm"], mw["uq1"], mw["uq2"], mw["wk"], mw["wv_t"]]
    out_shape = [
        jax.ShapeDtypeStruct((m, D_HEADS), BF16),
        jax.ShapeDtypeStruct((m, D_HEADS), BF16),
        jax.ShapeDtypeStruct((m // vt_blk, D_HEADS, vt_blk), BF16),
        jax.ShapeDtypeStruct((m, KV_LORA), F32),
        jax.ShapeDtypeStruct((m, LANES), F32),
        jax.ShapeDtypeStruct((m, D_RET), F32),
        jax.ShapeDtypeStruct((m, D_RET), F32),
        jax.ShapeDtypeStruct((m, D_RET), BF16),
        jax.ShapeDtypeStruct((m, D_RET), F32),
    ]
    return pl.pallas_call(
        _prep_kernel,
        grid=(m // tm,),
        in_specs=[row(D_MODEL)] + [_const_spec(c.shape) for c in consts] + [tab] * 4,
        out_specs=[pl.BlockSpec((1, D_HEADS, tm), lambda i: (i // per, 0, i % per)) if len(s.shape) == 3
                   else row(s.shape[1]) for s in out_shape],
        out_shape=out_shape,
        compiler_params=_params("parallel"),
        name="mixer_prep",
    )(h, *consts, cq, sq, cr, sr)


FLASH_HEADS = 2
FLASH_KEYS = 256


def _flash_kernel(q_ref, k_ref, vt_ref, o_ref, *, blk):
    i = pl.program_id(2)
    lanes = [slice(a * HEAD_PAD, (a + 1) * HEAD_PAD) for a in range(FLASH_HEADS)]
    qs = [q_ref[:, sl] for sl in lanes]

    pieces = [(k0, a) for k0 in range(0, blk, FLASH_KEYS) for a in range(FLASH_HEADS)]

    def step(j, ms, accs, masked):
        base = pl.multiple_of(j * blk, blk)
        sts = []
        for k0, a in pieces:
            st = lax.dot_general(k_ref[pl.ds(base + k0, FLASH_KEYS), lanes[a]], qs[a], _NT,
                                 preferred_element_type=F32)
            if masked:
                key = lax.broadcasted_iota(jnp.int32, (FLASH_KEYS, blk), 0) + k0
                qry = lax.broadcasted_iota(jnp.int32, (FLASH_KEYS, blk), 1)
                st = jnp.where(key <= qry, st, NEG)
            sts.append(st)
        ms, accs = list(ms), list(accs)
        for (k0, a), st in zip(pieces, sts):
            m_new = jnp.maximum(ms[a], jnp.max(st, axis=0, keepdims=True))
            p = jnp.exp2(st - m_new).astype(BF16)
            accs[a] = (accs[a] * jnp.exp2(ms[a] - m_new)
                       + jnp.dot(vt_ref[j, lanes[a], k0:k0 + FLASH_KEYS], p, preferred_element_type=F32))
            ms[a] = m_new
        return tuple(ms), tuple(accs)

    m0 = tuple(jnp.full((1, blk), NEG, F32) for _ in lanes)
    acc0 = tuple(jnp.zeros((HEAD_PAD, blk), F32) for _ in lanes)
    ms, accs = lax.fori_loop(0, i, lambda j, c: step(j, *c, False), (m0, acc0))
    ms, accs = step(i, ms, accs, True)
    outs = []
    for acc in accs:
        outs.append((acc / acc[V_DIM:V_DIM + 1, :])[:V_DIM])
    o_ref[...] = jnp.concatenate(outs, axis=0).T.astype(BF16)


def _flash_attention(q, k, vt, batch, seq, blk):
    nq = seq // blk
    width = FLASH_HEADS * HEAD_PAD
    qspec = pl.BlockSpec((blk, width), lambda b, h, i: (b * nq + i, h))
    kspec = pl.BlockSpec((seq, width), lambda b, h, i: (b, h))
    vspec = pl.BlockSpec((nq, width, blk), lambda b, h, i: (b, h, 0))
    return pl.pallas_call(
        functools.partial(_flash_kernel, blk=blk),
        grid=(batch, MLA_HEADS // FLASH_HEADS, nq),
        in_specs=[qspec, kspec, vspec],
        out_specs=pl.BlockSpec((blk, FLASH_HEADS * V_DIM), lambda b, h, i: (b * nq + i, h)),
        out_shape=jax.ShapeDtypeStruct((batch * seq, D_ATT), BF16),
        compiler_params=_params("parallel", "parallel", "arbitrary"),
        name="flash_attention",
    )(q, k, vt)


T_PAD = 8
NEW_PAD = PAGE_SIZE
KEY_CHUNK = 1024


def _paged_kernel(pt_ref, q_ref, cn_ref, kn_ref, wq_ref, wuv_ref, ckv_hbm, kpe_hbm, o_ref,
                  ckv_buf, kpe_buf, kb_ref, s_ref, p_ref, sem, *, n_pages, n_new):
    b = pl.program_id(0)
    nb = pl.num_programs(0)
    past = n_pages * PAGE_SIZE
    slot = b % 2

    def page_copies(bb, sl, p):
        page = pt_ref[bb, p]
        rows = pl.ds(p * PAGE_SIZE, PAGE_SIZE)
        return (pltpu.make_async_copy(ckv_hbm.at[page], ckv_buf.at[sl, rows, :], sem.at[sl, 0]),
                pltpu.make_async_copy(kpe_hbm.at[page], kpe_buf.at[sl, :, rows], sem.at[sl, 1]))

    def start_fetch(bb, sl):
        for p in range(n_pages):
            for cp in page_copies(bb, sl, p):
                cp.start()

    def wait_fetch(bb, sl):
        for p in range(n_pages):
            for cp in page_copies(bb, sl, p):
                cp.wait()

    @pl.when(b == 0)
    def _():
        ckv_buf[:, past:, :] = jnp.zeros((2, NEW_PAD, KV_LORA), F32)
        start_fetch(0, 0)

    @pl.when(b + 1 < nb)
    def _():
        start_fetch(b + 1, 1 - slot)

    q8 = q_ref[0]
    qrep = jnp.concatenate([q8] * MLA_HEADS, axis=0)
    n_rows = MLA_HEADS * T_PAD
    row_h = lax.broadcasted_iota(jnp.int32, (n_rows, D_HEADS), 0) // T_PAD
    col_h = lax.broadcasted_iota(jnp.int32, (n_rows, D_HEADS), 1) // HEAD_PAD
    qm = jnp.where(row_h == col_h, qrep, jnp.zeros_like(qrep))
    ql = jnp.dot(qm, wq_ref[...], preferred_element_type=F32)
    q_lat = ql[:, :KV_LORA].astype(BF16)
    q_pe = ql[:, KV_LORA:KV_LORA + QK_ROPE].astype(BF16)

    wait_fetch(b, slot)
    ckv_buf[slot, past:past + T_PAD, :] = cn_ref[0]
    kpe_buf[slot, :, past:] = kn_ref[0]

    chunk = min(KEY_CHUNK, past)
    bounds = [(c * chunk, chunk) for c in range(past // chunk)] + [(past, NEW_PAD)]
    qt = lax.broadcasted_iota(jnp.int32, (n_rows, NEW_PAD), 0) % T_PAD
    kt = lax.broadcasted_iota(jnp.int32, (n_rows, NEW_PAD), 1)
    for r0, n in bounds:
        kb_ref[r0:r0 + n, :] = ckv_buf[slot, r0:r0 + n, :].astype(BF16)
    s_ref[...] = (lax.dot_general(q_lat, kb_ref[...], _NT, preferred_element_type=F32)
                  + jnp.dot(q_pe, kpe_buf[slot].astype(BF16), preferred_element_type=F32))
    s_ref[:, past:] = jnp.where((kt <= qt) & (kt < n_new), s_ref[:, past:], NEG)
    m = jnp.max(s_ref[...], axis=-1, keepdims=True)
    l = jnp.zeros((n_rows, 1), F32)
    for r0, n in bounds:
        p = jnp.exp2(s_ref[:, r0:r0 + n] - m)
        l = l + jnp.sum(p, axis=-1, keepdims=True)
        p_ref[:, r0:r0 + n] = p.astype(BF16)
    o = jnp.dot(p_ref[...], kb_ref[...], preferred_element_type=F32)
    o_lat = (o / l).astype(BF16)
    pv = jnp.dot(o_lat, wuv_ref[...], preferred_element_type=F32)
    out_row_h = lax.broadcasted_iota(jnp.int32, (n_rows, D_ATT), 0) // T_PAD
    out_col_h = lax.broadcasted_iota(jnp.int32, (n_rows, D_ATT), 1) // V_DIM
    pv = jnp.where(out_row_h == out_col_h, pv, 0.0)
    out = pv[0:T_PAD]
    for hh in range(1, MLA_HEADS):
        out = out + pv[hh * T_PAD:(hh + 1) * T_PAD]
    o_ref[0] = out.astype(BF16)


def _paged_attention(page_table, q8, ckv_new8, kpe_new_t, wq, wuv, cache_ckv, cache_kpe_t, n_new):
    nb, n_pages = page_table.shape
    rows = n_pages * PAGE_SIZE + NEW_PAD
    grid_spec = pltpu.PrefetchScalarGridSpec(
        num_scalar_prefetch=1,
        grid=(nb,),
        in_specs=[
            pl.BlockSpec((1, T_PAD, D_HEADS), lambda b, pt: (b, 0, 0)),
            pl.BlockSpec((1, T_PAD, KV_LORA), lambda b, pt: (b, 0, 0)),
            pl.BlockSpec((1, QK_ROPE, NEW_PAD), lambda b, pt: (b, 0, 0)),
            pl.BlockSpec(wq.shape, lambda b, pt: (0, 0)),
            pl.BlockSpec(wuv.shape, lambda b, pt: (0, 0)),
            pl.BlockSpec(memory_space=pl.ANY),
            pl.BlockSpec(memory_space=pl.ANY),
        ],
        out_specs=pl.BlockSpec((1, T_PAD, D_ATT), lambda b, pt: (b, 0, 0)),
        scratch_shapes=[
            pltpu.VMEM((2, rows, KV_LORA), F32),
            pltpu.VMEM((2, QK_ROPE, rows), F32),
            pltpu.VMEM((rows, KV_LORA), BF16),
            pltpu.VMEM((MLA_HEADS * T_PAD, rows), F32),
            pltpu.VMEM((MLA_HEADS * T_PAD, rows), BF16),
            pltpu.SemaphoreType.DMA((2, 2)),
        ],
    )
    return pl.pallas_call(
        functools.partial(_paged_kernel, n_pages=n_pages, n_new=n_new),
        grid_spec=grid_spec,
        out_shape=jax.ShapeDtypeStruct((nb, T_PAD, D_ATT), BF16),
        compiler_params=_params("arbitrary"),
        name="paged_attention",
    )(page_table, q8, ckv_new8, kpe_new_t, wq, wuv, cache_ckv, cache_kpe_t)


def _group_norm_gate(o, gate, gain, bias):
    mu = jnp.mean(o, axis=-1, keepdims=True)
    d = o - mu
    var = jnp.mean(d * d, axis=-1, keepdims=True)
    return _silu(gate) * (d * lax.rsqrt(var + LN_EPS) * gain + bias)


def _ret_kernel(rq_ref, rk_ref, rv_ref, rg_ref, din_ref, qd_ref, kd_ref, gc_ref, gg_ref, gb_ref, s0_ref,
                ro_ref, so_ref, s_ref, *, chunks):
    i = pl.program_id(1)

    @pl.when(i == 0)
    def _():
        s_ref[...] = s0_ref[0]

    units = [(c, hh) for c in range(chunks) for hh in range(RET_HEADS)]

    def tile(ref, c, hh):
        return ref[c * RET_CHUNK:(c + 1) * RET_CHUNK, hh * RET_DK:(hh + 1) * RET_DK]

    def lanes(ref, hh):
        return ref[:, hh * RET_DK:(hh + 1) * RET_DK]

    inner = {u: (lax.dot_general(tile(rq_ref, *u).astype(BF16), tile(rk_ref, *u).astype(BF16), _NT,
                                 preferred_element_type=F32) * lanes(din_ref, u[1])).astype(BF16)
             for u in units}
    kv = {u: lax.dot_general((tile(rk_ref, *u) * lanes(kd_ref, u[1])).astype(BF16), tile(rv_ref, *u), _TN,
                             preferred_element_type=F32)
          for u in units}
    o_intra = {u: jnp.dot(inner[u], tile(rv_ref, *u), preferred_element_type=F32) for u in units}
    states = [s_ref[hh] for hh in range(RET_HEADS)]
    for c, hh in units:
        s = states[hh]
        o = o_intra[c, hh] + jnp.dot((tile(rq_ref, c, hh) * lanes(qd_ref, hh)).astype(BF16), s.astype(BF16),
                                     preferred_element_type=F32)
        states[hh] = s * lanes(gc_ref, hh) + kv[c, hh]
        ro_ref[c * RET_CHUNK:(c + 1) * RET_CHUNK, hh * RET_DK:(hh + 1) * RET_DK] = _group_norm_gate(
            o, tile(rg_ref, c, hh), lanes(gg_ref, hh), lanes(gb_ref, hh)).astype(BF16)
    for hh in range(RET_HEADS):
        s_ref[hh] = states[hh]

    @pl.when(i == pl.num_programs(1) - 1)
    def _():
        so_ref[0] = s_ref[...]


def _retention_prompt(rq, rk, rv, rg, dec, gn_gain, gn_bias, state0, batch, seq, chunks):
    rows = chunks * RET_CHUNK
    steps = seq // rows
    rspec = pl.BlockSpec((rows, D_RET), lambda b, i: (b * steps + i, 0))
    sspec = pl.BlockSpec((1, RET_HEADS, RET_DK, RET_DV), lambda b, i: (b, 0, 0, 0))
    consts = [dec["din"], dec["qdec"], dec["kdec"], dec["gc"], gn_gain, gn_bias]
    return pl.pallas_call(
        functools.partial(_ret_kernel, chunks=chunks),
        grid=(batch, steps),
        in_specs=[rspec] * 4 + [_const_spec(c.shape) for c in consts] + [sspec],
        out_specs=[rspec, sspec],
        out_shape=[jax.ShapeDtypeStruct((batch * seq, D_RET), BF16),
                   jax.ShapeDtypeStruct((batch, RET_HEADS, RET_DK, RET_DV), F32)],
        scratch_shapes=[pltpu.VMEM((RET_HEADS, RET_DK, RET_DV), F32)],
        compiler_params=_params("parallel", "arbitrary"),
        name="retention_prompt",
    )(rq, rk, rv, rg, *consts, state0)


RS_BATCH = 8


def _ret_sample_kernel(q_ref, k_ref, v_ref, rg_ref, kt_ref, qd_ref, kd_ref, din_ref, gc_ref, gg_ref, gb_ref,
                       s0_ref, ro_ref, so_ref, *, n_tok):
    def one_seq(bi, carry):
        q8 = q_ref[bi]
        k8 = k_ref[bi]
        v8 = v_ref[bi]
        g8 = rg_ref[bi]
        qs = (q8 * qd_ref[...]).astype(BF16)
        heads = [slice(hh * RET_DK, (hh + 1) * RET_DK) for hh in range(RET_HEADS)]
        cross = [jnp.dot(qs[:, sl], s0_ref[bi, hh].astype(BF16), preferred_element_type=F32)
                 for hh, sl in enumerate(heads)]
        vbd = jnp.concatenate([v8 * kd_ref[hh] for hh in range(RET_HEADS)]
                              + [jnp.zeros((LANES - RET_HEADS * T_PAD, D_RET), F32)], axis=0)
        upd = jnp.dot(kt_ref[bi].astype(BF16), vbd.astype(BF16), preferred_element_type=F32)
        outs = []
        for hh, sl in enumerate(heads):
            o = cross[hh]
            din = din_ref[hh]
            for m in range(n_tok):
                a = jnp.sum(q8[:, sl] * k8[m:m + 1, sl], axis=-1, keepdims=True) * din[:, m:m + 1]
                o = o + a * v8[m:m + 1, sl]
            outs.append(_group_norm_gate(o, g8[:, sl], gg_ref[:, sl], gb_ref[:, sl]))
            so_ref[bi, hh] = s0_ref[bi, hh] * gc_ref[:, sl] + upd[:, sl]
        ro_ref[bi] = jnp.concatenate(outs, axis=1).astype(BF16)
        return carry

    lax.fori_loop(0, RS_BATCH, one_seq, 0)


def _retention_sample(q8, k8, v8, rg8, k_t, dec, gn_gain, gn_bias, state0, n_tok):
    nb = q8.shape[0]
    rspec = pl.BlockSpec((RS_BATCH, T_PAD, D_RET), lambda i: (i, 0, 0))
    sspec = pl.BlockSpec((RS_BATCH, RET_HEADS, RET_DK, RET_DV), lambda i: (i, 0, 0, 0))
    consts = [dec["qdec"], dec["kdec"], dec["din"], dec["gc"], gn_gain, gn_bias]
    return pl.pallas_call(
        functools.partial(_ret_sample_kernel, n_tok=n_tok),
        grid=(nb // RS_BATCH,),
        in_specs=[rspec] * 4 + [pl.BlockSpec((RS_BATCH, RET_DK, LANES), lambda i: (i, 0, 0))]
                 + [_const_spec(c.shape) for c in consts] + [sspec],
        out_specs=[rspec, sspec],
        out_shape=[jax.ShapeDtypeStruct((nb, T_PAD, D_RET), BF16),
                   jax.ShapeDtypeStruct((nb, RET_HEADS, RET_DK, RET_DV), F32)],
        compiler_params=_params("parallel"),
        name="retention_sample",
    )(q8, k8, v8, rg8, k_t, *consts, state0)


def _outproj_ln_kernel(a_ref, ro_ref, h_ref, wa_ref, wr_ref, g_ref, b_ref, o_ref):
    y = (jnp.dot(a_ref[...], wa_ref[...], preferred_element_type=F32)
         + jnp.dot(ro_ref[...], wr_ref[...], preferred_element_type=F32))
    o_ref[...] = _layer_norm(ALPHA * h_ref[...] + y, g_ref[...], b_ref[...])


def _outproj_ln(a, ro, h, wa, wr, g, b, tm):
    m = h.shape[0]

    def row(n):
        return pl.BlockSpec((tm, n), lambda i: (i, 0))

    return pl.pallas_call(
        _outproj_ln_kernel,
        grid=(m // tm,),
        in_specs=[row(D_ATT), row(D_RET), row(D_MODEL), _const_spec(wa.shape), _const_spec(wr.shape),
                  _const_spec(g.shape), _const_spec(b.shape)],
        out_specs=row(D_MODEL),
        out_shape=jax.ShapeDtypeStruct((m, D_MODEL), F32),
        compiler_params=_params("parallel"),
        name="outproj_ln",
    )(a, ro, h, wa, wr, g, b)


HALO = 16


def _pool_tail(pooled_groups, x, pw_ref, pb_ref, ps_ref, g_ref, b_ref):
    ys = [jnp.dot(p.astype(BF16), pw_ref[gi], preferred_element_type=F32) for gi, p in enumerate(pooled_groups)]
    y = (jnp.concatenate(ys, axis=-1) + pb_ref[...]) * ps_ref[...]
    return _layer_norm(ALPHA * x + y, g_ref[...], b_ref[...])


HALO_P = 32


def _pool_prompt_kernel(h_ref, halo_ref, pre_ref, pw_ref, pb_ref, ps_ref, g_ref, b_ref, o_ref,
                        xs_ref, a_ref, b2_ref, *, tm, tiles, start):
    t = pl.program_id(0) % tiles
    x = h_ref[...]
    xs_ref[0:HALO_P, :] = jnp.where(t == 0, pre_ref[0], halo_ref[...])
    xs_ref[HALO_P:, :] = x
    n = HALO_P + tm
    pos = start + t * tm + lax.broadcasted_iota(jnp.int32, (tm, 1), 0)
    pooled = []
    src = xs_ref
    for k, wl in enumerate(POOL_WINDOWS, start=1):
        lo = (k - 1) * POOL_GROUP
        r0, shift = 8 * k, wl // 2
        level = src[r0:n, lo:] + src[r0 - shift:n - shift, lo:]
        cnt = jnp.minimum(pos + 1, wl).astype(F32)
        pooled.append(level[HALO_P - r0:, :POOL_GROUP] / cnt - x[:, lo:lo + POOL_GROUP])
        if k < len(POOL_WINDOWS):
            dst = a_ref if k % 2 else b2_ref
            dst[r0:n, lo + POOL_GROUP:] = level[:, POOL_GROUP:]
            src = dst
    o_ref[...] = _pool_tail(pooled, x, pw_ref, pb_ref, ps_ref, g_ref, b_ref)


def _pool_prompt(h, prefix, pw, pb, ps, g, b, batch, seq, start, tm):
    assert all(w == 2 ** (i + 1) for i, w in enumerate(POOL_WINDOWS)) and 8 * len(POOL_WINDOWS) <= HALO_P
    tiles = seq // tm
    per = tm // HALO_P
    consts = [pw, pb, ps, g, b]
    buf = pltpu.VMEM((HALO_P + tm, D_MODEL), F32)
    return pl.pallas_call(
        functools.partial(_pool_prompt_kernel, tm=tm, tiles=tiles, start=start),
        grid=(batch * tiles,),
        in_specs=[pl.BlockSpec((tm, D_MODEL), lambda i: (i, 0)),
                  pl.BlockSpec((HALO_P, D_MODEL), lambda i: (jnp.maximum(i * per - 1, 0), 0)),
                  pl.BlockSpec((1, HALO_P, D_MODEL), lambda i: (i // tiles, 0, 0))]
                 + [_const_spec(c.shape) for c in consts],
        out_specs=pl.BlockSpec((tm, D_MODEL), lambda i: (i, 0)),
        out_shape=jax.ShapeDtypeStruct((batch * seq, D_MODEL), F32),
        scratch_shapes=[buf, buf, buf],
        compiler_params=_params("parallel"),
        name="pool_prompt",
    )(h, h, prefix, *consts)


def _pool_sample_kernel(xs_ref, pw_ref, pb_ref, ps_ref, g_ref, b_ref, o_ref, *, n_tok, start):
    nb = xs_ref.shape[1]
    x = jnp.concatenate([xs_ref[HALO + t] for t in range(n_tok)], axis=0)
    pooled = []
    for gi, wl in enumerate(POOL_WINDOWS):
        sl = slice(gi * POOL_GROUP, (gi + 1) * POOL_GROUP)
        parts = []
        for t in range(n_tok):
            acc = xs_ref[HALO + t, :, sl]
            for d in range(1, wl):
                acc = acc + xs_ref[HALO + t - d, :, sl]
            parts.append(acc / float(min(start + t + 1, wl)))
        pooled.append(jnp.concatenate(parts, axis=0) - x[:, sl])
    y = _pool_tail(pooled, x, pw_ref, pb_ref, ps_ref, g_ref, b_ref)
    for t in range(n_tok):
        o_ref[t] = y[t * nb:(t + 1) * nb]


def _pool_sample(xs_t, pw, pb, ps, g, b, n_tok, start):
    nb = xs_t.shape[1]
    args = [xs_t, pw, pb, ps, g, b]
    return pl.pallas_call(
        functools.partial(_pool_sample_kernel, n_tok=n_tok, start=start),
        grid=(1,),
        in_specs=[_const_spec(a.shape) for a in args],
        out_specs=_const_spec((n_tok, nb, D_MODEL)),
        out_shape=jax.ShapeDtypeStruct((n_tok, nb, D_MODEL), F32),
        compiler_params=_params("arbitrary"),
        name="pool_sample",
    )(*args)


def _head_pad(w, width):
    r, nh, d = w.shape
    out = jnp.zeros((r, nh, HEAD_PAD), w.dtype).at[:, :, :d].set(w)
    return out.reshape(r, nh * HEAD_PAD)[:, :width]


def _mixer_weights(mix_w_in, q_norm, kv_norm, w_uq, w_uk, w_uv, mix_w_out):
    offs = np.concatenate([[0], np.cumsum(SPLIT_SIZES)])
    wq, wckv, wkpe, wrq, wrk, wrv, wrg = [mix_w_in[:, offs[i]:offs[i + 1]] for i in range(7)]
    half = QK_ROPE // 2
    z_lo = jnp.zeros((D_MODEL, QK_NOPE), F32)
    z_hi = jnp.zeros((D_MODEL, HEAD_PAD - QK_NOPE - QK_ROPE), F32)
    kpe_blk = jnp.concatenate([z_lo, wkpe, z_hi], axis=1)
    kpe_swp = jnp.concatenate([z_lo, -wkpe[:, half:], wkpe[:, :half], z_hi], axis=1)
    w_main = jnp.concatenate([wq, wckv, kpe_blk, kpe_swp, wrq, wrk, wrv, wrg], axis=1).astype(BF16)
    pe = w_uq[:, :, QK_NOPE:]
    uq1 = _head_pad(w_uq, D_HEADS)
    uq2 = _head_pad(jnp.concatenate([jnp.zeros_like(w_uq[:, :, :QK_NOPE]), -pe[:, :, half:], pe[:, :, :half]], axis=2),
                    D_HEADS)
    wk = _head_pad(w_uk, D_HEADS)
    wv = _head_pad(w_uv, D_HEADS)
    wa = mix_w_out[:D_ATT]
    wr = mix_w_out[MLA_HEADS * V_DIM:]
    wuk_t = jnp.zeros((MLA_HEADS, HEAD_PAD, KV_LORA), F32).at[:, :QK_NOPE, :].set(
        jnp.transpose(w_uk, (1, 2, 0))).reshape(D_HEADS, KV_LORA)
    r = np.arange(D_HEADS) % HEAD_PAD
    e_pe = ((r[:, None] - QK_NOPE) == np.arange(LANES)[None, :]) & (r[:, None] >= QK_NOPE) & (r[:, None] < QK_NOPE + QK_ROPE)
    wq_abs = jnp.concatenate([wuk_t, jnp.asarray(e_pe.astype(np.float32))], axis=1)
    return {
        "w_main": w_main, "q_norm": q_norm[None, :], "kv_norm": kv_norm[None, :],
        "uq1": uq1.astype(BF16), "uq2": uq2.astype(BF16), "wk": wk.astype(BF16),
        "wv_c": w_uv.reshape(KV_LORA, D_ATT).astype(BF16),
        "wv_t": wv.T.astype(BF16), "wa": wa.astype(BF16), "wr": wr.astype(BF16),
        "wq_abs": wq_abs.astype(BF16),
    }


def _rope_tables(pos):
    def angles(r):
        inv = 1.0 / (ROPE_BASE ** (np.arange(0, r, 2, dtype=np.float64) / r))
        return pos.astype(np.float64)[:, None] * inv[None, :]

    n = pos.shape[0]
    a = angles(QK_ROPE)
    c, s = np.cos(a), np.sin(a)
    hi = HEAD_PAD - QK_NOPE - QK_ROPE
    cq = np.concatenate([np.ones((n, QK_NOPE)), c, c, np.ones((n, hi))], axis=1)
    sq = np.concatenate([np.zeros((n, QK_NOPE)), s, s, np.zeros((n, hi))], axis=1)
    a = angles(RET_DK)
    c, s = np.cos(a), np.sin(a)
    tabs = (cq, sq, np.concatenate([c, c], axis=1), np.concatenate([-s, s], axis=1))
    return tuple(jnp.asarray(t, F32) for t in tabs)


def _ret_log_decay():
    return jnp.log(1.0 - 2.0 ** (-5.0 - jnp.arange(RET_HEADS, dtype=F32)))


def _ret_decay_tables(chunk):
    log_g = _ret_log_decay()
    idx = jnp.arange(chunk, dtype=F32)
    diff = idx[:, None] - idx[None, :]
    d_in = jnp.where(diff >= 0, jnp.exp(jnp.maximum(diff, 0.0)[None] * log_g[:, None, None]), 0.0)
    q_dec = jnp.exp((idx + 1.0)[None, :] * log_g[:, None])
    k_dec = jnp.exp((chunk - 1.0 - idx)[None, :] * log_g[:, None])
    g_c = jnp.exp(chunk * log_g)
    lanes = (chunk, RET_HEADS * RET_DK)
    return {
        "din": jnp.transpose(d_in, (1, 0, 2)).reshape(chunk, RET_HEADS * chunk),
        "qdec": jnp.broadcast_to(q_dec.T[:, :, None], (chunk, RET_HEADS, RET_DK)).reshape(lanes),
        "kdec": jnp.broadcast_to(k_dec.T[:, :, None], (chunk, RET_HEADS, RET_DK)).reshape(lanes),
        "gc": jnp.broadcast_to(g_c[:, None], (RET_HEADS, RET_DV)).reshape(1, RET_HEADS * RET_DV),
    }


def _ret_sample_tables(n_tok):
    log_g = _ret_log_decay()
    idx = jnp.arange(n_tok, dtype=F32)
    diff = idx[:, None] - idx[None, :]
    d_in = jnp.where(diff >= 0, jnp.exp(jnp.maximum(diff, 0.0)[None] * log_g[:, None, None]), 0.0)
    q_dec = jnp.exp((idx + 1.0)[None, :] * log_g[:, None])
    k_dec = jnp.exp((n_tok - 1.0 - idx)[None, :] * log_g[:, None])
    g_c = jnp.exp(n_tok * log_g)
    pad_t = T_PAD - n_tok
    own_lanes = (np.arange(D_RET) // RET_DV)[None, None, :] == np.arange(RET_HEADS)[:, None, None]
    return {
        "qdec": jnp.pad(jnp.broadcast_to(q_dec.T[:, :, None], (n_tok, RET_HEADS, RET_DK)).reshape(n_tok, D_RET),
                        ((0, pad_t), (0, 0))),
        "kdec": jnp.pad(k_dec[:, :, None] * jnp.asarray(own_lanes, F32), ((0, 0), (0, pad_t), (0, 0))),
        "din": jnp.pad(d_in, ((0, 0), (0, pad_t), (0, LANES - n_tok))),
        "gc": jnp.broadcast_to(g_c[:, None], (RET_HEADS, RET_DV)).reshape(1, D_RET),
    }


FFN_ORDER = ((0, 0), (0, 1), (1, 0), (1, 1))


def _trunk(x, start, ret_state0, pool_prefix, mla_cache, w, mw):
    batch, seq, _ = x.shape
    m = batch * seq
    prompt = mla_cache is None
    tm = min(512, m)
    h = x.reshape(m, D_MODEL)

    h = yield h
    pos = start + (np.arange(seq) if prompt else np.arange(m) % seq)
    flash_blk = min(1024, seq) if prompt else tm
    q, k, vt, ckv, kpe, rq, rk, rv, rg = _mixer_prep(h, mw, _rope_tables(pos), tm, flash_blk)
    kpe = kpe[:, QK_NOPE:QK_NOPE + QK_ROPE]
    gn_gain = w["ret_gn_gain"][0][None, :]
    gn_bias = w["ret_gn_bias"][0][None, :]
    if prompt:
        a = _flash_attention(q, k, vt, batch, seq, flash_blk)
        chunk = RET_CHUNK if seq % RET_CHUNK == 0 else seq
        assert chunk == RET_CHUNK
        ro, ret_state = _retention_prompt(rq, rk, rv, rg, _ret_decay_tables(chunk), gn_gain, gn_bias,
                                          ret_state0, batch, seq, 8)
    else:
        assert seq <= T_PAD and seq % RET_CHUNK != 0
        cache_ckv, cache_kpe, page_table = mla_cache
        pad_t = ((0, 0), (0, T_PAD - seq), (0, 0))
        q8 = jnp.pad(q.reshape(batch, seq, D_HEADS), pad_t)
        cn8 = jnp.pad(ckv.reshape(batch, seq, KV_LORA), pad_t)
        kn_t = jnp.pad(jnp.swapaxes(kpe.reshape(batch, seq, QK_ROPE), 1, 2), ((0, 0), (0, 0), (0, NEW_PAD - seq)))
        a8 = _paged_attention(page_table, q8, cn8, kn_t, mw["wq_abs"], mw["wv_c"], cache_ckv,
                              jnp.swapaxes(cache_kpe, 1, 2), seq)
        a = a8[:, :seq].reshape(m, D_ATT)

        def rows8(t):
            return jnp.pad(t.reshape(batch, seq, D_RET), pad_t)

        k8 = rows8(rk)
        k_t = jnp.transpose(k8.reshape(batch, T_PAD, RET_HEADS, RET_DK), (0, 3, 2, 1)).reshape(
            batch, RET_DK, RET_HEADS * T_PAD)
        k_t = jnp.pad(k_t, ((0, 0), (0, 0), (0, LANES - RET_HEADS * T_PAD)))
        ro8, ret_state = _retention_sample(rows8(rq), k8, rows8(rv.astype(F32)), rows8(rg), k_t,
                                           _ret_sample_tables(seq), gn_gain, gn_bias, ret_state0, seq)
        ro = ro8[:, :seq].reshape(m, D_RET)
    h = _outproj_ln(a, ro, h, mw["wa"], mw["wr"], w["ln_gain"][0, 1][None, :], w["ln_bias"][0, 1][None, :], tm)
    h = yield h

    h = yield h
    xp_tail = jnp.concatenate([pool_prefix, h.reshape(batch, seq, D_MODEL)], axis=1)[:, -POOL_PREFIX:]
    halo = HALO_P if prompt else HALO
    prefix16 = jnp.pad(pool_prefix, ((0, 0), (halo - POOL_PREFIX, 0), (0, 0)))
    pool_args = (w["pool_w"], w["pool_b"][0][None, :], w["pool_scale"][0][None, :],
                 w["ln_gain"][1, 1][None, :], w["ln_bias"][1, 1][None, :])
    if prompt:
        h = _pool_prompt(h, prefix16, *pool_args, batch, seq, start, tm)
    else:
        xs_t = jnp.transpose(jnp.concatenate([prefix16, h.reshape(batch, seq, D_MODEL)], axis=1), (1, 0, 2))
        h = jnp.transpose(_pool_sample(xs_t, *pool_args, seq, start), (1, 0, 2)).reshape(m, D_MODEL)
    h = yield h
    return (h.reshape(batch, seq, D_MODEL), ckv.reshape(1, batch, seq, KV_LORA),
            kpe.reshape(1, batch, seq, QK_ROPE), ret_state[None], xp_tail[None])


def _finish(trunk, last):
    try:
        trunk.send(last)
    except StopIteration as done:
        return done.value
    raise AssertionError("trunk yielded more FFN requests than FFN_ORDER")


def kernel(x_prompt, x_sample, cache_mla_ckv, cache_mla_kpe, state_ret, state_pool, page_table, ffn_w_gate, ffn_w_up, ffn_w_down, ln_gain, ln_bias, mix_w_in, mla_q_norm, mla_kv_norm, mla_w_uq, mla_w_uk, mla_w_uv, ret_gn_gain, ret_gn_bias, mix_w_out, pool_w, pool_b, pool_scale):
    assert DEPTH == 2 and mix_w_in.shape[0] == 1 and pool_w.shape[0] == 1
    w = {
        "wg": ffn_w_gate, "wu": ffn_w_up, "wd": ffn_w_down,
        "ln_gain": ln_gain, "ln_bias": ln_bias, "ret_gn_gain": ret_gn_gain, "ret_gn_bias": ret_gn_bias,
        "pool_w": pool_w[0].astype(BF16), "pool_b": pool_b, "pool_scale": pool_scale,
    }
    mw = _mixer_weights(mix_w_in[0], mla_q_norm[0], mla_kv_norm[0], mla_w_uq[0], mla_w_uk[0], mla_w_uv[0],
                        mix_w_out[0])
    bp = x_prompt.shape[0]
    zero_ret = jnp.zeros((bp, RET_HEADS, RET_DK, RET_DV), F32)
    zero_pool = jnp.zeros((bp, POOL_PREFIX, D_MODEL), x_prompt.dtype)
    trunk_p = _trunk(x_prompt, 0, zero_ret, zero_pool, None, w, mw)
    trunk_s = _trunk(x_sample, PAST_LEN, state_ret[0], state_pool[0],
                     (cache_mla_ckv[0], cache_mla_kpe[0], page_table), w, mw)
    h_p, h_s = next(trunk_p), next(trunk_s)
    for n, (layer, half) in enumerate(FFN_ORDER):
        f_p, f_s = _ffn_ln(h_p, h_s, w["wg"], w["wu"], w["wd"], layer, half,
                           ln_gain[layer, 2 * half][None, :], ln_bias[layer, 2 * half][None, :])
        if n + 1 < len(FFN_ORDER):
            h_p, h_s = trunk_p.send(f_p), trunk_s.send(f_s)
    y_p, ckv_p, kpe_p, ret_p, pool_p = _finish(trunk_p, f_p)
    y_s, ckv_s, kpe_s, ret_s, pool_s = _finish(trunk_s, f_s)
    return (y_p, y_s, ckv_p, kpe_p, ckv_s, kpe_s, ret_p, ret_s, pool_p, pool_s)
```

```python
import functools

import numpy as np
import jax
import jax.numpy as jnp
from jax import lax
from jax.experimental import pallas as pl
from jax.experimental.pallas import tpu as pltpu

F32 = jnp.float32
BF16 = jnp.bfloat16

D_MODEL = 1024
DEPTH = 2
PAST_LEN = 8192
PAGE_SIZE = 128
ALPHA = (2 * DEPTH) ** 0.25
D_FF = 2816
MLA_HEADS = 8
Q_LORA = 512
KV_LORA = 256
QK_NOPE = 64
QK_ROPE = 32
V_DIM = 64
RET_HEADS = 4
RET_DK = 128
RET_DV = 128
RET_CHUNK = 128
POOL_WINDOWS = (2, 4, 8, 16)
POOL_GROUPS = 4
POOL_GROUP = D_MODEL // POOL_GROUPS
POOL_PREFIX = 15
ROPE_BASE = 10000.0
LN_EPS = 1e-5
RMS_EPS = 1e-6
SPLIT_SIZES = (Q_LORA, KV_LORA, QK_ROPE, RET_HEADS * RET_DK, RET_HEADS * RET_DK,
               RET_HEADS * RET_DV, RET_HEADS * RET_DV)
ATT_SCALE = (QK_NOPE + QK_ROPE) ** -0.5 * 1.4426950408889634

LANES = 128
HEAD_PAD = LANES
D_HEADS = MLA_HEADS * HEAD_PAD
D_ATT = MLA_HEADS * V_DIM
D_RET = RET_HEADS * RET_DV
NEG = -1e30
VMEM_LIMIT = 56 * 1024 * 1024

_NT = (((1,), (1,)), ((), ()))
_TN = (((0,), (0,)), ((), ()))


def _params(*sem):
    return pltpu.CompilerParams(dimension_semantics=sem, vmem_limit_bytes=VMEM_LIMIT)


def _const_spec(shape):
    nd = len(shape)
    return pl.BlockSpec(shape, lambda *_: (0,) * nd, pipeline_mode=pl.Buffered(1))


def _layer_norm(y, g, b):
    mu = jnp.mean(y, axis=-1, keepdims=True)
    d = y - mu
    var = jnp.mean(d * d, axis=-1, keepdims=True)
    return d * lax.rsqrt(var + LN_EPS) * g + b


def _silu(x):
    return x * jax.nn.sigmoid(x)


FFN_CHUNK = 256
FFN_ROWS = 512


def _ffn_ln_kernel(xp_ref, xs_ref, wg_ref, wu_ref, wd_ref, g_ref, b_ref, op_ref, os_ref, *, n_prompt):
    is_prompt = pl.program_id(0) < n_prompt

    def half_step(x_ref, o_ref):
        x = x_ref[...]
        xb = x.astype(BF16)
        acc = None
        for c in range(D_FF // FFN_CHUNK):
            sl = slice(c * FFN_CHUNK, (c + 1) * FFN_CHUNK)
            g = jnp.dot(xb, wg_ref[:, sl].astype(BF16), preferred_element_type=F32)
            u = jnp.dot(xb, wu_ref[:, sl].astype(BF16), preferred_element_type=F32)
            a = (_silu(g) * u).astype(BF16)
            d = jnp.dot(a, wd_ref[sl, :].astype(BF16), preferred_element_type=F32)
            acc = d if acc is None else acc + d
        o_ref[...] = _layer_norm(ALPHA * x + 0.5 * acc, g_ref[...], b_ref[...])

    pl.when(is_prompt)(functools.partial(half_step, xp_ref, op_ref))
    pl.when(jnp.logical_not(is_prompt))(functools.partial(half_step, xs_ref, os_ref))


def _ffn_ln(xp, xs, wg, wu, wd, layer, half, g, b):
    tm = FFN_ROWS
    assert xp.shape[0] % tm == 0 and xs.shape[0] % tm == 0
    n_p, n_s = xp.shape[0] // tm, xs.shape[0] // tm
    pspec = pl.BlockSpec((tm, D_MODEL), lambda i: (jnp.minimum(i, n_p - 1), 0))
    sspec = pl.BlockSpec((tm, D_MODEL), lambda i: (jnp.maximum(i - n_p, 0), 0))

    def wspec(w):
        return pl.BlockSpec((None, None) + w.shape[2:], lambda i: (layer, half, 0, 0),
                            pipeline_mode=pl.Buffered(1))

    return pl.pallas_call(
        functools.partial(_ffn_ln_kernel, n_prompt=n_p),
        grid=(n_p + n_s,),
        in_specs=[pspec, sspec, wspec(wg), wspec(wu), wspec(wd), _const_spec(g.shape), _const_spec(b.shape)],
        out_specs=[pspec, sspec],
        out_shape=[jax.ShapeDtypeStruct(xp.shape, F32), jax.ShapeDtypeStruct(xs.shape, F32)],
        compiler_params=_params("arbitrary"),
        name="ffn_ln",
    )(xp, xs, wg, wu, wd, g, b)


_C_QL, _C_CKV, _C_KPE, _C_KPS, _C_RQ, _C_RK, _C_RV, _C_RG, _C_END = (
    0, 512, 768, 896, 1024, 1536, 2048, 2560, 3072)


def _prep_kernel(h_ref, w_ref, qn_ref, kvn_ref, uq1_ref, uq2_ref, wk_ref, wvt_ref,
                 cq_ref, sq_ref, cr_ref, sr_ref,
                 q_ref, k_ref, vt_ref, ckv_ref, kpe_ref, rq_ref, rk_ref, rv_ref, rg_ref):
    xb = h_ref[...].astype(BF16)

    def proj(a, b):
        return jnp.dot(xb, w_ref[:, a:b], preferred_element_type=F32)

    cq = cq_ref[...]
    sq = sq_ref[...]
    ql = proj(_C_QL, _C_CKV)
    c = proj(_C_CKV, _C_KPE)
    kp = proj(_C_KPE, _C_RQ)
    rq = proj(_C_RQ, _C_RK)
    rk = proj(_C_RK, _C_RV)
    rv_ref[...] = proj(_C_RV, _C_RG).astype(BF16)
    rg_ref[...] = proj(_C_RG, _C_END)
    qn = (ql * lax.rsqrt(jnp.mean(ql * ql, axis=-1, keepdims=True) + RMS_EPS) * qn_ref[...]).astype(BF16)
    ckv = c * lax.rsqrt(jnp.mean(c * c, axis=-1, keepdims=True) + RMS_EPS) * kvn_ref[...]
    ckv_ref[...] = ckv
    cb = ckv.astype(BF16)
    kpe = kp[:, :HEAD_PAD] * cq + kp[:, HEAD_PAD:] * sq
    kpe_ref[...] = kpe
    qa = jnp.dot(qn, uq1_ref[...], preferred_element_type=F32)
    qb = jnp.dot(qn, uq2_ref[...], preferred_element_type=F32)
    kn = jnp.dot(cb, wk_ref[...], preferred_element_type=F32)
    vt = lax.dot_general(wvt_ref[...], cb, _NT, preferred_element_type=F32)
    for hh in range(MLA_HEADS):
        sl = slice(hh * HEAD_PAD, (hh + 1) * HEAD_PAD)
        q_ref[:, sl] = ((qa[:, sl] * cq + qb[:, sl] * sq) * ATT_SCALE).astype(BF16)
        k_ref[:, sl] = (kn[:, sl] + kpe).astype(BF16)
    head_row = lax.broadcasted_iota(jnp.int32, vt.shape, 0) % HEAD_PAD
    vt_ref[0] = jnp.where(head_row == V_DIM, 1.0, vt).astype(BF16)
    cr = cr_ref[...]
    sr = sr_ref[...]
    for hh in range(RET_HEADS):
        sl = slice(hh * RET_DK, (hh + 1) * RET_DK)
        xq = rq[:, sl]
        xk = rk[:, sl]
        rq_ref[:, sl] = xq * cr + pltpu.roll(xq, RET_DK // 2, 1) * sr
        rk_ref[:, sl] = (xk * cr + pltpu.roll(xk, RET_DK // 2, 1) * sr) * (RET_DK ** -0.5)


def _mixer_prep(h, mw, tabs, tm, vt_blk):
    m = h.shape[0]
    cq, sq, cr, sr = tabs
    tab_blocks = cq.shape[0] // tm
    per = vt_blk // tm

    def row(n):
        return pl.BlockSpec((tm, n), lambda i: (i, 0))

    tab = pl.BlockSpec((tm, LANES), lambda i: (i % tab_blocks, 0))
    consts = [mw["w_main"], mw["q_norm"], mw["kv_norm"], mw["uq1"], mw["uq2"], mw["wk"], mw["wv_t"]]
    out_shape = [
        jax.ShapeDtypeStruct((m, D_HEADS), BF16),
        jax.ShapeDtypeStruct((m, D_HEADS), BF16),
        jax.ShapeDtypeStruct((m // vt_blk, D_HEADS, vt_blk), BF16),
        jax.ShapeDtypeStruct((m, KV_LORA), F32),
        jax.ShapeDtypeStruct((m, LANES), F32),
        jax.ShapeDtypeStruct((m, D_RET), F32),
        jax.ShapeDtypeStruct((m, D_RET), F32),
        jax.ShapeDtypeStruct((m, D_RET), BF16),
        jax.ShapeDtypeStruct((m, D_RET), F32),
    ]
    return pl.pallas_call(
        _prep_kernel,
        grid=(m // tm,),
        in_specs=[row(D_MODEL)] + [_const_spec(c.shape) for c in consts] + [tab] * 4,
        out_specs=[pl.BlockSpec((1, D_HEADS, tm), lambda i: (i // per, 0, i % per)) if len(s.shape) == 3
                   else row(s.shape[1]) for s in out_shape],
        out_shape=out_shape,
        compiler_params=_params("parallel"),
        name="mixer_prep",
    )(h, *consts, cq, sq, cr, sr)


FLASH_HEADS = 2
FLASH_KEYS = 256


def _flash_kernel(q_ref, k_ref, vt_ref, o_ref, *, blk):
    i = pl.program_id(2)
    lanes = [slice(a * HEAD_PAD, (a + 1) * HEAD_PAD) for a in range(FLASH_HEADS)]
    qs = [q_ref[:, sl] for sl in lanes]

    pieces = [(k0, a) for k0 in range(0, blk, FLASH_KEYS) for a in range(FLASH_HEADS)]

    def step(j, ms, accs, masked):
        base = pl.multiple_of(j * blk, blk)
        sts = []
        for k0, a in pieces:
            q0 = k0 if masked else 0
            st = lax.dot_general(k_ref[pl.ds(base + k0, FLASH_KEYS), lanes[a]], qs[a][q0:], _NT,
                                 preferred_element_type=F32)
            if masked:
                key = lax.broadcasted_iota(jnp.int32, st.shape, 0)
                qry = lax.broadcasted_iota(jnp.int32, st.shape, 1)
                st = jnp.where(key <= qry, st, NEG)
            sts.append(st)
        ms, accs = list(ms), list(accs)
        for (k0, a), st in zip(pieces, sts):
            q0 = k0 if masked else 0
            m_old, acc_old = ms[a][:, q0:], accs[a][:, q0:]
            m_new = jnp.maximum(m_old, jnp.max(st, axis=0, keepdims=True))
            p = jnp.exp2(st - m_new).astype(BF16)
            acc_new = (acc_old * jnp.exp2(m_old - m_new)
                       + jnp.dot(vt_ref[j, lanes[a], k0:k0 + FLASH_KEYS], p, preferred_element_type=F32))
            ms[a] = jnp.concatenate([ms[a][:, :q0], m_new], axis=1) if q0 else m_new
            accs[a] = jnp.concatenate([accs[a][:, :q0], acc_new], axis=1) if q0 else acc_new
        return tuple(ms), tuple(accs)

    m0 = tuple(jnp.full((1, blk), NEG, F32) for _ in lanes)
    acc0 = tuple(jnp.zeros((HEAD_PAD, blk), F32) for _ in lanes)
    ms, accs = lax.fori_loop(0, i, lambda j, c: step(j, *c, False), (m0, acc0))
    ms, accs = step(i, ms, accs, True)
    outs = []
    for acc in accs:
        outs.append((acc / acc[V_DIM:V_DIM + 1, :])[:V_DIM])
    o_ref[...] = jnp.concatenate(outs, axis=0).T.astype(BF16)


def _flash_attention(q, k, vt, batch, seq, blk):
    nq = seq // blk
    width = FLASH_HEADS * HEAD_PAD
    qspec = pl.BlockSpec((blk, width), lambda b, h, i: (b * nq + i, h))
    kspec = pl.BlockSpec((seq, width), lambda b, h, i: (b, h))
    vspec = pl.BlockSpec((nq, width, blk), lambda b, h, i: (b, h, 0))
    return pl.pallas_call(
        functools.partial(_flash_kernel, blk=blk),
        grid=(batch, MLA_HEADS // FLASH_HEADS, nq),
        in_specs=[qspec, kspec, vspec],
        out_specs=pl.BlockSpec((blk, FLASH_HEADS * V_DIM), lambda b, h, i: (b * nq + i, h)),
        out_shape=jax.ShapeDtypeStruct((batch * seq, D_ATT), BF16),
        compiler_params=_params("parallel", "parallel", "arbitrary"),
        name="flash_attention",
    )(q, k, vt)


T_PAD = 8
NEW_PAD = PAGE_SIZE
KEY_CHUNK = 1024


def _paged_kernel(pt_ref, q_ref, cn_ref, kn_ref, wq_ref, wuv_ref, ckv_hbm, kpe_hbm, o_ref,
                  ckv_buf, kpe_buf, kb_ref, s_ref, p_ref, sem, *, n_pages, n_new):
    b = pl.program_id(0)
    nb = pl.num_programs(0)
    past = n_pages * PAGE_SIZE
    slot = b % 2

    def page_copies(bb, sl, p):
        page = pt_ref[bb, p]
        rows = pl.ds(p * PAGE_SIZE, PAGE_SIZE)
        return (pltpu.make_async_copy(ckv_hbm.at[page], ckv_buf.at[sl, rows, :], sem.at[sl, 0]),
                pltpu.make_async_copy(kpe_hbm.at[page], kpe_buf.at[sl, :, rows], sem.at[sl, 1]))

    def start_fetch(bb, sl):
        for p in range(n_pages):
            for cp in page_copies(bb, sl, p):
                cp.start()

    def wait_fetch(bb, sl):
        for p in range(n_pages):
            for cp in page_copies(bb, sl, p):
                cp.wait()

    @pl.when(b == 0)
    def _():
        ckv_buf[:, past:, :] = jnp.zeros((2, NEW_PAD, KV_LORA), F32)
        start_fetch(0, 0)

    @pl.when(b + 1 < nb)
    def _():
        start_fetch(b + 1, 1 - slot)

    q8 = q_ref[0]
    qrep = jnp.concatenate([q8] * MLA_HEADS, axis=0)
    n_rows = MLA_HEADS * T_PAD
    row_h = lax.broadcasted_iota(jnp.int32, (n_rows, D_HEADS), 0) // T_PAD
    col_h = lax.broadcasted_iota(jnp.int32, (n_rows, D_HEADS), 1) // HEAD_PAD
    qm = jnp.where(row_h == col_h, qrep, jnp.zeros_like(qrep))
    ql = jnp.dot(qm, wq_ref[...], preferred_element_type=F32)
    q_lat = ql[:, :KV_LORA].astype(BF16)
    q_pe = ql[:, KV_LORA:KV_LORA + QK_ROPE].astype(BF16)

    wait_fetch(b, slot)
    ckv_buf[slot, past:past + T_PAD, :] = cn_ref[0]
    kpe_buf[slot, :, past:] = kn_ref[0]

    chunk = min(KEY_CHUNK, past)
    bounds = [(c * chunk, chunk) for c in range(past // chunk)] + [(past, NEW_PAD)]
    qt = lax.broadcasted_iota(jnp.int32, (n_rows, NEW_PAD), 0) % T_PAD
    kt = lax.broadcasted_iota(jnp.int32, (n_rows, NEW_PAD), 1)
    for r0, n in bounds:
        kb_ref[r0:r0 + n, :] = ckv_buf[slot, r0:r0 + n, :].astype(BF16)
    s_ref[...] = (lax.dot_general(q_lat, kb_ref[...], _NT, preferred_element_type=F32)
                  + jnp.dot(q_pe, kpe_buf[slot].astype(BF16), preferred_element_type=F32))
    s_ref[:, past:] = jnp.where((kt <= qt) & (kt < n_new), s_ref[:, past:], NEG)
    m = jnp.max(s_ref[...], axis=-1, keepdims=True)
    l = jnp.zeros((n_rows, 1), F32)
    for r0, n in bounds:
        p = jnp.exp2(s_ref[:, r0:r0 + n] - m)
        l = l + jnp.sum(p, axis=-1, keepdims=True)
        p_ref[:, r0:r0 + n] = p.astype(BF16)
    o = jnp.dot(p_ref[...], kb_ref[...], preferred_element_type=F32)
    o_lat = (o / l).astype(BF16)
    pv = jnp.dot(o_lat, wuv_ref[...], preferred_element_type=F32)
    out_row_h = lax.broadcasted_iota(jnp.int32, (n_rows, D_ATT), 0) // T_PAD
    out_col_h = lax.broadcasted_iota(jnp.int32, (n_rows, D_ATT), 1) // V_DIM
    pv = jnp.where(out_row_h == out_col_h, pv, 0.0)
    out = pv[0:T_PAD]
    for hh in range(1, MLA_HEADS):
        out = out + pv[hh * T_PAD:(hh + 1) * T_PAD]
    o_ref[0] = out.astype(BF16)


def _paged_attention(page_table, q8, ckv_new8, kpe_new_t, wq, wuv, cache_ckv, cache_kpe_t, n_new):
    nb, n_pages = page_table.shape
    rows = n_pages * PAGE_SIZE + NEW_PAD
    grid_spec = pltpu.PrefetchScalarGridSpec(
        num_scalar_prefetch=1,
        grid=(nb,),
        in_specs=[
            pl.BlockSpec((1, T_PAD, D_HEADS), lambda b, pt: (b, 0, 0)),
            pl.BlockSpec((1, T_PAD, KV_LORA), lambda b, pt: (b, 0, 0)),
            pl.BlockSpec((1, QK_ROPE, NEW_PAD), lambda b, pt: (b, 0, 0)),
            pl.BlockSpec(wq.shape, lambda b, pt: (0, 0)),
            pl.BlockSpec(wuv.shape, lambda b, pt: (0, 0)),
            pl.BlockSpec(memory_space=pl.ANY),
            pl.BlockSpec(memory_space=pl.ANY),
        ],
        out_specs=pl.BlockSpec((1, T_PAD, D_ATT), lambda b, pt: (b, 0, 0)),
        scratch_shapes=[
            pltpu.VMEM((2, rows, KV_LORA), F32),
            pltpu.VMEM((2, QK_ROPE, rows), F32),
            pltpu.VMEM((rows, KV_LORA), BF16),
            pltpu.VMEM((MLA_HEADS * T_PAD, rows), F32),
            pltpu.VMEM((MLA_HEADS * T_PAD, rows), BF16),
            pltpu.SemaphoreType.DMA((2, 2)),
        ],
    )
    return pl.pallas_call(
        functools.partial(_paged_kernel, n_pages=n_pages, n_new=n_new),
        grid_spec=grid_spec,
        out_shape=jax.ShapeDtypeStruct((nb, T_PAD, D_ATT), BF16),
        compiler_params=_params("arbitrary"),
        name="paged_attention",
    )(page_table, q8, ckv_new8, kpe_new_t, wq, wuv, cache_ckv, cache_kpe_t)


def _group_norm_gate(o, gate, gain, bias):
    mu = jnp.mean(o, axis=-1, keepdims=True)
    d = o - mu
    var = jnp.mean(d * d, axis=-1, keepdims=True)
    return _silu(gate) * (d * lax.rsqrt(var + LN_EPS) * gain + bias)


def _ret_kernel(rq_ref, rk_ref, rv_ref, rg_ref, din_ref, qd_ref, kd_ref, gc_ref, gg_ref, gb_ref, s0_ref,
                ro_ref, so_ref, s_ref, *, chunks):
    i = pl.program_id(1)

    @pl.when(i == 0)
    def _():
        s_ref[...] = s0_ref[0]

    units = [(c, hh) for c in range(chunks) for hh in range(RET_HEADS)]

    def tile(ref, c, hh):
        return ref[c * RET_CHUNK:(c + 1) * RET_CHUNK, hh * RET_DK:(hh + 1) * RET_DK]

    def lanes(ref, hh):
        return ref[:, hh * RET_DK:(hh + 1) * RET_DK]

    inner = {u: (lax.dot_general(tile(rq_ref, *u).astype(BF16), tile(rk_ref, *u).astype(BF16), _NT,
                                 preferred_element_type=F32) * lanes(din_ref, u[1])).astype(BF16)
             for u in units}
    kv = {u: lax.dot_general((tile(rk_ref, *u) * lanes(kd_ref, u[1])).astype(BF16), tile(rv_ref, *u), _TN,
                             preferred_element_type=F32)
          for u in units}
    o_intra = {u: jnp.dot(inner[u], tile(rv_ref, *u), preferred_element_type=F32) for u in units}
    states = [s_ref[hh] for hh in range(RET_HEADS)]
    for c, hh in units:
        s = states[hh]
        o = o_intra[c, hh] + jnp.dot((tile(rq_ref, c, hh) * lanes(qd_ref, hh)).astype(BF16), s.astype(BF16),
                                     preferred_element_type=F32)
        states[hh] = s * lanes(gc_ref, hh) + kv[c, hh]
        ro_ref[c * RET_CHUNK:(c + 1) * RET_CHUNK, hh * RET_DK:(hh + 1) * RET_DK] = _group_norm_gate(
            o, tile(rg_ref, c, hh), lanes(gg_ref, hh), lanes(gb_ref, hh)).astype(BF16)
    for hh in range(RET_HEADS):
        s_ref[hh] = states[hh]

    @pl.when(i == pl.num_programs(1) - 1)
    def _():
        so_ref[0] = s_ref[...]


def _retention_prompt(rq, rk, rv, rg, dec, gn_gain, gn_bias, state0, batch, seq, chunks):
    rows = chunks * RET_CHUNK
    steps = seq // rows
    rspec = pl.BlockSpec((rows, D_RET), lambda b, i: (b * steps + i, 0))
    sspec = pl.BlockSpec((1, RET_HEADS, RET_DK, RET_DV), lambda b, i: (b, 0, 0, 0))
    consts = [dec["din"], dec["qdec"], dec["kdec"], dec["gc"], gn_gain, gn_bias]
    return pl.pallas_call(
        functools.partial(_ret_kernel, chunks=chunks),
        grid=(batch, steps),
        in_specs=[rspec] * 4 + [_const_spec(c.shape) for c in consts] + [sspec],
        out_specs=[rspec, sspec],
        out_shape=[jax.ShapeDtypeStruct((batch * seq, D_RET), BF16),
                   jax.ShapeDtypeStruct((batch, RET_HEADS, RET_DK, RET_DV), F32)],
        scratch_shapes=[pltpu.VMEM((RET_HEADS, RET_DK, RET_DV), F32)],
        compiler_params=_params("parallel", "arbitrary"),
        name="retention_prompt",
    )(rq, rk, rv, rg, *consts, state0)


RS_BATCH = 8


def _ret_sample_kernel(q_ref, k_ref, v_ref, rg_ref, kt_ref, qd_ref, kd_ref, din_ref, gc_ref, gg_ref, gb_ref,
                       s0_ref, ro_ref, so_ref, *, n_tok):
    def one_seq(bi, carry):
        q8 = q_ref[bi]
        k8 = k_ref[bi]
        v8 = v_ref[bi]
        g8 = rg_ref[bi]
        qs = (q8 * qd_ref[...]).astype(BF16)
        heads = [slice(hh * RET_DK, (hh + 1) * RET_DK) for hh in range(RET_HEADS)]
        cross = [jnp.dot(qs[:, sl], s0_ref[bi, hh].astype(BF16), preferred_element_type=F32)
                 for hh, sl in enumerate(heads)]
        vbd = jnp.concatenate([v8 * kd_ref[hh] for hh in range(RET_HEADS)]
                              + [jnp.zeros((LANES - RET_HEADS * T_PAD, D_RET), F32)], axis=0)
        upd = jnp.dot(kt_ref[bi].astype(BF16), vbd.astype(BF16), preferred_element_type=F32)
        outs = []
        for hh, sl in enumerate(heads):
            o = cross[hh]
            din = din_ref[hh]
            for m in range(n_tok):
                a = jnp.sum(q8[:, sl] * k8[m:m + 1, sl], axis=-1, keepdims=True) * din[:, m:m + 1]
                o = o + a * v8[m:m + 1, sl]
            outs.append(_group_norm_gate(o, g8[:, sl], gg_ref[:, sl], gb_ref[:, sl]))
            so_ref[bi, hh] = s0_ref[bi, hh] * gc_ref[:, sl] + upd[:, sl]
        ro_ref[bi] = jnp.concatenate(outs, axis=1).astype(BF16)
        return carry

    lax.fori_loop(0, RS_BATCH, one_seq, 0)


def _retention_sample(q8, k8, v8, rg8, k_t, dec, gn_gain, gn_bias, state0, n_tok):
    nb = q8.shape[0]
    rspec = pl.BlockSpec((RS_BATCH, T_PAD, D_RET), lambda i: (i, 0, 0))
    sspec = pl.BlockSpec((RS_BATCH, RET_HEADS, RET_DK, RET_DV), lambda i: (i, 0, 0, 0))
    consts = [dec["qdec"], dec["kdec"], dec["din"], dec["gc"], gn_gain, gn_bias]
    return pl.pallas_call(
        functools.partial(_ret_sample_kernel, n_tok=n_tok),
        grid=(nb // RS_BATCH,),
        in_specs=[rspec] * 4 + [pl.BlockSpec((RS_BATCH, RET_DK, LANES), lambda i: (i, 0, 0))]
                 + [_const_spec(c.shape) for c in consts] + [sspec],
        out_specs=[rspec, sspec],
        out_shape=[jax.ShapeDtypeStruct((nb, T_PAD, D_RET), BF16),
                   jax.ShapeDtypeStruct((nb, RET_HEADS, RET_DK, RET_DV), F32)],
        compiler_params=_params("parallel"),
        name="retention_sample",
    )(q8, k8, v8, rg8, k_t, *consts, state0)


def _outproj_ln_kernel(a_ref, ro_ref, h_ref, wa_ref, wr_ref, g_ref, b_ref, o_ref):
    y = (jnp.dot(a_ref[...], wa_ref[...], preferred_element_type=F32)
         + jnp.dot(ro_ref[...], wr_ref[...], preferred_element_type=F32))
    o_ref[...] = _layer_norm(ALPHA * h_ref[...] + y, g_ref[...], b_ref[...])


def _outproj_ln(a, ro, h, wa, wr, g, b, tm):
    m = h.shape[0]

    def row(n):
        return pl.BlockSpec((tm, n), lambda i: (i, 0))

    return pl.pallas_call(
        _outproj_ln_kernel,
        grid=(m // tm,),
        in_specs=[row(D_ATT), row(D_RET), row(D_MODEL), _const_spec(wa.shape), _const_spec(wr.shape),
                  _const_spec(g.shape), _const_spec(b.shape)],
        out_specs=row(D_MODEL),
        out_shape=jax.ShapeDtypeStruct((m, D_MODEL), F32),
        compiler_params=_params("parallel"),
        name="outproj_ln",
    )(a, ro, h, wa, wr, g, b)


HALO = 16


def _pool_tail(pooled_groups, x, pw_ref, pb_ref, ps_ref, g_ref, b_ref):
    ys = [jnp.dot(p.astype(BF16), pw_ref[gi], preferred_element_type=F32) for gi, p in enumerate(pooled_groups)]
    y = (jnp.concatenate(ys, axis=-1) + pb_ref[...]) * ps_ref[...]
    return _layer_norm(ALPHA * x + y, g_ref[...], b_ref[...])


HALO_P = 32


def _pool_prompt_kernel(h_ref, halo_ref, pre_ref, pw_ref, pb_ref, ps_ref, g_ref, b_ref, o_ref,
                        xs_ref, a_ref, b2_ref, *, tm, tiles, start):
    t = pl.program_id(0) % tiles
    x = h_ref[...]
    xs_ref[0:HALO_P, :] = jnp.where(t == 0, pre_ref[0], halo_ref[...])
    xs_ref[HALO_P:, :] = x
    n = HALO_P + tm
    pos = start + t * tm + lax.broadcasted_iota(jnp.int32, (tm, 1), 0)
    pooled = []
    src = xs_ref
    for k, wl in enumerate(POOL_WINDOWS, start=1):
        lo = (k - 1) * POOL_GROUP
        r0, shift = 8 * k, wl // 2
        level = src[r0:n, lo:] + src[r0 - shift:n - shift, lo:]
        cnt = jnp.minimum(pos + 1, wl).astype(F32)
        pooled.append(level[HALO_P - r0:, :POOL_GROUP] / cnt - x[:, lo:lo + POOL_GROUP])
        if k < len(POOL_WINDOWS):
            dst = a_ref if k % 2 else b2_ref
            dst[r0:n, lo + POOL_GROUP:] = level[:, POOL_GROUP:]
            src = dst
    o_ref[...] = _pool_tail(pooled, x, pw_ref, pb_ref, ps_ref, g_ref, b_ref)


def _pool_prompt(h, prefix, pw, pb, ps, g, b, batch, seq, start, tm):
    assert all(w == 2 ** (i + 1) for i, w in enumerate(POOL_WINDOWS)) and 8 * len(POOL_WINDOWS) <= HALO_P
    tiles = seq // tm
    per = tm // HALO_P
    consts = [pw, pb, ps, g, b]
    buf = pltpu.VMEM((HALO_P + tm, D_MODEL), F32)
    return pl.pallas_call(
        functools.partial(_pool_prompt_kernel, tm=tm, tiles=tiles, start=start),
        grid=(batch * tiles,),
        in_specs=[pl.BlockSpec((tm, D_MODEL), lambda i: (i, 0)),
                  pl.BlockSpec((HALO_P, D_MODEL), lambda i: (jnp.maximum(i * per - 1, 0), 0)),
                  pl.BlockSpec((1, HALO_P, D_MODEL), lambda i: (i // tiles, 0, 0))]
                 + [_const_spec(c.shape) for c in consts],
        out_specs=pl.BlockSpec((tm, D_MODEL), lambda i: (i, 0)),
        out_shape=jax.ShapeDtypeStruct((batch * seq, D_MODEL), F32),
        scratch_shapes=[buf, buf, buf],
        compiler_params=_params("parallel"),
        name="pool_prompt",
    )(h, h, prefix, *consts)


def _pool_sample_kernel(xs_ref, pw_ref, pb_ref, ps_ref, g_ref, b_ref, o_ref, *, n_tok, start):
    nb = xs_ref.shape[1]
    x = jnp.concatenate([xs_ref[HALO + t] for t in range(n_tok)], axis=0)
    pooled = []
    for gi, wl in enumerate(POOL_WINDOWS):
        sl = slice(gi * POOL_GROUP, (gi + 1) * POOL_GROUP)
        parts = []
        for t in range(n_tok):
            acc = xs_ref[HALO + t, :, sl]
            for d in range(1, wl):
                acc = acc + xs_ref[HALO + t - d, :, sl]
            parts.append(acc / float(min(start + t + 1, wl)))
        pooled.append(jnp.concatenate(parts, axis=0) - x[:, sl])
    y = _pool_tail(pooled, x, pw_ref, pb_ref, ps_ref, g_ref, b_ref)
    for t in range(n_tok):
        o_ref[t] = y[t * nb:(t + 1) * nb]


def _pool_sample(xs_t, pw, pb, ps, g, b, n_tok, start):
    nb = xs_t.shape[1]
    args = [xs_t, pw, pb, ps, g, b]
    return pl.pallas_call(
        functools.partial(_pool_sample_kernel, n_tok=n_tok, start=start),
        grid=(1,),
        in_specs=[_const_spec(a.shape) for a in args],
        out_specs=_const_spec((n_tok, nb, D_MODEL)),
        out_shape=jax.ShapeDtypeStruct((n_tok, nb, D_MODEL), F32),
        compiler_params=_params("arbitrary"),
        name="pool_sample",
    )(*args)


def _head_pad(w, width):
    r, nh, d = w.shape
    out = jnp.zeros((r, nh, HEAD_PAD), w.dtype).at[:, :, :d].set(w)
    return out.reshape(r, nh * HEAD_PAD)[:, :width]


def _mixer_weights(mix_w_in, q_norm, kv_norm, w_uq, w_uk, w_uv, mix_w_out):
    offs = np.concatenate([[0], np.cumsum(SPLIT_SIZES)])
    wq, wckv, wkpe, wrq, wrk, wrv, wrg = [mix_w_in[:, offs[i]:offs[i + 1]] for i in range(7)]
    half = QK_ROPE // 2
    z_lo = jnp.zeros((D_MODEL, QK_NOPE), F32)
    z_hi = jnp.zeros((D_MODEL, HEAD_PAD - QK_NOPE - QK_ROPE), F32)
    kpe_blk = jnp.concatenate([z_lo, wkpe, z_hi], axis=1)
    kpe_swp = jnp.concatenate([z_lo, -wkpe[:, half:], wkpe[:, :half], z_hi], axis=1)
    w_main = jnp.concatenate([wq, wckv, kpe_blk, kpe_swp, wrq, wrk, wrv, wrg], axis=1).astype(BF16)
    pe = w_uq[:, :, QK_NOPE:]
    uq1 = _head_pad(w_uq, D_HEADS)
    uq2 = _head_pad(jnp.concatenate([jnp.zeros_like(w_uq[:, :, :QK_NOPE]), -pe[:, :, half:], pe[:, :, :half]], axis=2),
                    D_HEADS)
    wk = _head_pad(w_uk, D_HEADS)
    wv = _head_pad(w_uv, D_HEADS)
    wa = mix_w_out[:D_ATT]
    wr = mix_w_out[MLA_HEADS * V_DIM:]
    wuk_t = jnp.zeros((MLA_HEADS, HEAD_PAD, KV_LORA), F32).at[:, :QK_NOPE, :].set(
        jnp.transpose(w_uk, (1, 2, 0))).reshape(D_HEADS, KV_LORA)
    r = np.arange(D_HEADS) % HEAD_PAD
    e_pe = ((r[:, None] - QK_NOPE) == np.arange(LANES)[None, :]) & (r[:, None] >= QK_NOPE) & (r[:, None] < QK_NOPE + QK_ROPE)
    wq_abs = jnp.concatenate([wuk_t, jnp.asarray(e_pe.astype(np.float32))], axis=1)
    return {
        "w_main": w_main, "q_norm": q_norm[None, :], "kv_norm": kv_norm[None, :],
        "uq1": uq1.astype(BF16), "uq2": uq2.astype(BF16), "wk": wk.astype(BF16),
        "wv_c": w_uv.reshape(KV_LORA, D_ATT).astype(BF16),
        "wv_t": wv.T.astype(BF16), "wa": wa.astype(BF16), "wr": wr.astype(BF16),
        "wq_abs": wq_abs.astype(BF16),
    }


def _rope_tables(pos):
    def angles(r):
        inv = 1.0 / (ROPE_BASE ** (np.arange(0, r, 2, dtype=np.float64) / r))
        return pos.astype(np.float64)[:, None] * inv[None, :]

    n = pos.shape[0]
    a = angles(QK_ROPE)
    c, s = np.cos(a), np.sin(a)
    hi = HEAD_PAD - QK_NOPE - QK_ROPE
    cq = np.concatenate([np.ones((n, QK_NOPE)), c, c, np.ones((n, hi))], axis=1)
    sq = np.concatenate([np.zeros((n, QK_NOPE)), s, s, np.zeros((n, hi))], axis=1)
    a = angles(RET_DK)
    c, s = np.cos(a), np.sin(a)
    tabs = (cq, sq, np.concatenate([c, c], axis=1), np.concatenate([-s, s], axis=1))
    return tuple(jnp.asarray(t, F32) for t in tabs)


def _ret_log_decay():
    return jnp.log(1.0 - 2.0 ** (-5.0 - jnp.arange(RET_HEADS, dtype=F32)))


def _ret_decay_tables(chunk):
    log_g = _ret_log_decay()
    idx = jnp.arange(chunk, dtype=F32)
    diff = idx[:, None] - idx[None, :]
    d_in = jnp.where(diff >= 0, jnp.exp(jnp.maximum(diff, 0.0)[None] * log_g[:, None, None]), 0.0)
    q_dec = jnp.exp((idx + 1.0)[None, :] * log_g[:, None])
    k_dec = jnp.exp((chunk - 1.0 - idx)[None, :] * log_g[:, None])
    g_c = jnp.exp(chunk * log_g)
    lanes = (chunk, RET_HEADS * RET_DK)
    return {
        "din": jnp.transpose(d_in, (1, 0, 2)).reshape(chunk, RET_HEADS * chunk),
        "qdec": jnp.broadcast_to(q_dec.T[:, :, None], (chunk, RET_HEADS, RET_DK)).reshape(lanes),
        "kdec": jnp.broadcast_to(k_dec.T[:, :, None], (chunk, RET_HEADS, RET_DK)).reshape(lanes),
        "gc": jnp.broadcast_to(g_c[:, None], (RET_HEADS, RET_DV)).reshape(1, RET_HEADS * RET_DV),
    }


def _ret_sample_tables(n_tok):
    log_g = _ret_log_decay()
    idx = jnp.arange(n_tok, dtype=F32)
    diff = idx[:, None] - idx[None, :]
    d_in = jnp.where(diff >= 0, jnp.exp(jnp.maximum(diff, 0.0)[None] * log_g[:, None, None]), 0.0)
    q_dec = jnp.exp((idx + 1.0)[None, :] * log_g[:, None])
    k_dec = jnp.exp((n_tok - 1.0 - idx)[None, :] * log_g[:, None])
    g_c = jnp.exp(n_tok * log_g)
    pad_t = T_PAD - n_tok
    own_lanes = (np.arange(D_RET) // RET_DV)[None, None, :] == np.arange(RET_HEADS)[:, None, None]
    return {
        "qdec": jnp.pad(jnp.broadcast_to(q_dec.T[:, :, None], (n_tok, RET_HEADS, RET_DK)).reshape(n_tok, D_RET),
                        ((0, pad_t), (0, 0))),
        "kdec": jnp.pad(k_dec[:, :, None] * jnp.asarray(own_lanes, F32), ((0, 0), (0, pad_t), (0, 0))),
        "din": jnp.pad(d_in, ((0, 0), (0, pad_t), (0, LANES - n_tok))),
        "gc": jnp.broadcast_to(g_c[:, None], (RET_HEADS, RET_DV)).reshape(1, D_RET),
    }


FFN_ORDER = ((0, 0), (0, 1), (1, 0), (1, 1))


def _trunk(x, start, ret_state0, pool_prefix, mla_cache, w, mw):
    batch, seq, _ = x.shape
    m = batch * seq
    prompt = mla_cache is None
    tm = min(512, m)
    h = x.reshape(m, D_MODEL)

    h = yield h
    pos = start + (np.arange(seq) if prompt else np.arange(m) % seq)
    flash_blk = min(1024, seq) if prompt else tm
    q, k, vt, ckv, kpe, rq, rk, rv, rg = _mixer_prep(h, mw, _rope_tables(pos), tm, flash_blk)
    kpe = kpe[:, QK_NOPE:QK_NOPE + QK_ROPE]
    gn_gain = w["ret_gn_gain"][0][None, :]
    gn_bias = w["ret_gn_bias"][0][None, :]
    if prompt:
        a = _flash_attention(q, k, vt, batch, seq, flash_blk)
        chunk = RET_CHUNK if seq % RET_CHUNK == 0 else seq
        assert chunk == RET_CHUNK
        ro, ret_state = _retention_prompt(rq, rk, rv, rg, _ret_decay_tables(chunk), gn_gain, gn_bias,
                                          ret_state0, batch, seq, 8)
    else:
        assert seq <= T_PAD and seq % RET_CHUNK != 0
        cache_ckv, cache_kpe, page_table = mla_cache
        pad_t = ((0, 0), (0, T_PAD - seq), (0, 0))
        q8 = jnp.pad(q.reshape(batch, seq, D_HEADS), pad_t)
        cn8 = jnp.pad(ckv.reshape(batch, seq, KV_LORA), pad_t)
        kn_t = jnp.pad(jnp.swapaxes(kpe.reshape(batch, seq, QK_ROPE), 1, 2), ((0, 0), (0, 0), (0, NEW_PAD - seq)))
        a8 = _paged_attention(page_table, q8, cn8, kn_t, mw["wq_abs"], mw["wv_c"], cache_ckv,
                              jnp.swapaxes(cache_kpe, 1, 2), seq)
        a = a8[:, :seq].reshape(m, D_ATT)

        def rows8(t):
            return jnp.pad(t.reshape(batch, seq, D_RET), pad_t)

        k8 = rows8(rk)
        k_t = jnp.transpose(k8.reshape(batch, T_PAD, RET_HEADS, RET_DK), (0, 3, 2, 1)).reshape(
            batch, RET_DK, RET_HEADS * T_PAD)
        k_t = jnp.pad(k_t, ((0, 0), (0, 0), (0, LANES - RET_HEADS * T_PAD)))
        ro8, ret_state = _retention_sample(rows8(rq), k8, rows8(rv.astype(F32)), rows8(rg), k_t,
                                           _ret_sample_tables(seq), gn_gain, gn_bias, ret_state0, seq)
        ro = ro8[:, :seq].reshape(m, D_RET)
    h = _outproj_ln(a, ro, h, mw["wa"], mw["wr"], w["ln_gain"][0, 1][None, :], w["ln_bias"][0, 1][None, :], tm)
    h = yield h

    h = yield h
    xp_tail = jnp.concatenate([pool_prefix, h.reshape(batch, seq, D_MODEL)], axis=1)[:, -POOL_PREFIX:]
    halo = HALO_P if prompt else HALO
    prefix16 = jnp.pad(pool_prefix, ((0, 0), (halo - POOL_PREFIX, 0), (0, 0)))
    pool_args = (w["pool_w"], w["pool_b"][0][None, :], w["pool_scale"][0][None, :],
                 w["ln_gain"][1, 1][None, :], w["ln_bias"][1, 1][None, :])
    if prompt:
        h = _pool_prompt(h, prefix16, *pool_args, batch, seq, start, tm)
    else:
        xs_t = jnp.transpose(jnp.concatenate([prefix16, h.reshape(batch, seq, D_MODEL)], axis=1), (1, 0, 2))
        h = jnp.transpose(_pool_sample(xs_t, *pool_args, seq, start), (1, 0, 2)).reshape(m, D_MODEL)
    h = yield h
    return (h.reshape(batch, seq, D_MODEL), ckv.reshape(1, batch, seq, KV_LORA),
            kpe.reshape(1, batch, seq, QK_ROPE), ret_state[None], xp_tail[None])


def _finish(trunk, last):
    try:
        trunk.send(last)
    except StopIteration as done:
        return done.value
    raise AssertionError("trunk yielded more FFN requests than FFN_ORDER")


def kernel(x_prompt, x_sample, cache_mla_ckv, cache_mla_kpe, state_ret, state_pool, page_table, ffn_w_gate, ffn_w_up, ffn_w_down, ln_gain, ln_bias, mix_w_in, mla_q_norm, mla_kv_norm, mla_w_uq, mla_w_uk, mla_w_uv, ret_gn_gain, ret_gn_bias, mix_w_out, pool_w, pool_b, pool_scale):
    assert DEPTH == 2 and mix_w_in.shape[0] == 1 and pool_w.shape[0] == 1
    w = {
        "wg": ffn_w_gate, "wu": ffn_w_up, "wd": ffn_w_down,
        "ln_gain": ln_gain, "ln_bias": ln_bias, "ret_gn_gain": ret_gn_gain, "ret_gn_bias": ret_gn_bias,
        "pool_w": pool_w[0].astype(BF16), "pool_b": pool_b, "pool_scale": pool_scale,
    }
    mw = _mixer_weights(mix_w_in[0], mla_q_norm[0], mla_kv_norm[0], mla_w_uq[0], mla_w_uk[0], mla_w_uv[0],
                        mix_w_out[0])
    bp = x_prompt.shape[0]
    zero_ret = jnp.zeros((bp, RET_HEADS, RET_DK, RET_DV), F32)
    zero_pool = jnp.zeros((bp, POOL_PREFIX, D_MODEL), x_prompt.dtype)
    trunk_p = _trunk(x_prompt, 0, zero_ret, zero_pool, None, w, mw)
    trunk_s = _trunk(x_sample, PAST_LEN, state_ret[0], state_pool[0],
                     (cache_mla_ckv[0], cache_mla_kpe[0], page_table), w, mw)
    h_p, h_s = next(trunk_p), next(trunk_s)
    for n, (layer, half) in enumerate(FFN_ORDER):
        f_p, f_s = _ffn_ln(h_p, h_s, w["wg"], w["wu"], w["wd"], layer, half,
                           ln_gain[layer, 2 * half][None, :], ln_bias[layer, 2 * half][None, :])
        if n + 1 < len(FFN_ORDER):
            h_p, h_s = trunk_p.send(f_p), trunk_s.send(f_s)
    y_p, ckv_p, kpe_p, ret_p, pool_p = _finish(trunk_p, f_p)
    y_s, ckv_s, kpe_s, ret_s, pool_s = _finish(trunk_s, f_s)
    return (y_p, y_s, ckv_p, kpe_p, ckv_s, kpe_s, ret_p, ret_s, pool_p, pool_s)
```

```python
import functools

import numpy as np
import jax
import jax.numpy as jnp
from jax import lax
from jax.experimental import pallas as pl
from jax.experimental.pallas import tpu as pltpu

F32 = jnp.float32
BF16 = jnp.bfloat16

D_MODEL = 1024
DEPTH = 2
PAST_LEN = 8192
PAGE_SIZE = 128
ALPHA = (2 * DEPTH) ** 0.25
D_FF = 2816
MLA_HEADS = 8
Q_LORA = 512
KV_LORA = 256
QK_NOPE = 64
QK_ROPE = 32
V_DIM = 64
RET_HEADS = 4
RET_DK = 128
RET_DV = 128
RET_CHUNK = 128
POOL_WINDOWS = (2, 4, 8, 16)
POOL_GROUPS = 4
POOL_GROUP = D_MODEL // POOL_GROUPS
POOL_PREFIX = 15
ROPE_BASE = 10000.0
LN_EPS = 1e-5
RMS_EPS = 1e-6
SPLIT_SIZES = (Q_LORA, KV_LORA, QK_ROPE, RET_HEADS * RET_DK, RET_HEADS * RET_DK,
               RET_HEADS * RET_DV, RET_HEADS * RET_DV)
ATT_SCALE = (QK_NOPE + QK_ROPE) ** -0.5 * 1.4426950408889634

LANES = 128
HEAD_PAD = LANES
D_HEADS = MLA_HEADS * HEAD_PAD
D_ATT = MLA_HEADS * V_DIM
D_RET = RET_HEADS * RET_DV
NEG = -1e30
VMEM_LIMIT = 56 * 1024 * 1024

_NT = (((1,), (1,)), ((), ()))
_TN = (((0,), (0,)), ((), ()))


def _params(*sem):
    return pltpu.CompilerParams(dimension_semantics=sem, vmem_limit_bytes=VMEM_LIMIT)


def _const_spec(shape):
    nd = len(shape)
    return pl.BlockSpec(shape, lambda *_: (0,) * nd, pipeline_mode=pl.Buffered(1))


def _layer_norm(y, g, b):
    mu = jnp.mean(y, axis=-1, keepdims=True)
    d = y - mu
    var = jnp.mean(d * d, axis=-1, keepdims=True)
    return d * lax.rsqrt(var + LN_EPS) * g + b


def _silu(x):
    return x * jax.nn.sigmoid(x)


FFN_CHUNK = 256
FFN_ROWS = 512


def _ffn_ln_kernel(xp_ref, xs_ref, wg_ref, wu_ref, wd_ref, g_ref, b_ref, op_ref, os_ref, *, n_prompt):
    is_prompt = pl.program_id(0) < n_prompt

    def half_step(x_ref, o_ref):
        x = x_ref[...]
        xb = x.astype(BF16)
        acc = None
        for c in range(D_FF // FFN_CHUNK):
            sl = slice(c * FFN_CHUNK, (c + 1) * FFN_CHUNK)
            g = jnp.dot(xb, wg_ref[:, sl].astype(BF16), preferred_element_type=F32)
            u = jnp.dot(xb, wu_ref[:, sl].astype(BF16), preferred_element_type=F32)
            a = (_silu(g) * u).astype(BF16)
            d = jnp.dot(a, wd_ref[sl, :].astype(BF16), preferred_element_type=F32)
            acc = d if acc is None else acc + d
        o_ref[...] = _layer_norm(ALPHA * x + 0.5 * acc, g_ref[...], b_ref[...])

    pl.when(is_prompt)(functools.partial(half_step, xp_ref, op_ref))
    pl.when(jnp.logical_not(is_prompt))(functools.partial(half_step, xs_ref, os_ref))


def _ffn_ln(xp, xs, wg, wu, wd, layer, half, g, b):
    tm = FFN_ROWS
    assert xp.shape[0] % tm == 0 and xs.shape[0] % tm == 0
    n_p, n_s = xp.shape[0] // tm, xs.shape[0] // tm
    pspec = pl.BlockSpec((tm, D_MODEL), lambda i: (jnp.minimum(i, n_p - 1), 0))
    sspec = pl.BlockSpec((tm, D_MODEL), lambda i: (jnp.maximum(i - n_p, 0), 0))

    def wspec(w):
        return pl.BlockSpec((None, None) + w.shape[2:], lambda i: (layer, half, 0, 0),
                            pipeline_mode=pl.Buffered(1))

    return pl.pallas_call(
        functools.partial(_ffn_ln_kernel, n_prompt=n_p),
        grid=(n_p + n_s,),
        in_specs=[pspec, sspec, wspec(wg), wspec(wu), wspec(wd), _const_spec(g.shape), _const_spec(b.shape)],
        out_specs=[pspec, sspec],
        out_shape=[jax.ShapeDtypeStruct(xp.shape, F32), jax.ShapeDtypeStruct(xs.shape, F32)],
        compiler_params=_params("arbitrary"),
        name="ffn_ln",
    )(xp, xs, wg, wu, wd, g, b)


_C_QL, _C_CKV, _C_KPE, _C_KPS, _C_RQ, _C_RK, _C_RV, _C_RG, _C_END = (
    0, 512, 768, 896, 1024, 1536, 2048, 2560, 3072)


def _prep_kernel(h_ref, w_ref, qn_ref, kvn_ref, uq1_ref, uq2_ref, wk_ref, wvt_ref,
                 cq_ref, sq_ref, cr_ref, sr_ref,
                 q_ref, k_ref, vt_ref, ckv_ref, kpe_ref, rq_ref, rk_ref, rv_ref, rg_ref):
    xb = h_ref[...].astype(BF16)

    def proj(a, b):
        return jnp.dot(xb, w_ref[:, a:b], preferred_element_type=F32)

    cq = cq_ref[...]
    sq = sq_ref[...]
    ql = proj(_C_QL, _C_CKV)
    c = proj(_C_CKV, _C_KPE)
    kp = proj(_C_KPE, _C_RQ)
    rq = proj(_C_RQ, _C_RK)
    rk = proj(_C_RK, _C_RV)
    rv_ref[...] = proj(_C_RV, _C_RG).astype(BF16)
    rg_ref[...] = proj(_C_RG, _C_END)
    qn = (ql * lax.rsqrt(jnp.mean(ql * ql, axis=-1, keepdims=True) + RMS_EPS) * qn_ref[...]).astype(BF16)
    ckv = c * lax.rsqrt(jnp.mean(c * c, axis=-1, keepdims=True) + RMS_EPS) * kvn_ref[...]
    ckv_ref[...] = ckv
    cb = ckv.astype(BF16)
    kpe = kp[:, :HEAD_PAD] * cq + kp[:, HEAD_PAD:] * sq
    kpe_ref[...] = kpe
    qa = jnp.dot(qn, uq1_ref[...], preferred_element_type=F32)
    qb = jnp.dot(qn, uq2_ref[...], preferred_element_type=F32)
    kn = jnp.dot(cb, wk_ref[...], preferred_element_type=F32)
    vt = lax.dot_general(wvt_ref[...], cb, _NT, preferred_element_type=F32)
    for hh in range(MLA_HEADS):
        sl = slice(hh * HEAD_PAD, (hh + 1) * HEAD_PAD)
        q_ref[:, sl] = ((qa[:, sl] * cq + qb[:, sl] * sq) * ATT_SCALE).astype(BF16)
        k_ref[:, sl] = (kn[:, sl] + kpe).astype(BF16)
    head_row = lax.broadcasted_iota(jnp.int32, vt.shape, 0) % HEAD_PAD
    vt_ref[0] = jnp.where(head_row == V_DIM, 1.0, vt).astype(BF16)
    cr = cr_ref[...]
    sr = sr_ref[...]
    for hh in range(RET_HEADS):
        sl = slice(hh * RET_DK, (hh + 1) * RET_DK)
        xq = rq[:, sl]
        xk = rk[:, sl]
        rq_ref[:, sl] = xq * cr + pltpu.roll(xq, RET_DK // 2, 1) * sr
        rk_ref[:, sl] = (xk * cr + pltpu.roll(xk, RET_DK // 2, 1) * sr) * (RET_DK ** -0.5)


def _mixer_prep(h, mw, tabs, tm, vt_blk):
    m = h.shape[0]
    cq, sq, cr, sr = tabs
    tab_blocks = cq.shape[0] // tm
    per = vt_blk // tm

    def row(n):
        return pl.BlockSpec((tm, n), lambda i: (i, 0))

    tab = pl.BlockSpec((tm, LANES), lambda i: (i % tab_blocks, 0))
    consts = [mw["w_main"], mw["q_norm"], mw["kv_norm"], mw["uq1"], mw["uq2"], mw["wk"], mw["wv_t"]]
    out_shape = [
        jax.ShapeDtypeStruct((m, D_HEADS), BF16),
        jax.ShapeDtypeStruct((m, D_HEADS), BF16),
        jax.ShapeDtypeStruct((m // vt_blk, D_HEADS, vt_blk), BF16),
        jax.ShapeDtypeStruct((m, KV_LORA), F32),
        jax.ShapeDtypeStruct((m, LANES), F32),
        jax.ShapeDtypeStruct((m, D_RET), F32),
        jax.ShapeDtypeStruct((m, D_RET), F32),
        jax.ShapeDtypeStruct((m, D_RET), BF16),
        jax.ShapeDtypeStruct((m, D_RET), F32),
    ]
    return pl.pallas_call(
        _prep_kernel,
        grid=(m // tm,),
        in_specs=[row(D_MODEL)] + [_const_spec(c.shape) for c in consts] + [tab] * 4,
        out_specs=[pl.BlockSpec((1, D_HEADS, tm), lambda i: (i // per, 0, i % per)) if len(s.shape) == 3
                   else row(s.shape[1]) for s in out_shape],
        out_shape=out_shape,
        compiler_params=_params("parallel"),
        name="mixer_prep",
    )(h, *consts, cq, sq, cr, sr)


FLASH_HEADS = 2
FLASH_KEYS = 256


def _flash_kernel(q_ref, k_ref, vt_ref, o_ref, *, blk):
    i = pl.program_id(2)
    lanes = [slice(a * HEAD_PAD, (a + 1) * HEAD_PAD) for a in range(FLASH_HEADS)]
    qs = [q_ref[:, sl] for sl in lanes]

    pieces = [(k0, a) for k0 in range(0, blk, FLASH_KEYS) for a in range(FLASH_HEADS)]

    def step(j, ms, accs, masked):
        base = pl.multiple_of(j * blk, blk)
        sts = []
        for k0, a in pieces:
            q0 = k0 if masked else 0
            st = lax.dot_general(k_ref[pl.ds(base + k0, FLASH_KEYS), lanes[a]], qs[a][q0:], _NT,
                                 preferred_element_type=F32)
            if masked:
                key = lax.broadcasted_iota(jnp.int32, st.shape, 0)
                qry = lax.broadcasted_iota(jnp.int32, st.shape, 1)
                st = jnp.where(key <= qry, st, NEG)
            sts.append(st)
        ms, accs = list(ms), list(accs)
        for (k0, a), st in zip(pieces, sts):
            q0 = k0 if masked else 0
            m_old, acc_old = ms[a][:, q0:], accs[a][:, q0:]
            m_new = jnp.maximum(m_old, jnp.max(st, axis=0, keepdims=True))
            p = jnp.exp2(st - m_new).astype(BF16)
            acc_new = (acc_old * jnp.exp2(m_old - m_new)
                       + jnp.dot(vt_ref[j, lanes[a], k0:k0 + FLASH_KEYS], p, preferred_element_type=F32))
            ms[a] = jnp.concatenate([ms[a][:, :q0], m_new], axis=1) if q0 else m_new
            accs[a] = jnp.concatenate([accs[a][:, :q0], acc_new], axis=1) if q0 else acc_new
        return tuple(ms), tuple(accs)

    m0 = tuple(jnp.full((1, blk), NEG, F32) for _ in lanes)
    acc0 = tuple(jnp.zeros((HEAD_PAD, blk), F32) for _ in lanes)
    ms, accs = lax.fori_loop(0, i, lambda j, c: step(j, *c, False), (m0, acc0))
    ms, accs = step(i, ms, accs, True)
    outs = []
    for acc in accs:
        outs.append((acc / acc[V_DIM:V_DIM + 1, :])[:V_DIM])
    o_ref[...] = jnp.concatenate(outs, axis=0).T.astype(BF16)


def _flash_attention(q, k, vt, batch, seq, blk):
    nq = seq // blk
    width = FLASH_HEADS * HEAD_PAD
    qspec = pl.BlockSpec((blk, width), lambda b, h, i: (b * nq + i, h))
    kspec = pl.BlockSpec((seq, width), lambda b, h, i: (b, h))
    vspec = pl.BlockSpec((nq, width, blk), lambda b, h, i: (b, h, 0))
    return pl.pallas_call(
        functools.partial(_flash_kernel, blk=blk),
        grid=(batch, MLA_HEADS // FLASH_HEADS, nq),
        in_specs=[qspec, kspec, vspec],
        out_specs=pl.BlockSpec((blk, FLASH_HEADS * V_DIM), lambda b, h, i: (b * nq + i, h)),
        out_shape=jax.ShapeDtypeStruct((batch * seq, D_ATT), BF16),
        compiler_params=_params("parallel", "parallel", "arbitrary"),
        name="flash_attention",
    )(q, k, vt)


T_PAD = 8
NEW_PAD = PAGE_SIZE
KEY_CHUNK = 1024


def _paged_kernel(pt_ref, q_ref, cn_ref, kn_ref, wq_ref, wuv_ref, ckv_hbm, kpe_hbm, o_ref,
                  ckv_buf, kpe_buf, kb_ref, s_ref, p_ref, sem, *, n_pages, n_new, tq):
    b = pl.program_id(0)
    nb = pl.num_programs(0)
    past = n_pages * PAGE_SIZE
    slot = b % 2

    def page_copies(bb, sl, p):
        page = pt_ref[bb, p]
        rows = pl.ds(p * PAGE_SIZE, PAGE_SIZE)
        return (pltpu.make_async_copy(ckv_hbm.at[page], ckv_buf.at[sl, rows, :], sem.at[sl, 0]),
                pltpu.make_async_copy(kpe_hbm.at[page], kpe_buf.at[sl, :, rows], sem.at[sl, 1]))

    def start_fetch(bb, sl):
        for p in range(n_pages):
            for cp in page_copies(bb, sl, p):
                cp.start()

    def wait_fetch(bb, sl):
        for p in range(n_pages):
            for cp in page_copies(bb, sl, p):
                cp.wait()

    @pl.when(b == 0)
    def _():
        ckv_buf[:, past:, :] = jnp.zeros((2, NEW_PAD, KV_LORA), F32)
        start_fetch(0, 0)

    @pl.when(b + 1 < nb)
    def _():
        start_fetch(b + 1, 1 - slot)

    qrep = q_ref[0]
    n_rows = MLA_HEADS * tq
    row_h = lax.broadcasted_iota(jnp.int32, (n_rows, D_HEADS), 0) // tq
    col_h = lax.broadcasted_iota(jnp.int32, (n_rows, D_HEADS), 1) // HEAD_PAD
    qm = jnp.where(row_h == col_h, qrep, jnp.zeros_like(qrep))
    ql = jnp.dot(qm, wq_ref[...], preferred_element_type=F32)
    q_lat = ql[:, :KV_LORA].astype(BF16)
    q_pe = ql[:, KV_LORA:KV_LORA + QK_ROPE].astype(BF16)

    wait_fetch(b, slot)
    ckv_buf[slot, past:past + T_PAD, :] = cn_ref[0]
    kpe_buf[slot, :, past:] = kn_ref[0]

    chunk = min(KEY_CHUNK, past)
    bounds = [(c * chunk, chunk) for c in range(past // chunk)] + [(past, NEW_PAD)]
    qt = lax.broadcasted_iota(jnp.int32, (n_rows, NEW_PAD), 0) % tq
    kt = lax.broadcasted_iota(jnp.int32, (n_rows, NEW_PAD), 1)
    for r0, n in bounds:
        kb_ref[r0:r0 + n, :] = ckv_buf[slot, r0:r0 + n, :].astype(BF16)
    s_ref[...] = (lax.dot_general(q_lat, kb_ref[...], _NT, preferred_element_type=F32)
                  + jnp.dot(q_pe, kpe_buf[slot].astype(BF16), preferred_element_type=F32))
    s_ref[:, past:] = jnp.where((kt <= qt) & (kt < n_new), s_ref[:, past:], NEG)
    m = jnp.max(s_ref[...], axis=-1, keepdims=True)
    l = jnp.zeros((n_rows, 1), F32)
    for r0, n in bounds:
        p = jnp.exp2(s_ref[:, r0:r0 + n] - m)
        l = l + jnp.sum(p, axis=-1, keepdims=True)
        p_ref[:, r0:r0 + n] = p.astype(BF16)
    o = jnp.dot(p_ref[...], kb_ref[...], preferred_element_type=F32)
    o_lat = (o / l).astype(BF16)
    pv = jnp.dot(o_lat, wuv_ref[...], preferred_element_type=F32)
    out_row_h = lax.broadcasted_iota(jnp.int32, (n_rows, D_ATT), 0) // tq
    out_col_h = lax.broadcasted_iota(jnp.int32, (n_rows, D_ATT), 1) // V_DIM
    pv = jnp.where(out_row_h == out_col_h, pv, 0.0).astype(BF16)
    fold_t = lax.broadcasted_iota(jnp.int32, (T_PAD, n_rows), 0)
    fold_r = lax.broadcasted_iota(jnp.int32, (T_PAD, n_rows), 1) % tq
    fold = jnp.where(fold_t == fold_r, 1.0, 0.0).astype(BF16)
    o_ref[0] = jnp.dot(fold, pv, preferred_element_type=F32).astype(BF16)


def _paged_attention(page_table, q_rep, ckv_new8, kpe_new_t, wq, wuv, cache_ckv, cache_kpe_t, n_new):
    nb, n_pages = page_table.shape
    rows = n_pages * PAGE_SIZE + NEW_PAD
    n_rows = q_rep.shape[1]
    grid_spec = pltpu.PrefetchScalarGridSpec(
        num_scalar_prefetch=1,
        grid=(nb,),
        in_specs=[
            pl.BlockSpec((1, n_rows, D_HEADS), lambda b, pt: (b, 0, 0)),
            pl.BlockSpec((1, T_PAD, KV_LORA), lambda b, pt: (b, 0, 0)),
            pl.BlockSpec((1, QK_ROPE, NEW_PAD), lambda b, pt: (b, 0, 0)),
            pl.BlockSpec(wq.shape, lambda b, pt: (0, 0)),
            pl.BlockSpec(wuv.shape, lambda b, pt: (0, 0)),
            pl.BlockSpec(memory_space=pl.ANY),
            pl.BlockSpec(memory_space=pl.ANY),
        ],
        out_specs=pl.BlockSpec((1, T_PAD, D_ATT), lambda b, pt: (b, 0, 0)),
        scratch_shapes=[
            pltpu.VMEM((2, rows, KV_LORA), F32),
            pltpu.VMEM((2, QK_ROPE, rows), F32),
            pltpu.VMEM((rows, KV_LORA), BF16),
            pltpu.VMEM((n_rows, rows), F32),
            pltpu.VMEM((n_rows, rows), BF16),
            pltpu.SemaphoreType.DMA((2, 2)),
        ],
    )
    return pl.pallas_call(
        functools.partial(_paged_kernel, n_pages=n_pages, n_new=n_new, tq=n_rows // MLA_HEADS),
        grid_spec=grid_spec,
        out_shape=jax.ShapeDtypeStruct((nb, T_PAD, D_ATT), BF16),
        compiler_params=_params("arbitrary"),
        name="paged_attention",
    )(page_table, q_rep, ckv_new8, kpe_new_t, wq, wuv, cache_ckv, cache_kpe_t)


def _group_norm_gate(o, gate, gain, bias):
    mu = jnp.mean(o, axis=-1, keepdims=True)
    d = o - mu
    var = jnp.mean(d * d, axis=-1, keepdims=True)
    return _silu(gate) * (d * lax.rsqrt(var + LN_EPS) * gain + bias)


def _ret_kernel(rq_ref, rk_ref, rv_ref, rg_ref, din_ref, qd_ref, kd_ref, gc_ref, gg_ref, gb_ref, s0_ref,
                ro_ref, so_ref, s_ref, *, chunks):
    i = pl.program_id(1)

    @pl.when(i == 0)
    def _():
        s_ref[...] = s0_ref[0]

    units = [(c, hh) for c in range(chunks) for hh in range(RET_HEADS)]

    def tile(ref, c, hh):
        return ref[c * RET_CHUNK:(c + 1) * RET_CHUNK, hh * RET_DK:(hh + 1) * RET_DK]

    def lanes(ref, hh):
        return ref[:, hh * RET_DK:(hh + 1) * RET_DK]

    inner = {u: (lax.dot_general(tile(rq_ref, *u).astype(BF16), tile(rk_ref, *u).astype(BF16), _NT,
                                 preferred_element_type=F32) * lanes(din_ref, u[1])).astype(BF16)
             for u in units}
    kv = {u: lax.dot_general((tile(rk_ref, *u) * lanes(kd_ref, u[1])).astype(BF16), tile(rv_ref, *u), _TN,
                             preferred_element_type=F32)
          for u in units}
    o_intra = {u: jnp.dot(inner[u], tile(rv_ref, *u), preferred_element_type=F32) for u in units}
    states = [s_ref[hh] for hh in range(RET_HEADS)]
    for c, hh in units:
        s = states[hh]
        o = o_intra[c, hh] + jnp.dot((tile(rq_ref, c, hh) * lanes(qd_ref, hh)).astype(BF16), s.astype(BF16),
                                     preferred_element_type=F32)
        states[hh] = s * lanes(gc_ref, hh) + kv[c, hh]
        ro_ref[c * RET_CHUNK:(c + 1) * RET_CHUNK, hh * RET_DK:(hh + 1) * RET_DK] = _group_norm_gate(
            o, tile(rg_ref, c, hh), lanes(gg_ref, hh), lanes(gb_ref, hh)).astype(BF16)
    for hh in range(RET_HEADS):
        s_ref[hh] = states[hh]

    @pl.when(i == pl.num_programs(1) - 1)
    def _():
        so_ref[0] = s_ref[...]


def _retention_prompt(rq, rk, rv, rg, dec, gn_gain, gn_bias, state0, batch, seq, chunks):
    rows = chunks * RET_CHUNK
    steps = seq // rows
    rspec = pl.BlockSpec((rows, D_RET), lambda b, i: (b * steps + i, 0))
    sspec = pl.BlockSpec((1, RET_HEADS, RET_DK, RET_DV), lambda b, i: (b, 0, 0, 0))
    consts = [dec["din"], dec["qdec"], dec["kdec"], dec["gc"], gn_gain, gn_bias]
    return pl.pallas_call(
        functools.partial(_ret_kernel, chunks=chunks),
        grid=(batch, steps),
        in_specs=[rspec] * 4 + [_const_spec(c.shape) for c in consts] + [sspec],
        out_specs=[rspec, sspec],
        out_shape=[jax.ShapeDtypeStruct((batch * seq, D_RET), BF16),
                   jax.ShapeDtypeStruct((batch, RET_HEADS, RET_DK, RET_DV), F32)],
        scratch_shapes=[pltpu.VMEM((RET_HEADS, RET_DK, RET_DV), F32)],
        compiler_params=_params("parallel", "arbitrary"),
        name="retention_prompt",
    )(rq, rk, rv, rg, *consts, state0)


RS_BATCH = 8


def _ret_sample_kernel(q_ref, k_ref, v_ref, rg_ref, kt_ref, qd_ref, kd_ref, din_ref, gc_ref, gg_ref, gb_ref,
                       s0_ref, ro_ref, so_ref, *, n_tok):
    def one_seq(bi, carry):
        q8 = q_ref[bi]
        k8 = k_ref[bi]
        v8 = v_ref[bi]
        g8 = rg_ref[bi]
        qs = (q8 * qd_ref[...]).astype(BF16)
        heads = [slice(hh * RET_DK, (hh + 1) * RET_DK) for hh in range(RET_HEADS)]
        cross = [jnp.dot(qs[:, sl], s0_ref[bi, hh].astype(BF16), preferred_element_type=F32)
                 for hh, sl in enumerate(heads)]
        vbd = jnp.concatenate([v8 * kd_ref[hh] for hh in range(RET_HEADS)]
                              + [jnp.zeros((LANES - RET_HEADS * T_PAD, D_RET), F32)], axis=0)
        upd = jnp.dot(kt_ref[bi].astype(BF16), vbd.astype(BF16), preferred_element_type=F32)
        outs = []
        for hh, sl in enumerate(heads):
            o = cross[hh]
            din = din_ref[hh]
            for m in range(n_tok):
                a = jnp.sum(q8[:, sl] * k8[m:m + 1, sl], axis=-1, keepdims=True) * din[:, m:m + 1]
                o = o + a * v8[m:m + 1, sl]
            outs.append(_group_norm_gate(o, g8[:, sl], gg_ref[:, sl], gb_ref[:, sl]))
            so_ref[bi, hh] = s0_ref[bi, hh] * gc_ref[:, sl] + upd[:, sl]
        ro_ref[bi] = jnp.concatenate(outs, axis=1).astype(BF16)
        return carry

    lax.fori_loop(0, RS_BATCH, one_seq, 0)


def _retention_sample(q8, k8, v8, rg8, k_t, dec, gn_gain, gn_bias, state0, n_tok):
    nb = q8.shape[0]
    rspec = pl.BlockSpec((RS_BATCH, T_PAD, D_RET), lambda i: (i, 0, 0))
    sspec = pl.BlockSpec((RS_BATCH, RET_HEADS, RET_DK, RET_DV), lambda i: (i, 0, 0, 0))
    consts = [dec["qdec"], dec["kdec"], dec["din"], dec["gc"], gn_gain, gn_bias]
    return pl.pallas_call(
        functools.partial(_ret_sample_kernel, n_tok=n_tok),
        grid=(nb // RS_BATCH,),
        in_specs=[rspec] * 4 + [pl.BlockSpec((RS_BATCH, RET_DK, LANES), lambda i: (i, 0, 0))]
                 + [_const_spec(c.shape) for c in consts] + [sspec],
        out_specs=[rspec, sspec],
        out_shape=[jax.ShapeDtypeStruct((nb, T_PAD, D_RET), BF16),
                   jax.ShapeDtypeStruct((nb, RET_HEADS, RET_DK, RET_DV), F32)],
        compiler_params=_params("parallel"),
        name="retention_sample",
    )(q8, k8, v8, rg8, k_t, *consts, state0)


def _outproj_ln_kernel(a_ref, ro_ref, h_ref, wa_ref, wr_ref, g_ref, b_ref, o_ref):
    y = (jnp.dot(a_ref[...], wa_ref[...], preferred_element_type=F32)
         + jnp.dot(ro_ref[...], wr_ref[...], preferred_element_type=F32))
    o_ref[...] = _layer_norm(ALPHA * h_ref[...] + y, g_ref[...], b_ref[...])


def _outproj_ln(a, ro, h, wa, wr, g, b, tm):
    m = h.shape[0]

    def row(n):
        return pl.BlockSpec((tm, n), lambda i: (i, 0))

    return pl.pallas_call(
        _outproj_ln_kernel,
        grid=(m // tm,),
        in_specs=[row(D_ATT), row(D_RET), row(D_MODEL), _const_spec(wa.shape), _const_spec(wr.shape),
                  _const_spec(g.shape), _const_spec(b.shape)],
        out_specs=row(D_MODEL),
        out_shape=jax.ShapeDtypeStruct((m, D_MODEL), F32),
        compiler_params=_params("parallel"),
        name="outproj_ln",
    )(a, ro, h, wa, wr, g, b)


HALO = 16


def _pool_tail(pooled_groups, x, pw_ref, pb_ref, ps_ref, g_ref, b_ref):
    ys = [jnp.dot(p.astype(BF16), pw_ref[gi], preferred_element_type=F32) for gi, p in enumerate(pooled_groups)]
    y = (jnp.concatenate(ys, axis=-1) + pb_ref[...]) * ps_ref[...]
    return _layer_norm(ALPHA * x + y, g_ref[...], b_ref[...])


HALO_P = 32


def _pool_prompt_kernel(h_ref, halo_ref, pre_ref, pw_ref, pb_ref, ps_ref, g_ref, b_ref, o_ref,
                        xs_ref, a_ref, b2_ref, *, tm, tiles, start):
    t = pl.program_id(0) % tiles
    x = h_ref[...]
    xs_ref[0:HALO_P, :] = jnp.where(t == 0, pre_ref[0], halo_ref[...])
    xs_ref[HALO_P:, :] = x
    n = HALO_P + tm
    pos = start + t * tm + lax.broadcasted_iota(jnp.int32, (tm, 1), 0)
    pooled = []
    src = xs_ref
    for k, wl in enumerate(POOL_WINDOWS, start=1):
        lo = (k - 1) * POOL_GROUP
        r0, shift = 8 * k, wl // 2
        level = src[r0:n, lo:] + src[r0 - shift:n - shift, lo:]
        cnt = jnp.minimum(pos + 1, wl).astype(F32)
        pooled.append(level[HALO_P - r0:, :POOL_GROUP] / cnt - x[:, lo:lo + POOL_GROUP])
        if k < len(POOL_WINDOWS):
            dst = a_ref if k % 2 else b2_ref
            dst[r0:n, lo + POOL_GROUP:] = level[:, POOL_GROUP:]
            src = dst
    o_ref[...] = _pool_tail(pooled, x, pw_ref, pb_ref, ps_ref, g_ref, b_ref)


def _pool_prompt(h, prefix, pw, pb, ps, g, b, batch, seq, start, tm):
    assert all(w == 2 ** (i + 1) for i, w in enumerate(POOL_WINDOWS)) and 8 * len(POOL_WINDOWS) <= HALO_P
    tiles = seq // tm
    per = tm // HALO_P
    consts = [pw, pb, ps, g, b]
    buf = pltpu.VMEM((HALO_P + tm, D_MODEL), F32)
    return pl.pallas_call(
        functools.partial(_pool_prompt_kernel, tm=tm, tiles=tiles, start=start),
        grid=(batch * tiles,),
        in_specs=[pl.BlockSpec((tm, D_MODEL), lambda i: (i, 0)),
                  pl.BlockSpec((HALO_P, D_MODEL), lambda i: (jnp.maximum(i * per - 1, 0), 0)),
                  pl.BlockSpec((1, HALO_P, D_MODEL), lambda i: (i // tiles, 0, 0))]
                 + [_const_spec(c.shape) for c in consts],
        out_specs=pl.BlockSpec((tm, D_MODEL), lambda i: (i, 0)),
        out_shape=jax.ShapeDtypeStruct((batch * seq, D_MODEL), F32),
        scratch_shapes=[buf, buf, buf],
        compiler_params=_params("parallel"),
        name="pool_prompt",
    )(h, h, prefix, *consts)


def _pool_sample_kernel(xs_ref, pw_ref, pb_ref, ps_ref, g_ref, b_ref, o_ref, *, n_tok, start):
    nb = xs_ref.shape[1]
    x = jnp.concatenate([xs_ref[HALO + t] for t in range(n_tok)], axis=0)
    pooled = []
    for gi, wl in enumerate(POOL_WINDOWS):
        sl = slice(gi * POOL_GROUP, (gi + 1) * POOL_GROUP)
        parts = []
        for t in range(n_tok):
            acc = xs_ref[HALO + t, :, sl]
            for d in range(1, wl):
                acc = acc + xs_ref[HALO + t - d, :, sl]
            parts.append(acc / float(min(start + t + 1, wl)))
        pooled.append(jnp.concatenate(parts, axis=0) - x[:, sl])
    y = _pool_tail(pooled, x, pw_ref, pb_ref, ps_ref, g_ref, b_ref)
    for t in range(n_tok):
        o_ref[t] = y[t * nb:(t + 1) * nb]


def _pool_sample(xs_t, pw, pb, ps, g, b, n_tok, start):
    nb = xs_t.shape[1]
    args = [xs_t, pw, pb, ps, g, b]
    return pl.pallas_call(
        functools.partial(_pool_sample_kernel, n_tok=n_tok, start=start),
        grid=(1,),
        in_specs=[_const_spec(a.shape) for a in args],
        out_specs=_const_spec((n_tok, nb, D_MODEL)),
        out_shape=jax.ShapeDtypeStruct((n_tok, nb, D_MODEL), F32),
        compiler_params=_params("arbitrary"),
        name="pool_sample",
    )(*args)


def _head_pad(w, width):
    r, nh, d = w.shape
    out = jnp.zeros((r, nh, HEAD_PAD), w.dtype).at[:, :, :d].set(w)
    return out.reshape(r, nh * HEAD_PAD)[:, :width]


def _mixer_weights(mix_w_in, q_norm, kv_norm, w_uq, w_uk, w_uv, mix_w_out):
    offs = np.concatenate([[0], np.cumsum(SPLIT_SIZES)])
    wq, wckv, wkpe, wrq, wrk, wrv, wrg = [mix_w_in[:, offs[i]:offs[i + 1]] for i in range(7)]
    half = QK_ROPE // 2
    z_lo = jnp.zeros((D_MODEL, QK_NOPE), F32)
    z_hi = jnp.zeros((D_MODEL, HEAD_PAD - QK_NOPE - QK_ROPE), F32)
    kpe_blk = jnp.concatenate([z_lo, wkpe, z_hi], axis=1)
    kpe_swp = jnp.concatenate([z_lo, -wkpe[:, half:], wkpe[:, :half], z_hi], axis=1)
    w_main = jnp.concatenate([wq, wckv, kpe_blk, kpe_swp, wrq, wrk, wrv, wrg], axis=1).astype(BF16)
    pe = w_uq[:, :, QK_NOPE:]
    uq1 = _head_pad(w_uq, D_HEADS)
    uq2 = _head_pad(jnp.concatenate([jnp.zeros_like(w_uq[:, :, :QK_NOPE]), -pe[:, :, half:], pe[:, :, :half]], axis=2),
                    D_HEADS)
    wk = _head_pad(w_uk, D_HEADS)
    wv = _head_pad(w_uv, D_HEADS)
    wa = mix_w_out[:D_ATT]
    wr = mix_w_out[MLA_HEADS * V_DIM:]
    wuk_t = jnp.zeros((MLA_HEADS, HEAD_PAD, KV_LORA), F32).at[:, :QK_NOPE, :].set(
        jnp.transpose(w_uk, (1, 2, 0))).reshape(D_HEADS, KV_LORA)
    r = np.arange(D_HEADS) % HEAD_PAD
    e_pe = ((r[:, None] - QK_NOPE) == np.arange(LANES)[None, :]) & (r[:, None] >= QK_NOPE) & (r[:, None] < QK_NOPE + QK_ROPE)
    wq_abs = jnp.concatenate([wuk_t, jnp.asarray(e_pe.astype(np.float32))], axis=1)
    return {
        "w_main": w_main, "q_norm": q_norm[None, :], "kv_norm": kv_norm[None, :],
        "uq1": uq1.astype(BF16), "uq2": uq2.astype(BF16), "wk": wk.astype(BF16),
        "wv_c": w_uv.reshape(KV_LORA, D_ATT).astype(BF16),
        "wv_t": wv.T.astype(BF16), "wa": wa.astype(BF16), "wr": wr.astype(BF16),
        "wq_abs": wq_abs.astype(BF16),
    }


def _rope_tables(pos):
    def angles(r):
        inv = 1.0 / (ROPE_BASE ** (np.arange(0, r, 2, dtype=np.float64) / r))
        return pos.astype(np.float64)[:, None] * inv[None, :]

    n = pos.shape[0]
    a = angles(QK_ROPE)
    c, s = np.cos(a), np.sin(a)
    hi = HEAD_PAD - QK_NOPE - QK_ROPE
    cq = np.concatenate([np.ones((n, QK_NOPE)), c, c, np.ones((n, hi))], axis=1)
    sq = np.concatenate([np.zeros((n, QK_NOPE)), s, s, np.zeros((n, hi))], axis=1)
    a = angles(RET_DK)
    c, s = np.cos(a), np.sin(a)
    tabs = (cq, sq, np.concatenate([c, c], axis=1), np.concatenate([-s, s], axis=1))
    return tuple(jnp.asarray(t, F32) for t in tabs)


def _ret_log_decay():
    return jnp.log(1.0 - 2.0 ** (-5.0 - jnp.arange(RET_HEADS, dtype=F32)))


def _ret_decay_tables(chunk):
    log_g = _ret_log_decay()
    idx = jnp.arange(chunk, dtype=F32)
    diff = idx[:, None] - idx[None, :]
    d_in = jnp.where(diff >= 0, jnp.exp(jnp.maximum(diff, 0.0)[None] * log_g[:, None, None]), 0.0)
    q_dec = jnp.exp((idx + 1.0)[None, :] * log_g[:, None])
    k_dec = jnp.exp((chunk - 1.0 - idx)[None, :] * log_g[:, None])
    g_c = jnp.exp(chunk * log_g)
    lanes = (chunk, RET_HEADS * RET_DK)
    return {
        "din": jnp.transpose(d_in, (1, 0, 2)).reshape(chunk, RET_HEADS * chunk),
        "qdec": jnp.broadcast_to(q_dec.T[:, :, None], (chunk, RET_HEADS, RET_DK)).reshape(lanes),
        "kdec": jnp.broadcast_to(k_dec.T[:, :, None], (chunk, RET_HEADS, RET_DK)).reshape(lanes),
        "gc": jnp.broadcast_to(g_c[:, None], (RET_HEADS, RET_DV)).reshape(1, RET_HEADS * RET_DV),
    }


def _ret_sample_tables(n_tok):
    log_g = _ret_log_decay()
    idx = jnp.arange(n_tok, dtype=F32)
    diff = idx[:, None] - idx[None, :]
    d_in = jnp.where(diff >= 0, jnp.exp(jnp.maximum(diff, 0.0)[None] * log_g[:, None, None]), 0.0)
    q_dec = jnp.exp((idx + 1.0)[None, :] * log_g[:, None])
    k_dec = jnp.exp((n_tok - 1.0 - idx)[None, :] * log_g[:, None])
    g_c = jnp.exp(n_tok * log_g)
    pad_t = T_PAD - n_tok
    own_lanes = (np.arange(D_RET) // RET_DV)[None, None, :] == np.arange(RET_HEADS)[:, None, None]
    return {
        "qdec": jnp.pad(jnp.broadcast_to(q_dec.T[:, :, None], (n_tok, RET_HEADS, RET_DK)).reshape(n_tok, D_RET),
                        ((0, pad_t), (0, 0))),
        "kdec": jnp.pad(k_dec[:, :, None] * jnp.asarray(own_lanes, F32), ((0, 0), (0, pad_t), (0, 0))),
        "din": jnp.pad(d_in, ((0, 0), (0, pad_t), (0, LANES - n_tok))),
        "gc": jnp.broadcast_to(g_c[:, None], (RET_HEADS, RET_DV)).reshape(1, D_RET),
    }


FFN_ORDER = ((0, 0), (0, 1), (1, 0), (1, 1))


def _trunk(x, start, ret_state0, pool_prefix, mla_cache, w, mw):
    batch, seq, _ = x.shape
    m = batch * seq
    prompt = mla_cache is None
    tm = min(512, m)
    h = x.reshape(m, D_MODEL)

    h = yield h
    pos = start + (np.arange(seq) if prompt else np.arange(m) % seq)
    flash_blk = min(1024, seq) if prompt else tm
    q, k, vt, ckv, kpe, rq, rk, rv, rg = _mixer_prep(h, mw, _rope_tables(pos), tm, flash_blk)
    kpe = kpe[:, QK_NOPE:QK_NOPE + QK_ROPE]
    gn_gain = w["ret_gn_gain"][0][None, :]
    gn_bias = w["ret_gn_bias"][0][None, :]
    if prompt:
        a = _flash_attention(q, k, vt, batch, seq, flash_blk)
        chunk = RET_CHUNK if seq % RET_CHUNK == 0 else seq
        assert chunk == RET_CHUNK
        ro, ret_state = _retention_prompt(rq, rk, rv, rg, _ret_decay_tables(chunk), gn_gain, gn_bias,
                                          ret_state0, batch, seq, 8)
    else:
        assert seq <= T_PAD and seq % RET_CHUNK != 0
        cache_ckv, cache_kpe, page_table = mla_cache
        pad_t = ((0, 0), (0, T_PAD - seq), (0, 0))
        tq = seq if (MLA_HEADS * seq) % 16 == 0 else T_PAD
        q8 = jnp.tile(jnp.pad(q.reshape(batch, seq, D_HEADS), ((0, 0), (0, tq - seq), (0, 0))),
                      (1, MLA_HEADS, 1))
        cn8 = jnp.pad(ckv.reshape(batch, seq, KV_LORA), pad_t)
        kn_t = jnp.pad(jnp.swapaxes(kpe.reshape(batch, seq, QK_ROPE), 1, 2), ((0, 0), (0, 0), (0, NEW_PAD - seq)))
        a8 = _paged_attention(page_table, q8, cn8, kn_t, mw["wq_abs"], mw["wv_c"], cache_ckv,
                              jnp.swapaxes(cache_kpe, 1, 2), seq)
        a = a8[:, :seq].reshape(m, D_ATT)

        def rows8(t):
            return jnp.pad(t.reshape(batch, seq, D_RET), pad_t)

        k8 = rows8(rk)
        k_t = jnp.transpose(k8.reshape(batch, T_PAD, RET_HEADS, RET_DK), (0, 3, 2, 1)).reshape(
            batch, RET_DK, RET_HEADS * T_PAD)
        k_t = jnp.pad(k_t, ((0, 0), (0, 0), (0, LANES - RET_HEADS * T_PAD)))
        ro8, ret_state = _retention_sample(rows8(rq), k8, rows8(rv.astype(F32)), rows8(rg), k_t,
                                           _ret_sample_tables(seq), gn_gain, gn_bias, ret_state0, seq)
        ro = ro8[:, :seq].reshape(m, D_RET)
    h = _outproj_ln(a, ro, h, mw["wa"], mw["wr"], w["ln_gain"][0, 1][None, :], w["ln_bias"][0, 1][None, :], tm)
    h = yield h

    h = yield h
    xp_tail = jnp.concatenate([pool_prefix, h.reshape(batch, seq, D_MODEL)], axis=1)[:, -POOL_PREFIX:]
    halo = HALO_P if prompt else HALO
    prefix16 = jnp.pad(pool_prefix, ((0, 0), (halo - POOL_PREFIX, 0), (0, 0)))
    pool_args = (w["pool_w"], w["pool_b"][0][None, :], w["pool_scale"][0][None, :],
                 w["ln_gain"][1, 1][None, :], w["ln_bias"][1, 1][None, :])
    if prompt:
        h = _pool_prompt(h, prefix16, *pool_args, batch, seq, start, tm)
    else:
        xs_t = jnp.transpose(jnp.concatenate([prefix16, h.reshape(batch, seq, D_MODEL)], axis=1), (1, 0, 2))
        h = jnp.transpose(_pool_sample(xs_t, *pool_args, seq, start), (1, 0, 2)).reshape(m, D_MODEL)
    h = yield h
    return (h.reshape(batch, seq, D_MODEL), ckv.reshape(1, batch, seq, KV_LORA),
            kpe.reshape(1, batch, seq, QK_ROPE), ret_state[None], xp_tail[None])


def _finish(trunk, last):
    try:
        trunk.send(last)
    except StopIteration as done:
        return done.value
    raise AssertionError("trunk yielded more FFN requests than FFN_ORDER")


def kernel(x_prompt, x_sample, cache_mla_ckv, cache_mla_kpe, state_ret, state_pool, page_table, ffn_w_gate, ffn_w_up, ffn_w_down, ln_gain, ln_bias, mix_w_in, mla_q_norm, mla_kv_norm, mla_w_uq, mla_w_uk, mla_w_uv, ret_gn_gain, ret_gn_bias, mix_w_out, pool_w, pool_b, pool_scale):
    assert DEPTH == 2 and mix_w_in.shape[0] == 1 and pool_w.shape[0] == 1
    w = {
        "wg": ffn_w_gate, "wu": ffn_w_up, "wd": ffn_w_down,
        "ln_gain": ln_gain, "ln_bias": ln_bias, "ret_gn_gain": ret_gn_gain, "ret_gn_bias": ret_gn_bias,
        "pool_w": pool_w[0].astype(BF16), "pool_b": pool_b, "pool_scale": pool_scale,
    }
    mw = _mixer_weights(mix_w_in[0], mla_q_norm[0], mla_kv_norm[0], mla_w_uq[0], mla_w_uk[0], mla_w_uv[0],
                        mix_w_out[0])
    bp = x_prompt.shape[0]
    zero_ret = jnp.zeros((bp, RET_HEADS, RET_DK, RET_DV), F32)
    zero_pool = jnp.zeros((bp, POOL_PREFIX, D_MODEL), x_prompt.dtype)
    trunk_p = _trunk(x_prompt, 0, zero_ret, zero_pool, None, w, mw)
    trunk_s = _trunk(x_sample, PAST_LEN, state_ret[0], state_pool[0],
                     (cache_mla_ckv[0], cache_mla_kpe[0], page_table), w, mw)
    h_p, h_s = next(trunk_p), next(trunk_s)
    for n, (layer, half) in enumerate(FFN_ORDER):
        f_p, f_s = _ffn_ln(h_p, h_s, w["wg"], w["wu"], w["wd"], layer, half,
                           ln_gain[layer, 2 * half][None, :], ln_bias[layer, 2 * half][None, :])
        if n + 1 < len(FFN_ORDER):
            h_p, h_s = trunk_p.send(f_p), trunk_s.send(f_s)
    y_p, ckv_p, kpe_p, ret_p, pool_p = _finish(trunk_p, f_p)
    y_s, ckv_s, kpe_s, ret_s, pool_s = _finish(trunk_s, f_s)
    return (y_p, y_s, ckv_p, kpe_p, ckv_s, kpe_s, ret_p, ret_s, pool_p, pool_s)
```

```python
import functools

import numpy as np
import jax
import jax.numpy as jnp
from jax import lax
from jax.experimental import pallas as pl
from jax.experimental.pallas import tpu as pltpu

F32 = jnp.float32
BF16 = jnp.bfloat16

D_MODEL = 1024
DEPTH = 2
PAST_LEN = 8192
PAGE_SIZE = 128
ALPHA = (2 * DEPTH) ** 0.25
D_FF = 2816
MLA_HEADS = 8
Q_LORA = 512
KV_LORA = 256
QK_NOPE = 64
QK_ROPE = 32
V_DIM = 64
RET_HEADS = 4
RET_DK = 128
RET_DV = 128
RET_CHUNK = 128
POOL_WINDOWS = (2, 4, 8, 16)
POOL_GROUPS = 4
POOL_GROUP = D_MODEL // POOL_GROUPS
POOL_PREFIX = 15
ROPE_BASE = 10000.0
LN_EPS = 1e-5
RMS_EPS = 1e-6
SPLIT_SIZES = (Q_LORA, KV_LORA, QK_ROPE, RET_HEADS * RET_DK, RET_HEADS * RET_DK,
               RET_HEADS * RET_DV, RET_HEADS * RET_DV)
ATT_SCALE = (QK_NOPE + QK_ROPE) ** -0.5 * 1.4426950408889634

LANES = 128
HEAD_PAD = LANES
D_HEADS = MLA_HEADS * HEAD_PAD
D_ATT = MLA_HEADS * V_DIM
D_RET = RET_HEADS * RET_DV
NEG = -1e30
VMEM_LIMIT = 56 * 1024 * 1024

_NT = (((1,), (1,)), ((), ()))
_TN = (((0,), (0,)), ((), ()))


def _params(*sem):
    return pltpu.CompilerParams(dimension_semantics=sem, vmem_limit_bytes=VMEM_LIMIT)


def _const_spec(shape):
    nd = len(shape)
    return pl.BlockSpec(shape, lambda *_: (0,) * nd, pipeline_mode=pl.Buffered(1))


def _layer_norm(y, g, b):
    mu = jnp.mean(y, axis=-1, keepdims=True)
    d = y - mu
    var = jnp.mean(d * d, axis=-1, keepdims=True)
    return d * lax.rsqrt(var + LN_EPS) * g + b


def _silu(x):
    return x * jax.nn.sigmoid(x)


FFN_CHUNK = 256
FFN_ROWS = 512


def _ffn_ln_kernel(xp_ref, xs_ref, wg_ref, wu_ref, wd_ref, g_ref, b_ref, op_ref, os_ref, *, n_prompt):
    is_prompt = pl.program_id(0) < n_prompt

    def half_step(x_ref, o_ref):
        x = x_ref[...]
        xb = x.astype(BF16)
        acc = None
        for c in range(D_FF // FFN_CHUNK):
            sl = slice(c * FFN_CHUNK, (c + 1) * FFN_CHUNK)
            g = jnp.dot(xb, wg_ref[:, sl].astype(BF16), preferred_element_type=F32)
            u = jnp.dot(xb, wu_ref[:, sl].astype(BF16), preferred_element_type=F32)
            a = (_silu(g) * u).astype(BF16)
            d = jnp.dot(a, wd_ref[sl, :].astype(BF16), preferred_element_type=F32)
            acc = d if acc is None else acc + d
        o_ref[...] = _layer_norm(ALPHA * x + 0.5 * acc, g_ref[...], b_ref[...])

    pl.when(is_prompt)(functools.partial(half_step, xp_ref, op_ref))
    pl.when(jnp.logical_not(is_prompt))(functools.partial(half_step, xs_ref, os_ref))


def _ffn_ln(xp, xs, wg, wu, wd, layer, half, g, b):
    tm = FFN_ROWS
    assert xp.shape[0] % tm == 0 and xs.shape[0] % tm == 0
    n_p, n_s = xp.shape[0] // tm, xs.shape[0] // tm
    pspec = pl.BlockSpec((tm, D_MODEL), lambda i: (jnp.minimum(i, n_p - 1), 0))
    sspec = pl.BlockSpec((tm, D_MODEL), lambda i: (jnp.maximum(i - n_p, 0), 0))

    def wspec(w):
        return pl.BlockSpec((None, None) + w.shape[2:], lambda i: (layer, half, 0, 0),
                            pipeline_mode=pl.Buffered(1))

    return pl.pallas_call(
        functools.partial(_ffn_ln_kernel, n_prompt=n_p),
        grid=(n_p + n_s,),
        in_specs=[pspec, sspec, wspec(wg), wspec(wu), wspec(wd), _const_spec(g.shape), _const_spec(b.shape)],
        out_specs=[pspec, sspec],
        out_shape=[jax.ShapeDtypeStruct(xp.shape, F32), jax.ShapeDtypeStruct(xs.shape, F32)],
        compiler_params=_params("arbitrary"),
        name="ffn_ln",
    )(xp, xs, wg, wu, wd, g, b)


_C_QL, _C_CKV, _C_KPE, _C_KPS, _C_RQ, _C_RK, _C_RV, _C_RG, _C_END = (
    0, 512, 768, 896, 1024, 1536, 2048, 2560, 3072)


def _prep_kernel(h_ref, w_ref, qn_ref, kvn_ref, uq1_ref, uq2_ref, wk_ref, wvt_ref,
                 cq_ref, sq_ref, cr_ref, sr_ref,
                 q_ref, k_ref, vt_ref, ckv_ref, kpe_ref, rq_ref, rk_ref, rv_ref, rg_ref):
    xb = h_ref[...].astype(BF16)

    def proj(a, b):
        return jnp.dot(xb, w_ref[:, a:b], preferred_element_type=F32)

    cq = cq_ref[...]
    sq = sq_ref[...]
    ql = proj(_C_QL, _C_CKV)
    c = proj(_C_CKV, _C_KPE)
    kp = proj(_C_KPE, _C_RQ)
    rq = proj(_C_RQ, _C_RK)
    rk = proj(_C_RK, _C_RV)
    rv_ref[...] = proj(_C_RV, _C_RG).astype(BF16)
    rg_ref[...] = proj(_C_RG, _C_END)
    qn = (ql * lax.rsqrt(jnp.mean(ql * ql, axis=-1, keepdims=True) + RMS_EPS) * qn_ref[...]).astype(BF16)
    ckv = c * lax.rsqrt(jnp.mean(c * c, axis=-1, keepdims=True) + RMS_EPS) * kvn_ref[...]
    ckv_ref[...] = ckv
    cb = ckv.astype(BF16)
    kpe = kp[:, :HEAD_PAD] * cq + kp[:, HEAD_PAD:] * sq
    kpe_ref[...] = kpe
    qa = jnp.dot(qn, uq1_ref[...], preferred_element_type=F32)
    qb = jnp.dot(qn, uq2_ref[...], preferred_element_type=F32)
    kn = jnp.dot(cb, wk_ref[...], preferred_element_type=F32)
    vt = lax.dot_general(wvt_ref[...], cb, _NT, preferred_element_type=F32)
    for hh in range(MLA_HEADS):
        sl = slice(hh * HEAD_PAD, (hh + 1) * HEAD_PAD)
        q_ref[:, sl] = ((qa[:, sl] * cq + qb[:, sl] * sq) * ATT_SCALE).astype(BF16)
        k_ref[:, sl] = (kn[:, sl] + kpe).astype(BF16)
    head_row = lax.broadcasted_iota(jnp.int32, vt.shape, 0) % HEAD_PAD
    vt_ref[0] = jnp.where(head_row == V_DIM, 1.0, vt).astype(BF16)
    cr = cr_ref[...]
    sr = sr_ref[...]
    for hh in range(RET_HEADS):
        sl = slice(hh * RET_DK, (hh + 1) * RET_DK)
        xq = rq[:, sl]
        xk = rk[:, sl]
        rq_ref[:, sl] = xq * cr + pltpu.roll(xq, RET_DK // 2, 1) * sr
        rk_ref[:, sl] = (xk * cr + pltpu.roll(xk, RET_DK // 2, 1) * sr) * (RET_DK ** -0.5)


def _mixer_prep(h, mw, tabs, tm, vt_blk):
    m = h.shape[0]
    cq, sq, cr, sr = tabs
    tab_blocks = cq.shape[0] // tm
    per = vt_blk // tm

    def row(n):
        return pl.BlockSpec((tm, n), lambda i: (i, 0))

    tab = pl.BlockSpec((tm, LANES), lambda i: (i % tab_blocks, 0))
    consts = [mw["w_main"], mw["q_norm"], mw["kv_norm"], mw["uq1"], mw["uq2"], mw["wk"], mw["wv_t"]]
    out_shape = [
        jax.ShapeDtypeStruct((m, D_HEADS), BF16),
        jax.ShapeDtypeStruct((m, D_HEADS), BF16),
        jax.ShapeDtypeStruct((m // vt_blk, D_HEADS, vt_blk), BF16),
        jax.ShapeDtypeStruct((m, KV_LORA), F32),
        jax.ShapeDtypeStruct((m, LANES), F32),
        jax.ShapeDtypeStruct((m, D_RET), F32),
        jax.ShapeDtypeStruct((m, D_RET), F32),
        jax.ShapeDtypeStruct((m, D_RET), BF16),
        jax.ShapeDtypeStruct((m, D_RET), F32),
    ]
    return pl.pallas_call(
        _prep_kernel,
        grid=(m // tm,),
        in_specs=[row(D_MODEL)] + [_const_spec(c.shape) for c in consts] + [tab] * 4,
        out_specs=[pl.BlockSpec((1, D_HEADS, tm), lambda i: (i // per, 0, i % per)) if len(s.shape) == 3
                   else row(s.shape[1]) for s in out_shape],
        out_shape=out_shape,
        compiler_params=_params("parallel"),
        name="mixer_prep",
    )(h, *consts, cq, sq, cr, sr)


FLASH_HEADS = 2
FLASH_KEYS = 256


def _flash_kernel(q_ref, k_ref, vt_ref, o_ref, *, blk):
    i = pl.program_id(2)
    lanes = [slice(a * HEAD_PAD, (a + 1) * HEAD_PAD) for a in range(FLASH_HEADS)]
    qs = [q_ref[:, sl] for sl in lanes]

    pieces = [(k0, a) for k0 in range(0, blk, FLASH_KEYS) for a in range(FLASH_HEADS)]

    def step(j, ms, accs, masked):
        base = pl.multiple_of(j * blk, blk)
        sts = []
        for k0, a in pieces:
            q0 = k0 if masked else 0
            st = lax.dot_general(k_ref[pl.ds(base + k0, FLASH_KEYS), lanes[a]], qs[a][q0:], _NT,
                                 preferred_element_type=F32)
            if masked:
                key = lax.broadcasted_iota(jnp.int32, st.shape, 0)
                qry = lax.broadcasted_iota(jnp.int32, st.shape, 1)
                st = jnp.where(key <= qry, st, NEG)
            sts.append(st)
        ms, accs = list(ms), list(accs)
        for (k0, a), st in zip(pieces, sts):
            q0 = k0 if masked else 0
            m_old, acc_old = ms[a][:, q0:], accs[a][:, q0:]
            m_new = jnp.maximum(m_old, jnp.max(st, axis=0, keepdims=True))
            p = jnp.exp2(st - m_new).astype(BF16)
            acc_new = (acc_old * jnp.exp2(m_old - m_new)
                       + jnp.dot(vt_ref[j, lanes[a], k0:k0 + FLASH_KEYS], p, preferred_element_type=F32))
            ms[a] = jnp.concatenate([ms[a][:, :q0], m_new], axis=1) if q0 else m_new
            accs[a] = jnp.concatenate([accs[a][:, :q0], acc_new], axis=1) if q0 else acc_new
        return tuple(ms), tuple(accs)

    m0 = tuple(jnp.full((1, blk), NEG, F32) for _ in lanes)
    acc0 = tuple(jnp.zeros((HEAD_PAD, blk), F32) for _ in lanes)
    ms, accs = lax.fori_loop(0, i, lambda j, c: step(j, *c, False), (m0, acc0))
    ms, accs = step(i, ms, accs, True)
    outs = []
    for acc in accs:
        outs.append((acc / acc[V_DIM:V_DIM + 1, :])[:V_DIM])
    o_ref[...] = jnp.concatenate(outs, axis=0).T.astype(BF16)


def _flash_attention(q, k, vt, batch, seq, blk):
    nq = seq // blk
    width = FLASH_HEADS * HEAD_PAD
    qspec = pl.BlockSpec((blk, width), lambda b, h, i: (b * nq + i, h))
    kspec = pl.BlockSpec((seq, width), lambda b, h, i: (b, h))
    vspec = pl.BlockSpec((nq, width, blk), lambda b, h, i: (b, h, 0))
    return pl.pallas_call(
        functools.partial(_flash_kernel, blk=blk),
        grid=(batch, MLA_HEADS // FLASH_HEADS, nq),
        in_specs=[qspec, kspec, vspec],
        out_specs=pl.BlockSpec((blk, FLASH_HEADS * V_DIM), lambda b, h, i: (b * nq + i, h)),
        out_shape=jax.ShapeDtypeStruct((batch * seq, D_ATT), BF16),
        compiler_params=_params("parallel", "parallel", "arbitrary"),
        name="flash_attention",
    )(q, k, vt)


T_PAD = 8
NEW_PAD = PAGE_SIZE
KEY_CHUNK = 1024


def _paged_kernel(pt_ref, q_ref, cn_ref, kn_ref, wq_ref, wuv_ref, ckv_hbm, kpe_hbm, o_ref,
                  ckv_buf, kpe_buf, kb_ref, s_ref, p_ref, sem, *, n_pages, n_new, tq):
    b = pl.program_id(0)
    nb = pl.num_programs(0)
    past = n_pages * PAGE_SIZE
    slot = b % 2

    def page_copies(bb, sl, p):
        page = pt_ref[bb, p]
        rows = pl.ds(p * PAGE_SIZE, PAGE_SIZE)
        return (pltpu.make_async_copy(ckv_hbm.at[page], ckv_buf.at[sl, rows, :], sem.at[sl, 0]),
                pltpu.make_async_copy(kpe_hbm.at[page], kpe_buf.at[sl, :, rows], sem.at[sl, 1]))

    def start_fetch(bb, sl):
        for p in range(n_pages):
            for cp in page_copies(bb, sl, p):
                cp.start()

    def wait_fetch(bb, sl):
        for p in range(n_pages):
            for cp in page_copies(bb, sl, p):
                cp.wait()

    @pl.when(b == 0)
    def _():
        ckv_buf[:, past:, :] = jnp.zeros((2, NEW_PAD, KV_LORA), F32)
        start_fetch(0, 0)

    @pl.when(b + 1 < nb)
    def _():
        start_fetch(b + 1, 1 - slot)

    qrep = q_ref[0]
    n_rows = MLA_HEADS * tq
    row_h = lax.broadcasted_iota(jnp.int32, (n_rows, D_HEADS), 0) // tq
    col_h = lax.broadcasted_iota(jnp.int32, (n_rows, D_HEADS), 1) // HEAD_PAD
    qm = jnp.where(row_h == col_h, qrep, jnp.zeros_like(qrep))
    ql = jnp.dot(qm, wq_ref[...], preferred_element_type=F32)
    q_lat = ql[:, :KV_LORA].astype(BF16)
    q_pe = ql[:, KV_LORA:KV_LORA + QK_ROPE].astype(BF16)

    wait_fetch(b, slot)
    ckv_buf[slot, past:past + T_PAD, :] = cn_ref[0]
    kpe_buf[slot, :, past:] = kn_ref[0]

    chunk = min(KEY_CHUNK, past)
    bounds = [(c * chunk, chunk) for c in range(past // chunk)] + [(past, NEW_PAD)]
    qt = lax.broadcasted_iota(jnp.int32, (n_rows, NEW_PAD), 0) % tq
    kt = lax.broadcasted_iota(jnp.int32, (n_rows, NEW_PAD), 1)
    for r0, n in bounds:
        kb_ref[r0:r0 + n, :] = ckv_buf[slot, r0:r0 + n, :].astype(BF16)
    s_ref[...] = (lax.dot_general(q_lat, kb_ref[...], _NT, preferred_element_type=F32)
                  + jnp.dot(q_pe, kpe_buf[slot].astype(BF16), preferred_element_type=F32))
    s_ref[:, past:] = jnp.where((kt <= qt) & (kt < n_new), s_ref[:, past:], NEG)
    m = jnp.max(s_ref[...], axis=-1, keepdims=True)
    l = jnp.zeros((n_rows, 1), F32)
    for r0, n in bounds:
        p = jnp.exp2(s_ref[:, r0:r0 + n] - m)
        l = l + jnp.sum(p, axis=-1, keepdims=True)
        p_ref[:, r0:r0 + n] = p.astype(BF16)
    o = jnp.dot(p_ref[...], kb_ref[...], preferred_element_type=F32)
    o_lat = (o / l).astype(BF16)
    pv = jnp.dot(o_lat, wuv_ref[...], preferred_element_type=F32)
    out_row_h = lax.broadcasted_iota(jnp.int32, (n_rows, D_ATT), 0) // tq
    out_col_h = lax.broadcasted_iota(jnp.int32, (n_rows, D_ATT), 1) // V_DIM
    pv = jnp.where(out_row_h == out_col_h, pv, 0.0).astype(BF16)
    fold_t = lax.broadcasted_iota(jnp.int32, (T_PAD, n_rows), 0)
    fold_r = lax.broadcasted_iota(jnp.int32, (T_PAD, n_rows), 1) % tq
    fold = jnp.where(fold_t == fold_r, 1.0, 0.0).astype(BF16)
    o_ref[0] = jnp.dot(fold, pv, preferred_element_type=F32).astype(BF16)


def _paged_attention(page_table, q_rep, ckv_new8, kpe_new_t, wq, wuv, cache_ckv, cache_kpe_t, n_new):
    nb, n_pages = page_table.shape
    rows = n_pages * PAGE_SIZE + NEW_PAD
    n_rows = q_rep.shape[1]
    grid_spec = pltpu.PrefetchScalarGridSpec(
        num_scalar_prefetch=1,
        grid=(nb,),
        in_specs=[
            pl.BlockSpec((1, n_rows, D_HEADS), lambda b, pt: (b, 0, 0)),
            pl.BlockSpec((1, T_PAD, KV_LORA), lambda b, pt: (b, 0, 0)),
            pl.BlockSpec((1, QK_ROPE, NEW_PAD), lambda b, pt: (b, 0, 0)),
            pl.BlockSpec(wq.shape, lambda b, pt: (0, 0)),
            pl.BlockSpec(wuv.shape, lambda b, pt: (0, 0)),
            pl.BlockSpec(memory_space=pl.ANY),
            pl.BlockSpec(memory_space=pl.ANY),
        ],
        out_specs=pl.BlockSpec((1, T_PAD, D_ATT), lambda b, pt: (b, 0, 0)),
        scratch_shapes=[
            pltpu.VMEM((2, rows, KV_LORA), F32),
            pltpu.VMEM((2, QK_ROPE, rows), F32),
            pltpu.VMEM((rows, KV_LORA), BF16),
            pltpu.VMEM((n_rows, rows), F32),
            pltpu.VMEM((n_rows, rows), BF16),
            pltpu.SemaphoreType.DMA((2, 2)),
        ],
    )
    return pl.pallas_call(
        functools.partial(_paged_kernel, n_pages=n_pages, n_new=n_new, tq=n_rows // MLA_HEADS),
        grid_spec=grid_spec,
        out_shape=jax.ShapeDtypeStruct((nb, T_PAD, D_ATT), BF16),
        compiler_params=_params("arbitrary"),
        name="paged_attention",
    )(page_table, q_rep, ckv_new8, kpe_new_t, wq, wuv, cache_ckv, cache_kpe_t)


def _group_norm_gate(o, gate, gain, bias):
    mu = jnp.mean(o, axis=-1, keepdims=True)
    d = o - mu
    var = jnp.mean(d * d, axis=-1, keepdims=True)
    return _silu(gate) * (d * lax.rsqrt(var + LN_EPS) * gain + bias)


def _ret_kernel(rq_ref, rk_ref, rv_ref, rg_ref, din_ref, qd_ref, kd_ref, gc_ref, gg_ref, gb_ref, s0_ref,
                ro_ref, so_ref, s_ref, *, chunks):
    i = pl.program_id(1)

    @pl.when(i == 0)
    def _():
        s_ref[...] = s0_ref[0]

    units = [(c, hh) for c in range(chunks) for hh in range(RET_HEADS)]

    def tile(ref, c, hh):
        return ref[c * RET_CHUNK:(c + 1) * RET_CHUNK, hh * RET_DK:(hh + 1) * RET_DK]

    def lanes(ref, hh):
        return ref[:, hh * RET_DK:(hh + 1) * RET_DK]

    inner = {u: (lax.dot_general(tile(rq_ref, *u).astype(BF16), tile(rk_ref, *u).astype(BF16), _NT,
                                 preferred_element_type=F32) * lanes(din_ref, u[1])).astype(BF16)
             for u in units}
    kv = {u: lax.dot_general((tile(rk_ref, *u) * lanes(kd_ref, u[1])).astype(BF16), tile(rv_ref, *u), _TN,
                             preferred_element_type=F32)
          for u in units}
    o_intra = {u: jnp.dot(inner[u], tile(rv_ref, *u), preferred_element_type=F32) for u in units}
    states = [s_ref[hh] for hh in range(RET_HEADS)]
    for c, hh in units:
        s = states[hh]
        o = o_intra[c, hh] + jnp.dot((tile(rq_ref, c, hh) * lanes(qd_ref, hh)).astype(BF16), s.astype(BF16),
                                     preferred_element_type=F32)
        states[hh] = s * lanes(gc_ref, hh) + kv[c, hh]
        ro_ref[c * RET_CHUNK:(c + 1) * RET_CHUNK, hh * RET_DK:(hh + 1) * RET_DK] = _group_norm_gate(
            o, tile(rg_ref, c, hh), lanes(gg_ref, hh), lanes(gb_ref, hh)).astype(BF16)
    for hh in range(RET_HEADS):
        s_ref[hh] = states[hh]

    @pl.when(i == pl.num_programs(1) - 1)
    def _():
        so_ref[0] = s_ref[...]


def _retention_prompt(rq, rk, rv, rg, dec, gn_gain, gn_bias, state0, batch, seq, chunks):
    rows = chunks * RET_CHUNK
    steps = seq // rows
    rspec = pl.BlockSpec((rows, D_RET), lambda b, i: (b * steps + i, 0))
    sspec = pl.BlockSpec((1, RET_HEADS, RET_DK, RET_DV), lambda b, i: (b, 0, 0, 0))
    consts = [dec["din"], dec["qdec"], dec["kdec"], dec["gc"], gn_gain, gn_bias]
    return pl.pallas_call(
        functools.partial(_ret_kernel, chunks=chunks),
        grid=(batch, steps),
        in_specs=[rspec] * 4 + [_const_spec(c.shape) for c in consts] + [sspec],
        out_specs=[rspec, sspec],
        out_shape=[jax.ShapeDtypeStruct((batch * seq, D_RET), BF16),
                   jax.ShapeDtypeStruct((batch, RET_HEADS, RET_DK, RET_DV), F32)],
        scratch_shapes=[pltpu.VMEM((RET_HEADS, RET_DK, RET_DV), F32)],
        compiler_params=_params("parallel", "arbitrary"),
        name="retention_prompt",
    )(rq, rk, rv, rg, *consts, state0)


RS_BATCH = 8


def _ret_sample_kernel(q_ref, k_ref, v_ref, rg_ref, kt_ref, qd_ref, kd_ref, din_ref, gc_ref, gg_ref, gb_ref,
                       s0_ref, ro_ref, so_ref, *, n_tok):
    def one_seq(bi, carry):
        q8 = q_ref[bi]
        k8 = k_ref[bi]
        v8 = v_ref[bi]
        g8 = rg_ref[bi]
        qs = (q8 * qd_ref[...]).astype(BF16)
        heads = [slice(hh * RET_DK, (hh + 1) * RET_DK) for hh in range(RET_HEADS)]
        cross = [jnp.dot(qs[:, sl], s0_ref[bi, hh].astype(BF16), preferred_element_type=F32)
                 for hh, sl in enumerate(heads)]
        vbd = jnp.concatenate([v8 * kd_ref[hh] for hh in range(RET_HEADS)]
                              + [jnp.zeros((LANES - RET_HEADS * T_PAD, D_RET), F32)], axis=0)
        upd = jnp.dot(kt_ref[bi].astype(BF16), vbd.astype(BF16), preferred_element_type=F32)
        outs = []
        for hh, sl in enumerate(heads):
            o = cross[hh]
            din = din_ref[hh]
            for m in range(n_tok):
                a = jnp.sum(q8[:, sl] * k8[m:m + 1, sl], axis=-1, keepdims=True) * din[:, m:m + 1]
                o = o + a * v8[m:m + 1, sl]
            outs.append(_group_norm_gate(o, g8[:, sl], gg_ref[:, sl], gb_ref[:, sl]))
            so_ref[bi, hh] = s0_ref[bi, hh] * gc_ref[:, sl] + upd[:, sl]
        ro_ref[bi] = jnp.concatenate(outs, axis=1).astype(BF16)
        return carry

    lax.fori_loop(0, RS_BATCH, one_seq, 0)


def _retention_sample(q8, k8, v8, rg8, k_t, dec, gn_gain, gn_bias, state0, n_tok):
    nb = q8.shape[0]
    rspec = pl.BlockSpec((RS_BATCH, T_PAD, D_RET), lambda i: (i, 0, 0))
    sspec = pl.BlockSpec((RS_BATCH, RET_HEADS, RET_DK, RET_DV), lambda i: (i, 0, 0, 0))
    consts = [dec["qdec"], dec["kdec"], dec["din"], dec["gc"], gn_gain, gn_bias]
    return pl.pallas_call(
        functools.partial(_ret_sample_kernel, n_tok=n_tok),
        grid=(nb // RS_BATCH,),
        in_specs=[rspec] * 4 + [pl.BlockSpec((RS_BATCH, RET_DK, LANES), lambda i: (i, 0, 0))]
                 + [_const_spec(c.shape) for c in consts] + [sspec],
        out_specs=[rspec, sspec],
        out_shape=[jax.ShapeDtypeStruct((nb, T_PAD, D_RET), BF16),
                   jax.ShapeDtypeStruct((nb, RET_HEADS, RET_DK, RET_DV), F32)],
        compiler_params=_params("parallel"),
        name="retention_sample",
    )(q8, k8, v8, rg8, k_t, *consts, state0)


def _outproj_ln_kernel(a_ref, ro_ref, h_ref, wa_ref, wr_ref, g_ref, b_ref, o_ref):
    y = (jnp.dot(a_ref[...], wa_ref[...], preferred_element_type=F32)
         + jnp.dot(ro_ref[...], wr_ref[...], preferred_element_type=F32))
    o_ref[...] = _layer_norm(ALPHA * h_ref[...] + y, g_ref[...], b_ref[...])


def _outproj_ln(a, ro, h, wa, wr, g, b, tm):
    m = h.shape[0]

    def row(n):
        return pl.BlockSpec((tm, n), lambda i: (i, 0))

    return pl.pallas_call(
        _outproj_ln_kernel,
        grid=(m // tm,),
        in_specs=[row(D_ATT), row(D_RET), row(D_MODEL), _const_spec(wa.shape), _const_spec(wr.shape),
                  _const_spec(g.shape), _const_spec(b.shape)],
        out_specs=row(D_MODEL),
        out_shape=jax.ShapeDtypeStruct((m, D_MODEL), F32),
        compiler_params=_params("parallel"),
        name="outproj_ln",
    )(a, ro, h, wa, wr, g, b)


HALO = 16


def _pool_tail(pooled_groups, x, pw_ref, pb_ref, ps_ref, g_ref, b_ref):
    ys = [jnp.dot(p.astype(BF16), pw_ref[gi], preferred_element_type=F32) for gi, p in enumerate(pooled_groups)]
    y = (jnp.concatenate(ys, axis=-1) + pb_ref[...]) * ps_ref[...]
    return _layer_norm(ALPHA * x + y, g_ref[...], b_ref[...])


HALO_P = 32


def _pool_prompt_kernel(h_ref, halo_ref, pre_ref, pw_ref, pb_ref, ps_ref, g_ref, b_ref, o_ref,
                        xs_ref, a_ref, b2_ref, *, tm, tiles, start):
    t = pl.program_id(0) % tiles
    x = h_ref[...]
    xs_ref[0:HALO_P, :] = jnp.where(t == 0, pre_ref[0], halo_ref[...])
    xs_ref[HALO_P:, :] = x
    n = HALO_P + tm
    pos = start + t * tm + lax.broadcasted_iota(jnp.int32, (tm, 1), 0)
    pooled = []
    src = xs_ref
    for k, wl in enumerate(POOL_WINDOWS, start=1):
        lo = (k - 1) * POOL_GROUP
        r0, shift = 8 * k, wl // 2
        level = src[r0:n, lo:] + src[r0 - shift:n - shift, lo:]
        cnt = jnp.minimum(pos + 1, wl).astype(F32)
        pooled.append(level[HALO_P - r0:, :POOL_GROUP] / cnt - x[:, lo:lo + POOL_GROUP])
        if k < len(POOL_WINDOWS):
            dst = a_ref if k % 2 else b2_ref
            dst[r0:n, lo + POOL_GROUP:] = level[:, POOL_GROUP:]
            src = dst
    o_ref[...] = _pool_tail(pooled, x, pw_ref, pb_ref, ps_ref, g_ref, b_ref)


def _pool_prompt(h, prefix, pw, pb, ps, g, b, batch, seq, start, tm):
    assert all(w == 2 ** (i + 1) for i, w in enumerate(POOL_WINDOWS)) and 8 * len(POOL_WINDOWS) <= HALO_P
    tiles = seq // tm
    per = tm // HALO_P
    consts = [pw, pb, ps, g, b]
    buf = pltpu.VMEM((HALO_P + tm, D_MODEL), F32)
    return pl.pallas_call(
        functools.partial(_pool_prompt_kernel, tm=tm, tiles=tiles, start=start),
        grid=(batch * tiles,),
        in_specs=[pl.BlockSpec((tm, D_MODEL), lambda i: (i, 0)),
                  pl.BlockSpec((HALO_P, D_MODEL), lambda i: (jnp.maximum(i * per - 1, 0), 0)),
                  pl.BlockSpec((1, HALO_P, D_MODEL), lambda i: (i // tiles, 0, 0))]
                 + [_const_spec(c.shape) for c in consts],
        out_specs=pl.BlockSpec((tm, D_MODEL), lambda i: (i, 0)),
        out_shape=jax.ShapeDtypeStruct((batch * seq, D_MODEL), F32),
        scratch_shapes=[buf, buf, buf],
        compiler_params=_params("parallel"),
        name="pool_prompt",
    )(h, h, prefix, *consts)


def _pool_sample_kernel(xs_ref, pw_ref, pb_ref, ps_ref, g_ref, b_ref, o_ref, *, n_tok, start):
    nb = xs_ref.shape[1]
    x = jnp.concatenate([xs_ref[HALO + t] for t in range(n_tok)], axis=0)
    pooled = []
    for gi, wl in enumerate(POOL_WINDOWS):
        sl = slice(gi * POOL_GROUP, (gi + 1) * POOL_GROUP)
        parts = []
        for t in range(n_tok):
            acc = xs_ref[HALO + t, :, sl]
            for d in range(1, wl):
                acc = acc + xs_ref[HALO + t - d, :, sl]
            parts.append(acc / float(min(start + t + 1, wl)))
        pooled.append(jnp.concatenate(parts, axis=0) - x[:, sl])
    y = _pool_tail(pooled, x, pw_ref, pb_ref, ps_ref, g_ref, b_ref)
    for t in range(n_tok):
        o_ref[t] = y[t * nb:(t + 1) * nb]


def _pool_sample(xs_t, pw, pb, ps, g, b, n_tok, start):
    nb = xs_t.shape[1]
    args = [xs_t, pw, pb, ps, g, b]
    return pl.pallas_call(
        functools.partial(_pool_sample_kernel, n_tok=n_tok, start=start),
        grid=(1,),
        in_specs=[_const_spec(a.shape) for a in args],
        out_specs=_const_spec((n_tok, nb, D_MODEL)),
        out_shape=jax.ShapeDtypeStruct((n_tok, nb, D_MODEL), F32),
        compiler_params=_params("arbitrary"),
        name="pool_sample",
    )(*args)


def _head_pad(w, width):
    r, nh, d = w.shape
    out = jnp.zeros((r, nh, HEAD_PAD), w.dtype).at[:, :, :d].set(w)
    return out.reshape(r, nh * HEAD_PAD)[:, :width]


def _mixer_weights(mix_w_in, q_norm, kv_norm, w_uq, w_uk, w_uv, mix_w_out):
    offs = np.concatenate([[0], np.cumsum(SPLIT_SIZES)])
    wq, wckv, wkpe, wrq, wrk, wrv, wrg = [mix_w_in[:, offs[i]:offs[i + 1]] for i in range(7)]
    half = QK_ROPE // 2
    z_lo = jnp.zeros((D_MODEL, QK_NOPE), F32)
    z_hi = jnp.zeros((D_MODEL, HEAD_PAD - QK_NOPE - QK_ROPE), F32)
    kpe_blk = jnp.concatenate([z_lo, wkpe, z_hi], axis=1)
    kpe_swp = jnp.concatenate([z_lo, -wkpe[:, half:], wkpe[:, :half], z_hi], axis=1)
    w_main = jnp.concatenate([wq, wckv, kpe_blk, kpe_swp, wrq, wrk, wrv, wrg], axis=1).astype(BF16)
    pe = w_uq[:, :, QK_NOPE:]
    uq1 = _head_pad(w_uq, D_HEADS)
    uq2 = _head_pad(jnp.concatenate([jnp.zeros_like(w_uq[:, :, :QK_NOPE]), -pe[:, :, half:], pe[:, :, :half]], axis=2),
                    D_HEADS)
    wk = _head_pad(w_uk, D_HEADS)
    wv = _head_pad(w_uv, D_HEADS)
    wa = mix_w_out[:D_ATT]
    wr = mix_w_out[MLA_HEADS * V_DIM:]
    wuk_t = jnp.zeros((MLA_HEADS, HEAD_PAD, KV_LORA), F32).at[:, :QK_NOPE, :].set(
        jnp.transpose(w_uk, (1, 2, 0))).reshape(D_HEADS, KV_LORA)
    r = np.arange(D_HEADS) % HEAD_PAD
    e_pe = ((r[:, None] - QK_NOPE) == np.arange(LANES)[None, :]) & (r[:, None] >= QK_NOPE) & (r[:, None] < QK_NOPE + QK_ROPE)
    wq_abs = jnp.concatenate([wuk_t, jnp.asarray(e_pe.astype(np.float32))], axis=1)
    return {
        "w_main": w_main, "q_norm": q_norm[None, :], "kv_norm": kv_norm[None, :],
        "uq1": uq1.astype(BF16), "uq2": uq2.astype(BF16), "wk": wk.astype(BF16),
        "wv_c": w_uv.reshape(KV_LORA, D_ATT).astype(BF16),
        "wv_t": wv.T.astype(BF16), "wa": wa.astype(BF16), "wr": wr.astype(BF16),
        "wq_abs": wq_abs.astype(BF16),
    }


def _rope_tables(pos):
    def angles(r):
        inv = 1.0 / (ROPE_BASE ** (np.arange(0, r, 2, dtype=np.float64) / r))
        return pos.astype(np.float64)[:, None] * inv[None, :]

    n = pos.shape[0]
    a = angles(QK_ROPE)
    c, s = np.cos(a), np.sin(a)
    hi = HEAD_PAD - QK_NOPE - QK_ROPE
    cq = np.concatenate([np.ones((n, QK_NOPE)), c, c, np.ones((n, hi))], axis=1)
    sq = np.concatenate([np.zeros((n, QK_NOPE)), s, s, np.zeros((n, hi))], axis=1)
    a = angles(RET_DK)
    c, s = np.cos(a), np.sin(a)
    tabs = (cq, sq, np.concatenate([c, c], axis=1), np.concatenate([-s, s], axis=1))
    return tuple(jnp.asarray(t, F32) for t in tabs)


def _ret_log_decay():
    return jnp.log(1.0 - 2.0 ** (-5.0 - jnp.arange(RET_HEADS, dtype=F32)))


def _ret_decay_tables(chunk):
    log_g = _ret_log_decay()
    idx = jnp.arange(chunk, dtype=F32)
    diff = idx[:, None] - idx[None, :]
    d_in = jnp.where(diff >= 0, jnp.exp(jnp.maximum(diff, 0.0)[None] * log_g[:, None, None]), 0.0)
    q_dec = jnp.exp((idx + 1.0)[None, :] * log_g[:, None])
    k_dec = jnp.exp((chunk - 1.0 - idx)[None, :] * log_g[:, None])
    g_c = jnp.exp(chunk * log_g)
    lanes = (chunk, RET_HEADS * RET_DK)
    return {
        "din": jnp.transpose(d_in, (1, 0, 2)).reshape(chunk, RET_HEADS * chunk),
        "qdec": jnp.broadcast_to(q_dec.T[:, :, None], (chunk, RET_HEADS, RET_DK)).reshape(lanes),
        "kdec": jnp.broadcast_to(k_dec.T[:, :, None], (chunk, RET_HEADS, RET_DK)).reshape(lanes),
        "gc": jnp.broadcast_to(g_c[:, None], (RET_HEADS, RET_DV)).reshape(1, RET_HEADS * RET_DV),
    }


def _ret_sample_tables(n_tok):
    log_g = _ret_log_decay()
    idx = jnp.arange(n_tok, dtype=F32)
    diff = idx[:, None] - idx[None, :]
    d_in = jnp.where(diff >= 0, jnp.exp(jnp.maximum(diff, 0.0)[None] * log_g[:, None, None]), 0.0)
    q_dec = jnp.exp((idx + 1.0)[None, :] * log_g[:, None])
    k_dec = jnp.exp((n_tok - 1.0 - idx)[None, :] * log_g[:, None])
    g_c = jnp.exp(n_tok * log_g)
    pad_t = T_PAD - n_tok
    own_lanes = (np.arange(D_RET) // RET_DV)[None, None, :] == np.arange(RET_HEADS)[:, None, None]
    return {
        "qdec": jnp.pad(jnp.broadcast_to(q_dec.T[:, :, None], (n_tok, RET_HEADS, RET_DK)).reshape(n_tok, D_RET),
                        ((0, pad_t), (0, 0))),
        "kdec": jnp.pad(k_dec[:, :, None] * jnp.asarray(own_lanes, F32), ((0, 0), (0, pad_t), (0, 0))),
        "din": jnp.pad(d_in, ((0, 0), (0, pad_t), (0, LANES - n_tok))),
        "gc": jnp.broadcast_to(g_c[:, None], (RET_HEADS, RET_DV)).reshape(1, D_RET),
    }


FFN_ORDER = ((0, 0), (0, 1), (1, 0), (1, 1))


def _trunk(x, start, ret_state0, pool_prefix, mla_cache, w, mw):
    batch, seq, _ = x.shape
    m = batch * seq
    prompt = mla_cache is None
    tm = min(1024, m)
    h = x.reshape(m, D_MODEL)

    h = yield h
    pos = start + (np.arange(seq) if prompt else np.arange(m) % seq)
    flash_blk = min(1024, seq) if prompt else tm
    q, k, vt, ckv, kpe, rq, rk, rv, rg = _mixer_prep(h, mw, _rope_tables(pos), tm, flash_blk)
    kpe = kpe[:, QK_NOPE:QK_NOPE + QK_ROPE]
    gn_gain = w["ret_gn_gain"][0][None, :]
    gn_bias = w["ret_gn_bias"][0][None, :]
    if prompt:
        a = _flash_attention(q, k, vt, batch, seq, flash_blk)
        chunk = RET_CHUNK if seq % RET_CHUNK == 0 else seq
        assert chunk == RET_CHUNK
        ro, ret_state = _retention_prompt(rq, rk, rv, rg, _ret_decay_tables(chunk), gn_gain, gn_bias,
                                          ret_state0, batch, seq, 8)
    else:
        assert seq <= T_PAD and seq % RET_CHUNK != 0
        cache_ckv, cache_kpe, page_table = mla_cache
        pad_t = ((0, 0), (0, T_PAD - seq), (0, 0))
        tq = seq if (MLA_HEADS * seq) % 16 == 0 else T_PAD
        q8 = jnp.tile(jnp.pad(q.reshape(batch, seq, D_HEADS), ((0, 0), (0, tq - seq), (0, 0))),
                      (1, MLA_HEADS, 1))
        cn8 = jnp.pad(ckv.reshape(batch, seq, KV_LORA), pad_t)
        kn_t = jnp.pad(jnp.swapaxes(kpe.reshape(batch, seq, QK_ROPE), 1, 2), ((0, 0), (0, 0), (0, NEW_PAD - seq)))
        a8 = _paged_attention(page_table, q8, cn8, kn_t, mw["wq_abs"], mw["wv_c"], cache_ckv,
                              jnp.swapaxes(cache_kpe, 1, 2), seq)
        a = a8[:, :seq].reshape(m, D_ATT)

        def rows8(t):
            return jnp.pad(t.reshape(batch, seq, D_RET), pad_t)

        k8 = rows8(rk)
        k_t = jnp.transpose(k8.reshape(batch, T_PAD, RET_HEADS, RET_DK), (0, 3, 2, 1)).reshape(
            batch, RET_DK, RET_HEADS * T_PAD)
        k_t = jnp.pad(k_t, ((0, 0), (0, 0), (0, LANES - RET_HEADS * T_PAD)))
        ro8, ret_state = _retention_sample(rows8(rq), k8, rows8(rv.astype(F32)), rows8(rg), k_t,
                                           _ret_sample_tables(seq), gn_gain, gn_bias, ret_state0, seq)
        ro = ro8[:, :seq].reshape(m, D_RET)
    h = _outproj_ln(a, ro, h, mw["wa"], mw["wr"], w["ln_gain"][0, 1][None, :], w["ln_bias"][0, 1][None, :], tm)
    h = yield h

    h = yield h
    xp_tail = jnp.concatenate([pool_prefix, h.reshape(batch, seq, D_MODEL)], axis=1)[:, -POOL_PREFIX:]
    halo = HALO_P if prompt else HALO
    prefix16 = jnp.pad(pool_prefix, ((0, 0), (halo - POOL_PREFIX, 0), (0, 0)))
    pool_args = (w["pool_w"], w["pool_b"][0][None, :], w["pool_scale"][0][None, :],
                 w["ln_gain"][1, 1][None, :], w["ln_bias"][1, 1][None, :])
    if prompt:
        h = _pool_prompt(h, prefix16, *pool_args, batch, seq, start, tm)
    else:
        xs_t = jnp.transpose(jnp.concatenate([prefix16, h.reshape(batch, seq, D_MODEL)], axis=1), (1, 0, 2))
        h = jnp.transpose(_pool_sample(xs_t, *pool_args, seq, start), (1, 0, 2)).reshape(m, D_MODEL)
    h = yield h
    return (h.reshape(batch, seq, D_MODEL), ckv.reshape(1, batch, seq, KV_LORA),
            kpe.reshape(1, batch, seq, QK_ROPE), ret_state[None], xp_tail[None])


def _finish(trunk, last):
    try:
        trunk.send(last)
    except StopIteration as done:
        return done.value
    raise AssertionError("trunk yielded more FFN requests than FFN_ORDER")


def kernel(x_prompt, x_sample, cache_mla_ckv, cache_mla_kpe, state_ret, state_pool, page_table, ffn_w_gate, ffn_w_up, ffn_w_down, ln_gain, ln_bias, mix_w_in, mla_q_norm, mla_kv_norm, mla_w_uq, mla_w_uk, mla_w_uv, ret_gn_gain, ret_gn_bias, mix_w_out, pool_w, pool_b, pool_scale):
    assert DEPTH == 2 and mix_w_in.shape[0] == 1 and pool_w.shape[0] == 1
    w = {
        "wg": ffn_w_gate, "wu": ffn_w_up, "wd": ffn_w_down,
        "ln_gain": ln_gain, "ln_bias": ln_bias, "ret_gn_gain": ret_gn_gain, "ret_gn_bias": ret_gn_bias,
        "pool_w": pool_w[0].astype(BF16), "pool_b": pool_b, "pool_scale": pool_scale,
    }
    mw = _mixer_weights(mix_w_in[0], mla_q_norm[0], mla_kv_norm[0], mla_w_uq[0], mla_w_uk[0], mla_w_uv[0],
                        mix_w_out[0])
    bp = x_prompt.shape[0]
    zero_ret = jnp.zeros((bp, RET_HEADS, RET_DK, RET_DV), F32)
    zero_pool = jnp.zeros((bp, POOL_PREFIX, D_MODEL), x_prompt.dtype)
    trunk_p = _trunk(x_prompt, 0, zero_ret, zero_pool, None, w, mw)
    trunk_s = _trunk(x_sample, PAST_LEN, state_ret[0], state_pool[0],
                     (cache_mla_ckv[0], cache_mla_kpe[0], page_table), w, mw)
    h_p, h_s = next(trunk_p), next(trunk_s)
    for n, (layer, half) in enumerate(FFN_ORDER):
        f_p, f_s = _ffn_ln(h_p, h_s, w["wg"], w["wu"], w["wd"], layer, half,
                           ln_gain[layer, 2 * half][None, :], ln_bias[layer, 2 * half][None, :])
        if n + 1 < len(FFN_ORDER):
            h_p, h_s = trunk_p.send(f_p), trunk_s.send(f_s)
    y_p, ckv_p, kpe_p, ret_p, pool_p = _finish(trunk_p, f_p)
    y_s, ckv_s, kpe_s, ret_s, pool_s = _finish(trunk_s, f_s)
    return (y_p, y_s, ckv_p, kpe_p, ckv_s, kpe_s, ret_p, ret_s, pool_p, pool_s)
```
